```python
import jax, jax.numpy as jnp
from jax import lax
import numpy as np

D_MODEL = 4096
BATCH = 2
SEQ = 4096
DEPTH = 1

CTX_LEN = 256
GRID_W = 64
MIX_WIDTH = D_MODEL
RET_WIDTH = MIX_WIDTH // 2
CONV_WIDTH = MIX_WIDTH - RET_WIDTH
RET_HEADS = 8
RET_HEAD_DIM = RET_WIDTH // RET_HEADS
RET_CHUNK = 128
CONV_KERNEL = 31
ROPE_BASE = 10000.0
NORM_EPS = 1e-6
IN_COLS = 4 * RET_WIDTH + 3 * CONV_WIDTH

kernel_name = "hymba_retention_conformer_dit_block"


def rms_norm(x, g):
    xf = x.astype(jnp.float32)
    var = jnp.mean(xf * xf, axis=-1, keepdims=True)
    return (xf * lax.rsqrt(var + NORM_EPS) * g.astype(jnp.float32)).astype(x.dtype)


def layer_norm(x, g, b):
    xf = x.astype(jnp.float32)
    mu = jnp.mean(xf, axis=-1, keepdims=True)
    var = jnp.mean(jnp.square(xf - mu), axis=-1, keepdims=True)
    y = (xf - mu) * lax.rsqrt(var + NORM_EPS) * g.astype(jnp.float32) + b.astype(jnp.float32)
    return y.astype(x.dtype)


def head_group_norm(y, g):
    mu = jnp.mean(y, axis=-1, keepdims=True)
    var = jnp.mean(jnp.square(y - mu), axis=-1, keepdims=True)
    yn = (y - mu) * lax.rsqrt(var + NORM_EPS)
    b, h, n, d = y.shape
    return yn.transpose(0, 2, 1, 3).reshape(b, n, h * d) * g.astype(jnp.float32)


def to_heads(t):
    b, n, _ = t.shape
    return t.reshape(b, n, RET_HEADS, RET_HEAD_DIM).transpose(0, 2, 1, 3)


def grid_rope_tables(n_tokens):
    rows = n_tokens // GRID_W
    pos_r = jnp.repeat(jnp.arange(rows, dtype=jnp.float32), GRID_W)
    pos_c = jnp.tile(jnp.arange(GRID_W, dtype=jnp.float32), rows)
    n_freq = RET_HEAD_DIM // 4
    inv_freq = ROPE_BASE ** (-jnp.arange(n_freq, dtype=jnp.float32) / n_freq)
    ang_r = pos_r[:, None] * inv_freq[None, :]
    ang_c = pos_c[:, None] * inv_freq[None, :]
    return (jnp.cos(ang_r), jnp.sin(ang_r), jnp.cos(ang_c), jnp.sin(ang_c))


def _rotate(xh, cos, sin):
    half = xh.shape[-1] // 2
    x1, x2 = xh[..., :half], xh[..., half:]
    cos = cos.astype(xh.dtype)
    sin = sin.astype(xh.dtype)
    return jnp.concatenate([x1 * cos - x2 * sin, x2 * cos + x1 * sin], axis=-1)


def apply_grid_rope(t, rope):
    cos_r, sin_r, cos_c, sin_c = rope
    half = t.shape[-1] // 2
    return jnp.concatenate([_rotate(t[..., :half], cos_r, sin_r),
                            _rotate(t[..., half:], cos_c, sin_c)], axis=-1)


def retention_direction(q, k, v, log_g, r0):
    q = q.astype(jnp.float32)
    k = k.astype(jnp.float32)
    v = v.astype(jnp.float32)
    b, h, n, dk = q.shape
    dv = v.shape[-1]
    nc = n // RET_CHUNK
    qc = q.reshape(b, h, nc, RET_CHUNK, dk)
    kc = k.reshape(b, h, nc, RET_CHUNK, dk)
    vc = v.reshape(b, h, nc, RET_CHUNK, dv)
    idx = jnp.arange(RET_CHUNK, dtype=jnp.float32)
    lg = log_g.astype(jnp.float32)[:, None]
    rel = idx[:, None] - idx[None, :]
    intra = jnp.where(rel >= 0, jnp.exp(lg[:, :, None] * jnp.maximum(rel, 0.0)), 0.0)
    scores = jnp.einsum('bhnid,bhnjd->bhnij', qc, kc) * intra[None, :, None]
    inner = jnp.einsum('bhnij,bhnje->bhnie', scores, vc)
    k_w = jnp.exp(lg * (RET_CHUNK - 1.0 - idx))
    chunk_kv = jnp.einsum('bhnjd,bhnje,hj->nbhde', kc, vc, k_w)
    chunk_decay = jnp.exp(lg[:, 0] * RET_CHUNK)[None, :, None, None]

    def step(r, s):
        return r * chunk_decay + s, r

    r_last, r_prev = lax.scan(step, r0.astype(jnp.float32), chunk_kv)
    q_w = jnp.exp(lg * (idx + 1.0))
    cross = jnp.einsum('bhnid,hi,nbhde->bhnie', qc, q_w, r_prev)
    return (inner + cross).reshape(b, h, n, dv), r_last


def retention_final_state(k, v, log_g, reverse):
    k = k.astype(jnp.float32)
    v = v.astype(jnp.float32)
    n = k.shape[2]
    pos = jnp.arange(n, dtype=jnp.float32)
    steps = pos if reverse else (n - 1.0 - pos)
    w = jnp.exp(log_g.astype(jnp.float32)[:, None] * steps[None, :])
    return jnp.einsum('bhnd,bhne,hn->bhde', k, v, w)


def depthwise_conv(u, w, b):
    ch = u.shape[-1]
    rhs = w.astype(u.dtype)[:, None, :]
    y = lax.conv_general_dilated(u, rhs, window_strides=(1,),
                                 padding=[(CONV_KERNEL // 2, CONV_KERNEL // 2)],
                                 dimension_numbers=('NWC', 'WIO', 'NWC'),
                                 feature_group_count=ch)
    return y + b.astype(u.dtype)


def mixer_output(ret_y, proj, conv_dw_w, conv_dw_b, conv_ln_g, conv_ln_b, w_out):
    dt = proj.dtype
    r = RET_WIDTH
    cw = CONV_WIDTH
    g_ret = proj[..., 3 * r:4 * r]
    a_conv = proj[..., 4 * r:4 * r + cw]
    b_conv = proj[..., 4 * r + cw:4 * r + 2 * cw]
    g_conv = proj[..., 4 * r + 2 * cw:]
    ret_branch = ret_y.astype(dt) * jax.nn.silu(g_ret)
    u = a_conv * jax.nn.sigmoid(b_conv)
    u = depthwise_conv(u, conv_dw_w, conv_dw_b)
    u = jax.nn.silu(layer_norm(u, conv_ln_g, conv_ln_b))
    conv_branch = u * jax.nn.silu(g_conv)
    return jnp.concatenate([ret_branch, conv_branch], axis=-1) @ w_out


def hybrid_layer(x, ctx, c, c_ctx, ada_w, ada_b, pre_g, post_g, w_in, lg_f, lg_b,
                 ret_gn_g, conv_dw_w, conv_dw_b, conv_ln_g, conv_ln_b, w_out, rope, update_ctx):
    r = RET_WIDTH
    shift, scale, gate = jnp.split(jax.nn.silu(c) @ ada_w + ada_b, 3, axis=-1)
    shift_c, scale_c, gate_c = jnp.split(jax.nn.silu(c_ctx) @ ada_w + ada_b, 3, axis=-1)
    h = rms_norm(x, pre_g) * (1.0 + scale[:, None]) + shift[:, None]
    hc = rms_norm(ctx, pre_g) * (1.0 + scale_c) + shift_c

    proj = h @ w_in
    q = apply_grid_rope(to_heads(proj[..., :r]), rope)
    k = apply_grid_rope(to_heads(proj[..., r:2 * r]), rope) * (RET_HEAD_DIM ** -0.5)
    v = to_heads(proj[..., 2 * r:3 * r])

    b = x.shape[0]
    zero_state = jnp.zeros((b, RET_HEADS, RET_HEAD_DIM, RET_HEAD_DIM), jnp.float32)
    if update_ctx:
        proj_c = hc @ w_in
        q_c = to_heads(proj_c[..., :r])
        k_c = to_heads(proj_c[..., r:2 * r]) * (RET_HEAD_DIM ** -0.5)
        v_c = to_heads(proj_c[..., 2 * r:3 * r])
        ctx_f, r_cf = retention_direction(q_c, k_c, v_c, lg_f, zero_state)
        ctx_b_rev, r_cb = retention_direction(jnp.flip(q_c, 2), jnp.flip(k_c, 2),
                                              jnp.flip(v_c, 2), lg_b, zero_state)
    else:
        kv_c = hc @ w_in[:, r:3 * r]
        k_c = to_heads(kv_c[..., :r]) * (RET_HEAD_DIM ** -0.5)
        v_c = to_heads(kv_c[..., r:])
        r_cf = retention_final_state(k_c, v_c, lg_f, reverse=False)
        r_cb = retention_final_state(k_c, v_c, lg_b, reverse=True)

    lat_f, _ = retention_direction(q, k, v, lg_f, r_cf)
    lat_b_rev, _ = retention_direction(jnp.flip(q, 2), jnp.flip(k, 2), jnp.flip(v, 2), lg_b, r_cb)
    ret = head_group_norm(lat_f + jnp.flip(lat_b_rev, 2), ret_gn_g)
    y = mixer_output(ret, proj, conv_dw_w, conv_dw_b, conv_ln_g, conv_ln_b, w_out)
    x_new = x + gate[:, None] * rms_norm(y, post_g)

    if update_ctx:
        ret_c = head_group_norm(ctx_f + jnp.flip(ctx_b_rev, 2), ret_gn_g)
        y_c = mixer_output(ret_c, proj_c, conv_dw_w, conv_dw_b, conv_ln_g, conv_ln_b, w_out)
        ctx = ctx + gate_c * rms_norm(y_c, post_g)
    return x_new, ctx


def setup_inputs(seed: int = 0) -> dict:
    key = jax.random.key(seed)
    ks = jax.random.split(key, 18)
    f32 = jnp.float32
    nrm = lambda k, s: jax.random.normal(k, s, f32)
    base_decay = jnp.log1p(-(2.0 ** (-5.0 - jnp.arange(RET_HEADS, dtype=f32))))
    return {
        "x": nrm(ks[0], (BATCH, SEQ, D_MODEL)),
        "c": nrm(ks[1], (BATCH, D_MODEL)),
        "ctx": nrm(ks[2], (BATCH, CTX_LEN, D_MODEL)),
        "c_ctx": nrm(ks[3], (D_MODEL,)),
        "ada_w": nrm(ks[4], (DEPTH, D_MODEL, 3 * D_MODEL)) * (0.5 * D_MODEL ** -0.5),
        "ada_b": 0.01 * nrm(ks[5], (DEPTH, 3 * D_MODEL)),
        "pre_norm_g": 1.0 + 0.05 * nrm(ks[6], (DEPTH, D_MODEL)),
        "post_norm_g": 1.0 + 0.05 * nrm(ks[7], (DEPTH, D_MODEL)),
        "w_in": nrm(ks[8], (DEPTH, D_MODEL, IN_COLS)) * (D_MODEL ** -0.5),
        "ret_log_decay_fwd": base_decay[None, :] * (1.0 + 0.1 * nrm(ks[9], (DEPTH, RET_HEADS))),
        "ret_log_decay_bwd": base_decay[None, :] * (1.0 + 0.1 * nrm(ks[10], (DEPTH, RET_HEADS))),
        "ret_gn_g": 1.0 + 0.05 * nrm(ks[11], (DEPTH, RET_WIDTH)),
        "conv_dw_w": nrm(ks[12], (DEPTH, CONV_KERNEL, CONV_WIDTH)) * (CONV_KERNEL ** -0.5),
        "conv_dw_b": 0.01 * nrm(ks[13], (DEPTH, CONV_WIDTH)),
        "conv_ln_g": 1.0 + 0.05 * nrm(ks[14], (DEPTH, CONV_WIDTH)),
        "conv_ln_b": 0.01 * nrm(ks[15], (DEPTH, CONV_WIDTH)),
        "w_out": nrm(ks[16], (DEPTH, MIX_WIDTH, D_MODEL)) * (MIX_WIDTH ** -0.5),
    }


def reference(x, c, ctx, c_ctx, ada_w, ada_b, pre_norm_g, post_norm_g, w_in,
              ret_log_decay_fwd, ret_log_decay_bwd, ret_gn_g, conv_dw_w, conv_dw_b,
              conv_ln_g, conv_ln_b, w_out):
    rope = grid_rope_tables(x.shape[1])
    for layer in range(DEPTH):
        x, ctx = hybrid_layer(
            x, ctx, c, c_ctx, ada_w[layer], ada_b[layer], pre_norm_g[layer], post_norm_g[layer],
            w_in[layer], ret_log_decay_fwd[layer], ret_log_decay_bwd[layer], ret_gn_g[layer],
            conv_dw_w[layer], conv_dw_b[layer], conv_ln_g[layer], conv_ln_b[layer], w_out[layer],
            rope, update_ctx=(layer < DEPTH - 1))
    return x
```

```python
import functools

import numpy as np
import jax
import jax.numpy as jnp
from jax import lax
from jax.experimental import pallas as pl
from jax.experimental.pallas import tpu as pltpu

F32 = jnp.float32
BF16 = jnp.bfloat16

GRID_W = 64
RET_HEADS = 8
CONV_KERNEL = 31
ROPE_BASE = 10000.0
NORM_EPS = 1e-6

V7X_LANES = 128
V7X_SUBLANES_BF16 = 16
V7X_VMEM_LIMIT_BYTES = 56 * 1024 * 1024

RET_CHUNK = 256
CONV_HALO = 16


def _cparams(semantics):
    return pltpu.CompilerParams(dimension_semantics=semantics, vmem_limit_bytes=V7X_VMEM_LIMIT_BYTES)


def _ada_kernel(s_ref, w_ref, b_ref, o_ref):
    s = s_ref[...]
    s = s * jax.nn.sigmoid(s)
    acc = jnp.dot(s.astype(BF16), w_ref[...].astype(BF16), preferred_element_type=F32)
    o_ref[...] = acc + b_ref[...]


def _ada_mod(cc, ada_w, ada_b, tn=512):
    rows, d = cc.shape
    cols = ada_w.shape[1]
    return pl.pallas_call(
        _ada_kernel,
        grid=(cols // tn,),
        in_specs=[
            pl.BlockSpec((rows, d), lambda j: (0, 0)),
            pl.BlockSpec((d, tn), lambda j: (0, j)),
            pl.BlockSpec((1, tn), lambda j: (0, j)),
        ],
        out_specs=pl.BlockSpec((rows, tn), lambda j: (0, j)),
        out_shape=jax.ShapeDtypeStruct((rows, cols), F32),
        compiler_params=_cparams(("arbitrary",)),
        name="ada_mod",
    )(cc, ada_w, ada_b)


def _inproj_kernel(x_ref, g_ref, mod_ref, w_ref, o_ref, h_ref, *, row_chunk):
    @pl.when(pl.program_id(1) == 0)
    def _():
        gain = g_ref[...] * (1.0 + mod_ref[0, 1:2, :])
        shift = mod_ref[0, 0:1, :]

        def body(r, carry):
            rows = pl.ds(pl.multiple_of(r * row_chunk, row_chunk), row_chunk)
            xc = x_ref[rows, :]
            var = jnp.mean(xc * xc, axis=-1, keepdims=True)
            h_ref[rows, :] = (xc * lax.rsqrt(var + NORM_EPS) * gain + shift).astype(BF16)
            return carry

        lax.fori_loop(0, x_ref.shape[0] // row_chunk, body, 0)

    o_ref[...] = jnp.dot(h_ref[...], w_ref[...], preferred_element_type=F32).astype(o_ref.dtype)


def _in_proj(x2d, pre_g, mod3, w_bf16, *, tm, tn, col_block0, n_col_blocks, mod_row_fn, name):
    m, d = x2d.shape
    kern = functools.partial(_inproj_kernel, row_chunk=16)
    return pl.pallas_call(
        kern,
        grid=(m // tm, n_col_blocks),
        in_specs=[
            pl.BlockSpec((tm, d), lambda i, j: (i, 0)),
            pl.BlockSpec((1, d), lambda i, j: (0, 0)),
            pl.BlockSpec((1, 3, d), lambda i, j: (mod_row_fn(i), 0, 0)),
            pl.BlockSpec((d, tn), lambda i, j: (0, j + col_block0)),
        ],
        out_specs=pl.BlockSpec((tm, tn), lambda i, j: (i, j)),
        out_shape=jax.ShapeDtypeStruct((m, n_col_blocks * tn), BF16),
        scratch_shapes=[pltpu.VMEM((tm, d), BF16)],
        compiler_params=_cparams(("arbitrary", "arbitrary")),
        name=name,
    )(x2d, pre_g, mod3, w_bf16)


def _rope_tables(n_tokens, head_dim):
    rows = n_tokens // GRID_W
    pos_r = np.repeat(np.arange(rows, dtype=np.float64), GRID_W)
    pos_c = np.tile(np.arange(GRID_W, dtype=np.float64), rows)
    n_freq = head_dim // 4
    inv_freq = ROPE_BASE ** (-np.arange(n_freq, dtype=np.float64) / n_freq)
    ang_r = pos_r[:, None] * inv_freq[None, :]
    ang_c = pos_c[:, None] * inv_freq[None, :]
    cos = np.concatenate([np.cos(ang_r), np.cos(ang_r), np.cos(ang_c), np.cos(ang_c)], axis=-1)
    sin = np.concatenate([-np.sin(ang_r), np.sin(ang_r), -np.sin(ang_c), np.sin(ang_c)], axis=-1)
    return jnp.asarray(cos, F32), jnp.asarray(sin, F32)


def _dot_tn(a, b):
    return lax.dot_general(a, b, (((0,), (0,)), ((), ())), preferred_element_type=F32)


def _dot_nt(a, b):
    return lax.dot_general(a, b, (((1,), (1,)), ((), ())), preferred_element_type=F32)


def _retention_kernel(lg_ref, q_ref, k_ref, v_ref, gt_ref, kc_ref, vc_ref, cos_ref, sin_ref, gng_ref,
                      o_ref, qs_ref, ks_ref, rb_ref, st_ref, c0_ref, *, chunk, rope_rows):
    n, dh = q_ref.shape
    nc = n // chunk
    n_ctx = kc_ref.shape[0]
    head = pl.program_id(1)
    lgf = lg_ref[0, head]
    lgb = lg_ref[1, head]
    k_scale = dh ** -0.5
    half = V7X_LANES // 2

    def rope_body(r, carry):
        rows = pl.ds(pl.multiple_of(r * rope_rows, rope_rows), rope_rows)
        cos = cos_ref[rows, :]
        sin = sin_ref[rows, :]

        def rope(t):
            swapped = jnp.concatenate(
                [pltpu.roll(t[:, g * V7X_LANES:(g + 1) * V7X_LANES], half, axis=1)
                 for g in range(dh // V7X_LANES)], axis=1)
            return t * cos + swapped * sin

        qs_ref[rows, :] = rope(q_ref[rows, :].astype(F32)).astype(BF16)
        ks_ref[rows, :] = (rope(k_ref[rows, :].astype(F32)) * k_scale).astype(BF16)
        return carry

    lax.fori_loop(0, n // rope_rows, rope_body, 0)

    col = lax.broadcasted_iota(jnp.int32, (chunk, 1), 0).astype(F32)
    rel = (lax.broadcasted_iota(jnp.int32, (chunk, chunk), 0)
           - lax.broadcasted_iota(jnp.int32, (chunk, chunk), 1)).astype(F32)
    dmat = (jnp.where(rel >= 0, jnp.exp(lgf * jnp.maximum(rel, 0.0)), 0.0)
            + jnp.where(rel <= 0, jnp.exp(lgb * jnp.maximum(-rel, 0.0)), 0.0))
    qw_f = jnp.exp(lgf * (col + 1.0))
    qw_b = jnp.exp(lgb * (chunk - col))
    kw_f = jnp.exp(lgf * (chunk - 1.0 - col))
    kw_b = jnp.exp(lgb * col)
    dec_f = jnp.exp(jnp.full((1, dh), lgf * chunk, F32))
    dec_b = jnp.exp(jnp.full((1, dh), lgb * chunk, F32))

    pos_c = lax.broadcasted_iota(jnp.int32, (n_ctx, 1), 0).astype(F32)
    kc = kc_ref[...].astype(F32) * k_scale
    vc = vc_ref[...]
    c0_ref[0] = _dot_tn((kc * jnp.exp(lgf * (n_ctx - 1.0 - pos_c))).astype(BF16), vc)
    c0_ref[1] = _dot_tn((kc * jnp.exp(lgb * pos_c)).astype(BF16), vc)

    st_ref[...] = c0_ref[1]

    def bwd_body(t, carry):
        c = nc - 1 - t
        rows = pl.ds(pl.multiple_of(c * chunk, chunk), chunk)
        state = st_ref[...]
        rb_ref[c] = state.astype(BF16)
        kv = _dot_tn((ks_ref[rows, :].astype(F32) * kw_b).astype(BF16), v_ref[rows, :])
        st_ref[...] = state * dec_b + kv
        return carry

    lax.fori_loop(0, nc, bwd_body, 0)

    st_ref[...] = c0_ref[0]
    gng = gng_ref[...]

    def fwd_body(c, carry):
        rows = pl.ds(pl.multiple_of(c * chunk, chunk), chunk)
        q = qs_ref[rows, :]
        k = ks_ref[rows, :]
        v = v_ref[rows, :]
        state = st_ref[...]
        qf = q.astype(F32)
        scores = _dot_nt(q, k) * dmat
        o = jnp.dot(scores.astype(BF16), v, preferred_element_type=F32)
        o = o + jnp.dot((qf * qw_f).astype(BF16), state.astype(BF16), preferred_element_type=F32)
        o = o + jnp.dot((qf * qw_b).astype(BF16), rb_ref[c], preferred_element_type=F32)
        mu = jnp.mean(o, axis=-1, keepdims=True)
        oc = o - mu
        var = jnp.mean(oc * oc, axis=-1, keepdims=True)
        gt = gt_ref[rows, :].astype(F32)
        y = (oc * lax.rsqrt(var + NORM_EPS) * gng).astype(BF16)
        o_ref[rows, :] = (y * (gt * jax.nn.sigmoid(gt)).astype(BF16)).astype(o_ref.dtype)
        kv = _dot_tn((k.astype(F32) * kw_f).astype(BF16), v)
        st_ref[...] = state * dec_f + kv
        return carry

    lax.fori_loop(0, nc, fwd_body, 0)


def _retention(lg, proj, kv_c, cos, sin, gn_g, *, batch, n, n_ctx, heads, dh):
    r = heads * dh
    kern = functools.partial(_retention_kernel, chunk=RET_CHUNK, rope_rows=64)
    const2 = lambda b, h: (0, 0)
    return pl.pallas_call(
        kern,
        grid=(batch, heads),
        in_specs=[
            pl.BlockSpec(memory_space=pltpu.SMEM),
            pl.BlockSpec((n, dh), lambda b, h: (b, h)),
            pl.BlockSpec((n, dh), lambda b, h: (b, heads + h)),
            pl.BlockSpec((n, dh), lambda b, h: (b, 2 * heads + h)),
            pl.BlockSpec((n, dh), lambda b, h: (b, 3 * heads + h)),
            pl.BlockSpec((n_ctx, dh), lambda b, h: (b, h)),
            pl.BlockSpec((n_ctx, dh), lambda b, h: (b, heads + h)),
            pl.BlockSpec((n, dh), const2, pipeline_mode=pl.Buffered(1)),
            pl.BlockSpec((n, dh), const2, pipeline_mode=pl.Buffered(1)),
            pl.BlockSpec((1, dh), lambda b, h: (0, h)),
        ],
        out_specs=pl.BlockSpec((n, dh), lambda b, h: (b, h)),
        out_shape=jax.ShapeDtypeStruct((batch * n, r), BF16),
        scratch_shapes=[
            pltpu.VMEM((n, dh), BF16),
            pltpu.VMEM((n, dh), BF16),
            pltpu.VMEM((n // RET_CHUNK, dh, dh), BF16),
            pltpu.VMEM((dh, dh), F32),
            pltpu.VMEM((2, dh, dh), F32),
        ],
        compiler_params=_cparams(("arbitrary", "arbitrary")),
        name="retention",
    )(lg, proj, proj, proj, proj, kv_c, kv_c, cos, sin, gn_g)


def _conv_kernel(a_ref, b_ref, ap_ref, bp_ref, an_ref, bn_ref, gc_ref, w_ref, cb_ref, lg_ref, lb_ref,
                 o_ref, u_ref, *, tiles_per_seq, row_chunk, lane_chunk):
    tn, cw = a_ref.shape
    halo = ap_ref.shape[0]
    taps = w_ref.shape[0]
    pad = taps // 2
    il = pl.program_id(0) % tiles_per_seq

    def glu(a, b):
        return a.astype(F32) * jax.nn.sigmoid(b.astype(F32))

    u_ref[0:halo, :] = jnp.where(il > 0, glu(ap_ref[...], bp_ref[...]), 0.0)
    u_ref[halo + tn:2 * halo + tn, :] = jnp.where(il < tiles_per_seq - 1, glu(an_ref[...], bn_ref[...]), 0.0)

    def glu_body(r, carry):
        rows = pl.ds(pl.multiple_of(r * row_chunk, row_chunk), row_chunk)
        u_ref[pl.ds(pl.multiple_of(halo + r * row_chunk, row_chunk), row_chunk), :] = glu(a_ref[rows, :], b_ref[rows, :])
        return carry

    lax.fori_loop(0, tn // row_chunk, glu_body, 0)

    def row_body(r, carry):
        r0 = pl.multiple_of(r * row_chunk, row_chunk)
        parts = []
        for l0 in range(0, cw, lane_chunk):
            lanes = slice(l0, l0 + lane_chunk)
            acc = jnp.broadcast_to(cb_ref[:, lanes], (row_chunk, lane_chunk))
            win = u_ref[pl.ds(r0, row_chunk + 2 * halo), lanes]
            for t in range(taps):
                s0 = halo - pad + t
                acc = acc + win[s0:s0 + row_chunk, :] * w_ref[t:t + 1, lanes]
            parts.append(acc)
        y = jnp.concatenate(parts, axis=1)
        mu = jnp.mean(y, axis=-1, keepdims=True)
        yc = y - mu
        var = jnp.mean(yc * yc, axis=-1, keepdims=True)
        z = yc * lax.rsqrt(var + NORM_EPS) * lg_ref[...] + lb_ref[...]
        z = z * jax.nn.sigmoid(z)
        rows = pl.ds(r0, row_chunk)
        g = gc_ref[rows, :].astype(F32)
        o_ref[rows, :] = (z.astype(BF16) * (g * jax.nn.sigmoid(g)).astype(BF16)).astype(o_ref.dtype)
        return carry

    lax.fori_loop(0, tn // row_chunk, row_body, 0)


def _conv_branch(proj, dw_w, dw_b, ln_g, ln_b, *, n, cw, col_block0, tn=256):
    m = proj.shape[0]
    halo = CONV_HALO
    tiles_per_seq = n // tn
    hb = tn // halo
    last_hb = m // halo - 1
    kern = functools.partial(_conv_kernel, tiles_per_seq=tiles_per_seq, row_chunk=16, lane_chunk=512)
    main = lambda cb: pl.BlockSpec((tn, cw), lambda i: (i, cb))
    prev = lambda cb: pl.BlockSpec((halo, cw), lambda i: (jnp.maximum(i * hb - 1, 0), cb))
    nxt = lambda cb: pl.BlockSpec((halo, cw), lambda i: (jnp.minimum((i + 1) * hb, last_hb), cb))
    vec = lambda rows: pl.BlockSpec((rows, cw), lambda i: (0, 0))
    a_cb, b_cb, g_cb = col_block0, col_block0 + 1, col_block0 + 2
    return pl.pallas_call(
        kern,
        grid=(m // tn,),
        in_specs=[main(a_cb), main(b_cb), prev(a_cb), prev(b_cb), nxt(a_cb), nxt(b_cb), main(g_cb),
                  vec(dw_w.shape[0]), vec(1), vec(1), vec(1)],
        out_specs=pl.BlockSpec((tn, cw), lambda i: (i, 0)),
        out_shape=jax.ShapeDtypeStruct((m, cw), BF16),
        scratch_shapes=[pltpu.VMEM((tn + 2 * halo, cw), F32)],
        compiler_params=_cparams(("arbitrary",)),
        name="conv_branch",
    )(proj, proj, proj, proj, proj, proj, proj, dw_w, dw_b, ln_g, ln_b)


def _outproj_kernel(r_ref, c_ref, w1_ref, w2_ref, x_ref, mod_ref, pg_ref, o_ref, *, tn, row_chunk):
    tm, d = o_ref.shape
    for j0 in range(0, d, tn):
        cols = slice(j0, j0 + tn)
        y = jnp.dot(r_ref[...], w1_ref[:, cols], preferred_element_type=F32)
        y = y + jnp.dot(c_ref[...], w2_ref[:, cols], preferred_element_type=F32)
        o_ref[:, cols] = y
    gain = mod_ref[0, 2:3, :] * pg_ref[...]

    def body(r, carry):
        rows = pl.ds(pl.multiple_of(r * row_chunk, row_chunk), row_chunk)
        y = o_ref[rows, :]
        var = jnp.mean(y * y, axis=-1, keepdims=True)
        o_ref[rows, :] = x_ref[rows, :] + y * lax.rsqrt(var + NORM_EPS) * gain
        return carry

    lax.fori_loop(0, tm // row_chunk, body, 0)


def _out_proj(ret_b, conv_b, w_out_b, x2d, mod3, post_g, *, tm, tiles_per_batch):
    m, d = x2d.shape
    r = ret_b.shape[1]
    cw = conv_b.shape[1]
    assert r == cw and w_out_b.shape == (r + cw, d)
    kern = functools.partial(_outproj_kernel, tn=512, row_chunk=16)
    return pl.pallas_call(
        kern,
        grid=(m // tm,),
        in_specs=[
            pl.BlockSpec((tm, r), lambda i: (i, 0)),
            pl.BlockSpec((tm, cw), lambda i: (i, 0)),
            pl.BlockSpec((r, d), lambda i: (0, 0), pipeline_mode=pl.Buffered(1)),
            pl.BlockSpec((cw, d), lambda i: (1, 0), pipeline_mode=pl.Buffered(1)),
            pl.BlockSpec((tm, d), lambda i: (i, 0)),
            pl.BlockSpec((1, 3, d), lambda i: (i // tiles_per_batch, 0, 0)),
            pl.BlockSpec((1, d), lambda i: (0, 0)),
        ],
        out_specs=pl.BlockSpec((tm, d), lambda i: (i, 0)),
        out_shape=jax.ShapeDtypeStruct((m, d), F32),
        compiler_params=_cparams(("arbitrary",)),
        name="out_proj",
    )(ret_b, conv_b, w_out_b, w_out_b, x2d, mod3, post_g)


def kernel(x, c, ctx, c_ctx, ada_w, ada_b, pre_norm_g, post_norm_g, w_in, ret_log_decay_fwd,
           ret_log_decay_bwd, ret_gn_g, conv_dw_w, conv_dw_b, conv_ln_g, conv_ln_b, w_out):
    batch, n, d = x.shape
    n_ctx = ctx.shape[1]
    depth = ada_w.shape[0]
    r = ret_gn_g.shape[1]
    cw = conv_dw_w.shape[2]
    heads = RET_HEADS
    dh = r // heads
    assert depth == 1, "single-layer block: the context stream is never updated"
    assert w_in.shape[2] == 4 * r + 3 * cw and r == cw

    cos, sin = _rope_tables(n, dh)
    x2d = x.reshape(batch * n, d)
    ctx2d = ctx.reshape(batch * n_ctx, d)

    mod_rows = 8
    cc = jnp.concatenate([c, c_ctx[None, :], jnp.zeros((mod_rows - batch - 1, d), F32)], axis=0)
    mod3 = _ada_mod(cc, ada_w[0], ada_b).reshape(mod_rows, 3, d)

    w_in_b = w_in[0].astype(BF16)
    w_out_b = w_out[0].astype(BF16)

    tm_in, tn_in = 512, 1024
    proj = _in_proj(x2d, pre_norm_g, mod3, w_in_b, tm=tm_in, tn=tn_in, col_block0=0,
                    n_col_blocks=w_in.shape[2] // tn_in, mod_row_fn=lambda i: i // (n // tm_in),
                    name="in_proj")
    kv_c = _in_proj(ctx2d, pre_norm_g, mod3, w_in_b, tm=batch * n_ctx, tn=tn_in, col_block0=r // tn_in,
                    n_col_blocks=2 * r // tn_in, mod_row_fn=lambda i: batch, name="in_proj_ctx")

    lg = jnp.concatenate([ret_log_decay_fwd, ret_log_decay_bwd], axis=0)
    ret_b = _retention(lg, proj, kv_c, cos, sin, ret_gn_g, batch=batch, n=n, n_ctx=n_ctx, heads=heads, dh=dh)
    conv_b = _conv_branch(proj, conv_dw_w[0], conv_dw_b, conv_ln_g, conv_ln_b, n=n, cw=cw,
                          col_block0=4 * r // cw)

    tm_out = 256
    out = _out_proj(ret_b, conv_b, w_out_b, x2d, mod3, post_norm_g,
                    tm=tm_out, tiles_per_batch=n // tm_out)
    return out.reshape(batch, n, d)
```

```python
import functools

import numpy as np
import jax
import jax.numpy as jnp
from jax import lax
from jax.experimental import pallas as pl
from jax.experimental.pallas import tpu as pltpu

F32 = jnp.float32
BF16 = jnp.bfloat16

GRID_W = 64
RET_HEADS = 8
CONV_KERNEL = 31
ROPE_BASE = 10000.0
NORM_EPS = 1e-6

V7X_LANES = 128
V7X_SUBLANES_BF16 = 16
V7X_VMEM_LIMIT_BYTES = 56 * 1024 * 1024

RET_CHUNK = 256
CONV_HALO = 16


def _cparams(semantics):
    return pltpu.CompilerParams(dimension_semantics=semantics, vmem_limit_bytes=V7X_VMEM_LIMIT_BYTES)


def _ada_kernel(s_ref, w_ref, b_ref, o_ref):
    s = s_ref[...]
    s = s * jax.nn.sigmoid(s)
    acc = jnp.dot(s.astype(BF16), w_ref[...].astype(BF16), preferred_element_type=F32)
    o_ref[...] = acc + b_ref[...]


def _ada_mod(cc, ada_w, ada_b, tn=512):
    rows, d = cc.shape
    cols = ada_w.shape[1]
    return pl.pallas_call(
        _ada_kernel,
        grid=(cols // tn,),
        in_specs=[
            pl.BlockSpec((rows, d), lambda j: (0, 0)),
            pl.BlockSpec((d, tn), lambda j: (0, j)),
            pl.BlockSpec((1, tn), lambda j: (0, j)),
        ],
        out_specs=pl.BlockSpec((rows, tn), lambda j: (0, j)),
        out_shape=jax.ShapeDtypeStruct((rows, cols), F32),
        compiler_params=_cparams(("arbitrary",)),
        name="ada_mod",
    )(cc, ada_w, ada_b)


def _prenorm_to_scratch(x_ref, g_ref, mod_ref, h_ref, row_chunk):
    gain = g_ref[...] * (1.0 + mod_ref[0, 1:2, :])
    shift = mod_ref[0, 0:1, :]

    def body(r, carry):
        rows = pl.ds(pl.multiple_of(r * row_chunk, row_chunk), row_chunk)
        xc = x_ref[rows, :]
        var = jnp.mean(xc * xc, axis=-1, keepdims=True)
        h_ref[rows, :] = (xc * lax.rsqrt(var + NORM_EPS) * gain + shift).astype(BF16)
        return carry

    lax.fori_loop(0, x_ref.shape[0] // row_chunk, body, 0, unroll=4)


def _inproj_kernel(x_ref, g_ref, mod_ref, w_ref, o_ref, h_ref, *, row_chunk):
    @pl.when(pl.program_id(1) == 0)
    def _():
        _prenorm_to_scratch(x_ref, g_ref, mod_ref, h_ref, row_chunk)

    o_ref[...] = jnp.dot(h_ref[...], w_ref[...], preferred_element_type=F32).astype(o_ref.dtype)


def _inproj_rope_kernel(x_ref, g_ref, mod_ref, w_ref, cos_ref, sin_ref, o_ref, h_ref, *,
                        row_chunk, q_tiles, k_tiles, head_dim):
    j = pl.program_id(1)

    @pl.when(j == 0)
    def _():
        _prenorm_to_scratch(x_ref, g_ref, mod_ref, h_ref, row_chunk)

    acc = jnp.dot(h_ref[...], w_ref[...], preferred_element_type=F32)

    @pl.when(j < q_tiles + k_tiles)
    def _():
        scale = jnp.where(j >= q_tiles, head_dim ** -0.5, 1.0).astype(F32)
        half = V7X_LANES // 2
        groups_per_head = head_dim // V7X_LANES
        for g in range(acc.shape[1] // V7X_LANES):
            lanes = slice(g * V7X_LANES, (g + 1) * V7X_LANES)
            tl = slice((g % groups_per_head) * V7X_LANES, (g % groups_per_head + 1) * V7X_LANES)
            t = acc[:, lanes]
            roped = t * cos_ref[:, tl] + pltpu.roll(t, half, axis=1) * sin_ref[:, tl]
            o_ref[:, lanes] = (roped * scale).astype(o_ref.dtype)

    @pl.when(j >= q_tiles + k_tiles)
    def _():
        o_ref[...] = acc.astype(o_ref.dtype)


def _in_proj(x2d, pre_g, mod3, w_bf16, *, tm, tn, col_block0, n_col_blocks, mod_row_fn, name, rope=None):
    m, d = x2d.shape
    in_specs = [
        pl.BlockSpec((tm, d), lambda i, j: (i, 0)),
        pl.BlockSpec((1, d), lambda i, j: (0, 0)),
        pl.BlockSpec((1, 3, d), lambda i, j: (mod_row_fn(i), 0, 0)),
        pl.BlockSpec((d, tn), lambda i, j: (0, j + col_block0)),
    ]
    args = [x2d, pre_g, mod3, w_bf16]
    if rope is None:
        kern = functools.partial(_inproj_kernel, row_chunk=16)
    else:
        cos, sin, n_tokens, rope_cols = rope
        head_dim = cos.shape[1]
        tiles_per_seq = n_tokens // tm
        table = pl.BlockSpec((tm, head_dim), lambda i, j: (i % tiles_per_seq, 0))
        in_specs += [table, table]
        args += [cos, sin]
        kern = functools.partial(_inproj_rope_kernel, row_chunk=16, q_tiles=rope_cols // tn,
                                 k_tiles=rope_cols // tn, head_dim=head_dim)
    return pl.pallas_call(
        kern,
        grid=(m // tm, n_col_blocks),
        in_specs=in_specs,
        out_specs=pl.BlockSpec((tm, tn), lambda i, j: (i, j)),
        out_shape=jax.ShapeDtypeStruct((m, n_col_blocks * tn), BF16),
        scratch_shapes=[pltpu.VMEM((tm, d), BF16)],
        compiler_params=_cparams(("arbitrary", "arbitrary")),
        name=name,
    )(*args)


def _rope_tables(n_tokens, head_dim):
    rows = n_tokens // GRID_W
    pos_r = np.repeat(np.arange(rows, dtype=np.float64), GRID_W)
    pos_c = np.tile(np.arange(GRID_W, dtype=np.float64), rows)
    n_freq = head_dim // 4
    inv_freq = ROPE_BASE ** (-np.arange(n_freq, dtype=np.float64) / n_freq)
    ang_r = pos_r[:, None] * inv_freq[None, :]
    ang_c = pos_c[:, None] * inv_freq[None, :]
    cos = np.concatenate([np.cos(ang_r), np.cos(ang_r), np.cos(ang_c), np.cos(ang_c)], axis=-1)
    sin = np.concatenate([-np.sin(ang_r), np.sin(ang_r), -np.sin(ang_c), np.sin(ang_c)], axis=-1)
    return jnp.asarray(cos, F32), jnp.asarray(sin, F32)


def _dot_tn(a, b):
    return lax.dot_general(a, b, (((0,), (0,)), ((), ())), preferred_element_type=F32)


def _dot_nt(a, b):
    return lax.dot_general(a, b, (((1,), (1,)), ((), ())), preferred_element_type=F32)


def _retention_kernel(lg_ref, q_ref, k_ref, v_ref, gt_ref, kc_ref, vc_ref, gng_ref,
                      o_ref, kv_ref, r_ref, st_ref, dm_ref, wt_ref, *, chunk, unroll):
    n, dh = q_ref.shape
    nc = n // chunk
    n_ctx = kc_ref.shape[0]
    head = pl.program_id(1)
    lgf = lg_ref[0, head]
    lgb = lg_ref[1, head]

    col = lax.broadcasted_iota(jnp.int32, (chunk, dh), 0).astype(F32)
    rel = (lax.broadcasted_iota(jnp.int32, (chunk, chunk), 0)
           - lax.broadcasted_iota(jnp.int32, (chunk, chunk), 1)).astype(F32)
    dm_ref[...] = (jnp.where(rel >= 0, jnp.exp(lgf * jnp.maximum(rel, 0.0)), 0.0)
                   + jnp.where(rel <= 0, jnp.exp(lgb * jnp.maximum(-rel, 0.0)), 0.0))
    wt_ref[0] = jnp.exp(lgf * (col + 1.0))
    wt_ref[1] = jnp.exp(lgb * (chunk - col))
    wt_ref[2] = jnp.exp(lgf * (chunk - 1.0 - col))
    wt_ref[3] = jnp.exp(lgb * col)
    dec = jnp.concatenate([jnp.exp(jnp.full((1, dh), lgf * chunk, F32)),
                           jnp.exp(jnp.full((1, dh), lgb * chunk, F32))], axis=1)

    def weighted_v(v, w_fwd, w_bwd):
        vf = v.astype(F32)
        return jnp.concatenate([vf * w_fwd, vf * w_bwd], axis=1).astype(BF16)

    pos_c = lax.broadcasted_iota(jnp.int32, (n_ctx, dh), 0).astype(F32)
    kc = (kc_ref[...].astype(F32) * dh ** -0.5).astype(BF16)
    st_ref[...] = _dot_tn(kc, weighted_v(vc_ref[...], jnp.exp(lgf * (n_ctx - 1.0 - pos_c)), jnp.exp(lgb * pos_c)))

    def kv_body(c, carry):
        rows = pl.ds(pl.multiple_of(c * chunk, chunk), chunk)
        kv_ref[c] = _dot_tn(k_ref[rows, :], weighted_v(v_ref[rows, :], wt_ref[2], wt_ref[3]))
        return carry

    lax.fori_loop(0, nc, kv_body, 0, unroll=unroll)

    def scan_body(t, carry):
        cf = t
        cb = nc - 1 - t
        state = st_ref[...]
        r_ref[cf, :, 0:dh] = state[:, 0:dh].astype(BF16)
        r_ref[cb, :, dh:2 * dh] = state[:, dh:2 * dh].astype(BF16)
        kv = jnp.concatenate([kv_ref[cf, :, 0:dh], kv_ref[cb, :, dh:2 * dh]], axis=1)
        st_ref[...] = state * dec + kv
        return carry

    lax.fori_loop(0, nc, scan_body, 0)

    gng = gng_ref[...]

    def out_body(c, carry):
        rows = pl.ds(pl.multiple_of(c * chunk, chunk), chunk)
        q = q_ref[rows, :]
        v = v_ref[rows, :]
        scores = _dot_nt(q, k_ref[rows, :]) * dm_ref[...]
        o = jnp.dot(scores.astype(BF16), v, preferred_element_type=F32)
        x = jnp.dot(q, r_ref[c], preferred_element_type=F32)
        o = o + x[:, 0:dh] * wt_ref[0] + x[:, dh:2 * dh] * wt_ref[1]
        mu = jnp.mean(o, axis=-1, keepdims=True)
        oc = o - mu
        var = jnp.mean(oc * oc, axis=-1, keepdims=True)
        gt = gt_ref[rows, :].astype(F32)
        y = oc * lax.rsqrt(var + NORM_EPS) * gng
        o_ref[rows, :] = (y * (gt * jax.nn.sigmoid(gt))).astype(o_ref.dtype)
        return carry

    lax.fori_loop(0, nc, out_body, 0, unroll=unroll)


def _retention(lg, proj, kv_c, gn_g, *, batch, n, n_ctx, heads, dh):
    r = heads * dh
    nc = n // RET_CHUNK
    kern = functools.partial(_retention_kernel, chunk=RET_CHUNK, unroll=2)
    return pl.pallas_call(
        kern,
        grid=(batch, heads),
        in_specs=[
            pl.BlockSpec(memory_space=pltpu.SMEM),
            pl.BlockSpec((n, dh), lambda b, h: (b, h)),
            pl.BlockSpec((n, dh), lambda b, h: (b, heads + h)),
            pl.BlockSpec((n, dh), lambda b, h: (b, 2 * heads + h)),
            pl.BlockSpec((n, dh), lambda b, h: (b, 3 * heads + h)),
            pl.BlockSpec((n_ctx, dh), lambda b, h: (b, h)),
            pl.BlockSpec((n_ctx, dh), lambda b, h: (b, heads + h)),
            pl.BlockSpec((1, dh), lambda b, h: (0, h)),
        ],
        out_specs=pl.BlockSpec((n, dh), lambda b, h: (b, h)),
        out_shape=jax.ShapeDtypeStruct((batch * n, r), BF16),
        scratch_shapes=[
            pltpu.VMEM((nc, dh, 2 * dh), F32),
            pltpu.VMEM((nc, dh, 2 * dh), BF16),
            pltpu.VMEM((dh, 2 * dh), F32),
            pltpu.VMEM((RET_CHUNK, RET_CHUNK), F32),
            pltpu.VMEM((4, RET_CHUNK, dh), F32),
        ],
        compiler_params=_cparams(("arbitrary", "arbitrary")),
        name="retention",
    )(lg, proj, proj, proj, proj, kv_c, kv_c, gn_g)


def _conv_kernel(a_ref, b_ref, ap_ref, bp_ref, an_ref, bn_ref, gc_ref, w_ref, cb_ref, lg_ref, lb_ref,
                 o_ref, u_ref, y_ref, w3_ref, s1_ref, mu_ref, rs_ref, *, tiles_per_seq, row_chunk, conv_rows):
    tn, cw = a_ref.shape
    halo = ap_ref.shape[0]
    taps = w_ref.shape[0]
    pad = taps // 2
    n_groups = cw // V7X_LANES
    il = pl.program_id(0) % tiles_per_seq

    def glu(a, b):
        return a.astype(F32) * jax.nn.sigmoid(b.astype(F32))

    def lane_group(val, g):
        return val[:, g * V7X_LANES:(g + 1) * V7X_LANES]

    u_prev = jnp.where(il > 0, glu(ap_ref[...], bp_ref[...]), 0.0)
    u_next = jnp.where(il < tiles_per_seq - 1, glu(an_ref[...], bn_ref[...]), 0.0)
    for g in range(n_groups):
        u_ref[g, 0:halo, :] = lane_group(u_prev, g)
        u_ref[g, halo + tn:2 * halo + tn, :] = lane_group(u_next, g)
        w3_ref[g, 0:taps, :] = lane_group(w_ref[...], g)
        w3_ref[g, taps:taps + 1, :] = lane_group(cb_ref[...], g)

    def glu_body(r, carry):
        r0 = pl.multiple_of(r * row_chunk, row_chunk)
        val = glu(a_ref[pl.ds(r0, row_chunk), :], b_ref[pl.ds(r0, row_chunk), :])
        for g in range(n_groups):
            u_ref[g, pl.ds(halo + r0, row_chunk), :] = lane_group(val, g)
        return carry

    lax.fori_loop(0, tn // row_chunk, glu_body, 0)

    s1_ref[...] = jnp.zeros_like(s1_ref)

    def conv_body(g, carry):
        for r0 in range(0, tn, conv_rows):
            acc = jnp.broadcast_to(w3_ref[g, taps:taps + 1, :], (conv_rows, V7X_LANES))
            for t in range(taps):
                s0 = r0 + halo - pad + t
                acc = acc + u_ref[g, s0:s0 + conv_rows, :] * w3_ref[g, t:t + 1, :]
            y_ref[g, r0:r0 + conv_rows, :] = acc
            s1_ref[r0:r0 + conv_rows, :] += acc
        return carry

    lax.fori_loop(0, n_groups, conv_body, 0)

    inv_cw = 1.0 / cw
    for r0 in range(0, tn, conv_rows):
        rows = slice(r0, r0 + conv_rows)
        mu = jnp.broadcast_to(jnp.sum(s1_ref[rows, :], axis=-1, keepdims=True) * inv_cw, (conv_rows, V7X_LANES))
        sq = jnp.zeros((conv_rows, V7X_LANES), F32)
        for g in range(n_groups):
            dlt = y_ref[g, rows, :] - mu
            sq = sq + dlt * dlt
        var = jnp.sum(sq, axis=-1, keepdims=True) * inv_cw
        mu_ref[rows, :] = mu
        rs_ref[rows, :] = jnp.broadcast_to(lax.rsqrt(var + NORM_EPS), (conv_rows, V7X_LANES))

    def row_body(r, carry):
        r0 = pl.multiple_of(r * row_chunk, row_chunk)
        rows = pl.ds(r0, row_chunk)
        mu = mu_ref[rows, :]
        rs = rs_ref[rows, :]
        for g in range(n_groups):
            lanes = slice(g * V7X_LANES, (g + 1) * V7X_LANES)
            z = (y_ref[g, rows, :] - mu) * rs * lg_ref[:, lanes] + lb_ref[:, lanes]
            gate = gc_ref[rows, lanes].astype(F32)
            o_ref[rows, lanes] = (z * jax.nn.sigmoid(z) * (gate * jax.nn.sigmoid(gate))).astype(o_ref.dtype)
        return carry

    lax.fori_loop(0, tn // row_chunk, row_body, 0)


def _conv_branch(proj, dw_w, dw_b, ln_g, ln_b, *, n, cw, col_block0, tn=256):
    m = proj.shape[0]
    halo = CONV_HALO
    tiles_per_seq = n // tn
    hb = tn // halo
    last_hb = m // halo - 1
    n_groups = cw // V7X_LANES
    kern = functools.partial(_conv_kernel, tiles_per_seq=tiles_per_seq, row_chunk=16, conv_rows=64)
    main = lambda cb: pl.BlockSpec((tn, cw), lambda i: (i, cb))
    prev = lambda cb: pl.BlockSpec((halo, cw), lambda i: (jnp.maximum(i * hb - 1, 0), cb))
    nxt = lambda cb: pl.BlockSpec((halo, cw), lambda i: (jnp.minimum((i + 1) * hb, last_hb), cb))
    vec = lambda rows: pl.BlockSpec((rows, cw), lambda i: (0, 0))
    a_cb, b_cb, g_cb = col_block0, col_block0 + 1, col_block0 + 2
    return pl.pallas_call(
        kern,
        grid=(m // tn,),
        in_specs=[main(a_cb), main(b_cb), prev(a_cb), prev(b_cb), nxt(a_cb), nxt(b_cb), main(g_cb),
                  vec(dw_w.shape[0]), vec(1), vec(1), vec(1)],
        out_specs=pl.BlockSpec((tn, cw), lambda i: (i, 0)),
        out_shape=jax.ShapeDtypeStruct((m, cw), BF16),
        scratch_shapes=[
            pltpu.VMEM((n_groups, tn + 2 * halo, V7X_LANES), F32),
            pltpu.VMEM((n_groups, tn, V7X_LANES), F32),
            pltpu.VMEM((n_groups, dw_w.shape[0] + 1, V7X_LANES), F32),
            pltpu.VMEM((tn, V7X_LANES), F32),
            pltpu.VMEM((tn, V7X_LANES), F32),
            pltpu.VMEM((tn, V7X_LANES), F32),
        ],
        compiler_params=_cparams(("arbitrary",)),
        name="conv_branch",
    )(proj, proj, proj, proj, proj, proj, proj, dw_w, dw_b, ln_g, ln_b)


def _outproj_kernel(r_ref, c_ref, w1_ref, w2_ref, x_ref, mod_ref, pg_ref, o_ref, rs_ref, *, tn, row_chunk):
    tm, d = o_ref.shape
    ssq = jnp.zeros((tm, V7X_LANES), F32)
    for j0 in range(0, d, tn):
        cols = slice(j0, j0 + tn)
        y = jnp.dot(r_ref[...], w1_ref[:, cols], preferred_element_type=F32)
        y = y + jnp.dot(c_ref[...], w2_ref[:, cols], preferred_element_type=F32)
        o_ref[:, cols] = y
        for g0 in range(0, tn, V7X_LANES):
            yg = y[:, g0:g0 + V7X_LANES]
            ssq = ssq + yg * yg
    var = jnp.sum(ssq, axis=-1, keepdims=True) * (1.0 / d)
    rs_ref[...] = jnp.broadcast_to(lax.rsqrt(var + NORM_EPS), (tm, V7X_LANES))

    def body(r, carry):
        rows = pl.ds(pl.multiple_of(r * row_chunk, row_chunk), row_chunk)
        rs = rs_ref[rows, :]
        for g0 in range(0, d, V7X_LANES):
            lanes = slice(g0, g0 + V7X_LANES)
            gain = mod_ref[0, 2:3, lanes] * pg_ref[:, lanes]
            o_ref[rows, lanes] = x_ref[rows, lanes] + o_ref[rows, lanes] * rs * gain
        return carry

    lax.fori_loop(0, tm // row_chunk, body, 0)


def _out_proj(ret_b, conv_b, w_out_b, x2d, mod3, post_g, *, tm, tiles_per_batch):
    m, d = x2d.shape
    r = ret_b.shape[1]
    cw = conv_b.shape[1]
    assert r == cw and w_out_b.shape == (r + cw, d)
    kern = functools.partial(_outproj_kernel, tn=512, row_chunk=16)
    return pl.pallas_call(
        kern,
        grid=(m // tm,),
        in_specs=[
            pl.BlockSpec((tm, r), lambda i: (i, 0)),
            pl.BlockSpec((tm, cw), lambda i: (i, 0)),
            pl.BlockSpec((r, d), lambda i: (0, 0), pipeline_mode=pl.Buffered(1)),
            pl.BlockSpec((cw, d), lambda i: (1, 0), pipeline_mode=pl.Buffered(1)),
            pl.BlockSpec((tm, d), lambda i: (i, 0)),
            pl.BlockSpec((1, 3, d), lambda i: (i // tiles_per_batch, 0, 0)),
            pl.BlockSpec((1, d), lambda i: (0, 0)),
        ],
        out_specs=pl.BlockSpec((tm, d), lambda i: (i, 0)),
        out_shape=jax.ShapeDtypeStruct((m, d), F32),
        scratch_shapes=[pltpu.VMEM((tm, V7X_LANES), F32)],
        compiler_params=_cparams(("arbitrary",)),
        name="out_proj",
    )(ret_b, conv_b, w_out_b, w_out_b, x2d, mod3, post_g)


def kernel(x, c, ctx, c_ctx, ada_w, ada_b, pre_norm_g, post_norm_g, w_in, ret_log_decay_fwd,
           ret_log_decay_bwd, ret_gn_g, conv_dw_w, conv_dw_b, conv_ln_g, conv_ln_b, w_out):
    batch, n, d = x.shape
    n_ctx = ctx.shape[1]
    depth = ada_w.shape[0]
    r = ret_gn_g.shape[1]
    cw = conv_dw_w.shape[2]
    heads = RET_HEADS
    dh = r // heads
    assert depth == 1, "single-layer block: the context stream is never updated"
    assert w_in.shape[2] == 4 * r + 3 * cw and r == cw

    cos, sin = _rope_tables(n, dh)
    x2d = x.reshape(batch * n, d)
    ctx2d = ctx.reshape(batch * n_ctx, d)

    mod_rows = 8
    cc = jnp.concatenate([c, c_ctx[None, :], jnp.zeros((mod_rows - batch - 1, d), F32)], axis=0)
    mod3 = _ada_mod(cc, ada_w[0], ada_b).reshape(mod_rows, 3, d)

    w_in_b = w_in[0].astype(BF16)
    w_out_b = w_out[0].astype(BF16)

    tm_in, tn_in = 512, 1024
    proj = _in_proj(x2d, pre_norm_g, mod3, w_in_b, tm=tm_in, tn=tn_in, col_block0=0,
                    n_col_blocks=w_in.shape[2] // tn_in, mod_row_fn=lambda i: i // (n // tm_in),
                    name="in_proj", rope=(cos, sin, n, r))
    kv_c = _in_proj(ctx2d, pre_norm_g, mod3, w_in_b, tm=batch * n_ctx, tn=tn_in, col_block0=r // tn_in,
                    n_col_blocks=2 * r // tn_in, mod_row_fn=lambda i: batch, name="in_proj_ctx")

    lg = jnp.concatenate([ret_log_decay_fwd, ret_log_decay_bwd], axis=0)
    ret_b = _retention(lg, proj, kv_c, ret_gn_g, batch=batch, n=n, n_ctx=n_ctx, heads=heads, dh=dh)
    conv_b = _conv_branch(proj, conv_dw_w[0], conv_dw_b, conv_ln_g, conv_ln_b, n=n, cw=cw,
                          col_block0=4 * r // cw)

    tm_out = 256
    out = _out_proj(ret_b, conv_b, w_out_b, x2d, mod3, post_norm_g,
                    tm=tm_out, tiles_per_batch=n // tm_out)
    return out.reshape(batch, n, d)
```

```python
import functools

import numpy as np
import jax
import jax.numpy as jnp
from jax import lax
from jax.experimental import pallas as pl
from jax.experimental.pallas import tpu as pltpu

F32 = jnp.float32
BF16 = jnp.bfloat16

GRID_W = 64
RET_HEADS = 8
CONV_KERNEL = 31
ROPE_BASE = 10000.0
NORM_EPS = 1e-6

V7X_LANES = 128
V7X_SUBLANES_BF16 = 16
V7X_VMEM_LIMIT_BYTES = 56 * 1024 * 1024

RET_CHUNK = 256
CONV_HALO = 16


def _cparams(semantics):
    return pltpu.CompilerParams(dimension_semantics=semantics, vmem_limit_bytes=V7X_VMEM_LIMIT_BYTES)


def _ada_kernel(s_ref, w_ref, b_ref, o_ref):
    s = s_ref[...]
    s = s * jax.nn.sigmoid(s)
    acc = jnp.dot(s.astype(BF16), w_ref[...].astype(BF16), preferred_element_type=F32)
    o_ref[...] = acc + b_ref[...]


def _ada_mod(cc, ada_w, ada_b, tn=512):
    rows, d = cc.shape
    cols = ada_w.shape[1]
    return pl.pallas_call(
        _ada_kernel,
        grid=(cols // tn,),
        in_specs=[
            pl.BlockSpec((rows, d), lambda j: (0, 0)),
            pl.BlockSpec((d, tn), lambda j: (0, j)),
            pl.BlockSpec((1, tn), lambda j: (0, j)),
        ],
        out_specs=pl.BlockSpec((rows, tn), lambda j: (0, j)),
        out_shape=jax.ShapeDtypeStruct((rows, cols), F32),
        compiler_params=_cparams(("arbitrary",)),
        name="ada_mod",
    )(cc, ada_w, ada_b)


def _prenorm_kernel(x_ref, g_ref, mod_ref, h_ref, *, row_chunk):
    gain = g_ref[...] * (1.0 + mod_ref[0, 1:2, :])
    shift = mod_ref[0, 0:1, :]

    def body(r, carry):
        rows = pl.ds(pl.multiple_of(r * row_chunk, row_chunk), row_chunk)
        xc = x_ref[rows, :]
        var = jnp.mean(xc * xc, axis=-1, keepdims=True)
        h_ref[rows, :] = (xc * lax.rsqrt(var + NORM_EPS) * gain + shift).astype(h_ref.dtype)
        return carry

    lax.fori_loop(0, x_ref.shape[0] // row_chunk, body, 0, unroll=4)


def _prenorm(x2d, pre_g, mod3, *, tm, mod_row_fn, name):
    m, d = x2d.shape
    return pl.pallas_call(
        functools.partial(_prenorm_kernel, row_chunk=16),
        grid=(m // tm,),
        in_specs=[
            pl.BlockSpec((tm, d), lambda i: (i, 0)),
            pl.BlockSpec((1, d), lambda i: (0, 0)),
            pl.BlockSpec((1, 3, d), lambda i: (mod_row_fn(i), 0, 0)),
        ],
        out_specs=pl.BlockSpec((tm, d), lambda i: (i, 0)),
        out_shape=jax.ShapeDtypeStruct((m, d), BF16),
        compiler_params=_cparams(("arbitrary",)),
        name=name,
    )(x2d, pre_g, mod3)


def _inproj_kernel(h_ref, w_ref, o_ref):
    o_ref[...] = jnp.dot(h_ref[...], w_ref[...], preferred_element_type=F32).astype(o_ref.dtype)


def _inproj_rope_kernel(h_ref, w_ref, cos_ref, sin_ref, o_ref, *, q_tiles, k_tiles, head_dim):
    j = pl.program_id(1)
    is_rope = j < q_tiles + k_tiles
    scale = jnp.where(j >= q_tiles, head_dim ** -0.5, 1.0).astype(F32)
    half = V7X_LANES // 2
    groups_per_head = head_dim // V7X_LANES
    acc = jnp.dot(h_ref[...], w_ref[...], preferred_element_type=F32)
    for g in range(acc.shape[1] // V7X_LANES):
        lanes = slice(g * V7X_LANES, (g + 1) * V7X_LANES)
        tl = slice((g % groups_per_head) * V7X_LANES, (g % groups_per_head + 1) * V7X_LANES)
        t = acc[:, lanes]
        roped = (t * cos_ref[:, tl] + pltpu.roll(t, half, axis=1) * sin_ref[:, tl]) * scale
        o_ref[:, lanes] = jnp.where(is_rope, roped, t).astype(o_ref.dtype)


def _in_proj(h, w_bf16, *, tm, tn, col_block0, n_col_blocks, name, rope=None):
    m, d = h.shape
    in_specs = [
        pl.BlockSpec((tm, d), lambda i, j: (i, 0)),
        pl.BlockSpec((d, tn), lambda i, j: (0, j + col_block0)),
    ]
    args = [h, w_bf16]
    if rope is None:
        kern = _inproj_kernel
    else:
        cos, sin, n_tokens, rope_cols = rope
        head_dim = cos.shape[1]
        tiles_per_seq = n_tokens // tm
        table = pl.BlockSpec((tm, head_dim), lambda i, j: (i % tiles_per_seq, 0))
        in_specs += [table, table]
        args += [cos, sin]
        kern = functools.partial(_inproj_rope_kernel, q_tiles=rope_cols // tn, k_tiles=rope_cols // tn,
                                 head_dim=head_dim)
    return pl.pallas_call(
        kern,
        grid=(m // tm, n_col_blocks),
        in_specs=in_specs,
        out_specs=pl.BlockSpec((tm, tn), lambda i, j: (i, j)),
        out_shape=jax.ShapeDtypeStruct((m, n_col_blocks * tn), BF16),
        compiler_params=_cparams(("arbitrary", "arbitrary")),
        name=name,
    )(*args)


def _rope_tables(n_tokens, head_dim):
    rows = n_tokens // GRID_W
    pos_r = np.repeat(np.arange(rows, dtype=np.float64), GRID_W)
    pos_c = np.tile(np.arange(GRID_W, dtype=np.float64), rows)
    n_freq = head_dim // 4
    inv_freq = ROPE_BASE ** (-np.arange(n_freq, dtype=np.float64) / n_freq)
    ang_r = pos_r[:, None] * inv_freq[None, :]
    ang_c = pos_c[:, None] * inv_freq[None, :]
    cos = np.concatenate([np.cos(ang_r), np.cos(ang_r), np.cos(ang_c), np.cos(ang_c)], axis=-1)
    sin = np.concatenate([-np.sin(ang_r), np.sin(ang_r), -np.sin(ang_c), np.sin(ang_c)], axis=-1)
    return jnp.asarray(cos, F32), jnp.asarray(sin, F32)


def _dot_tn(a, b):
    return lax.dot_general(a, b, (((0,), (0,)), ((), ())), preferred_element_type=F32)


def _dot_nt(a, b):
    return lax.dot_general(a, b, (((1,), (1,)), ((), ())), preferred_element_type=F32)


def _retention_kernel(lg_ref, q_ref, k_ref, v_ref, gt_ref, kc_ref, vc_ref, gng_ref,
                      o_ref, kv_ref, r_ref, st_ref, dm_ref, wt_ref, *, chunk, unroll):
    n, dh = q_ref.shape
    nc = n // chunk
    n_ctx = kc_ref.shape[0]
    head = pl.program_id(1)
    lgf = lg_ref[0, head]
    lgb = lg_ref[1, head]

    col = lax.broadcasted_iota(jnp.int32, (chunk, dh), 0).astype(F32)
    rel = (lax.broadcasted_iota(jnp.int32, (chunk, chunk), 0)
           - lax.broadcasted_iota(jnp.int32, (chunk, chunk), 1)).astype(F32)
    dm_ref[...] = (jnp.where(rel >= 0, jnp.exp(lgf * jnp.maximum(rel, 0.0)), 0.0)
                   + jnp.where(rel <= 0, jnp.exp(lgb * jnp.maximum(-rel, 0.0)), 0.0))
    wt_ref[0] = jnp.exp(lgf * (col + 1.0))
    wt_ref[1] = jnp.exp(lgb * (chunk - col))
    wt_ref[2] = jnp.exp(lgf * (chunk - 1.0 - col))
    wt_ref[3] = jnp.exp(lgb * col)
    dec = jnp.concatenate([jnp.exp(jnp.full((1, dh), lgf * chunk, F32)),
                           jnp.exp(jnp.full((1, dh), lgb * chunk, F32))], axis=1)

    def weighted_v(v, w_fwd, w_bwd):
        vf = v.astype(F32)
        return jnp.concatenate([vf * w_fwd, vf * w_bwd], axis=1).astype(BF16)

    pos_c = lax.broadcasted_iota(jnp.int32, (n_ctx, dh), 0).astype(F32)
    kc = (kc_ref[...].astype(F32) * dh ** -0.5).astype(BF16)
    st_ref[...] = _dot_tn(kc, weighted_v(vc_ref[...], jnp.exp(lgf * (n_ctx - 1.0 - pos_c)), jnp.exp(lgb * pos_c)))

    def kv_body(c, carry):
        rows = pl.ds(pl.multiple_of(c * chunk, chunk), chunk)
        kv_ref[c] = _dot_tn(k_ref[rows, :], weighted_v(v_ref[rows, :], wt_ref[2], wt_ref[3]))
        return carry

    lax.fori_loop(0, nc, kv_body, 0, unroll=unroll)

    def scan_body(t, carry):
        cf = t
        cb = nc - 1 - t
        state = st_ref[...]
        r_ref[cf, :, 0:dh] = state[:, 0:dh].astype(BF16)
        r_ref[cb, :, dh:2 * dh] = state[:, dh:2 * dh].astype(BF16)
        kv = jnp.concatenate([kv_ref[cf, :, 0:dh], kv_ref[cb, :, dh:2 * dh]], axis=1)
        st_ref[...] = state * dec + kv
        return carry

    lax.fori_loop(0, nc, scan_body, 0)

    gng = gng_ref[...]

    def out_body(c, carry):
        rows = pl.ds(pl.multiple_of(c * chunk, chunk), chunk)
        q = q_ref[rows, :]
        v = v_ref[rows, :]
        scores = _dot_nt(q, k_ref[rows, :]) * dm_ref[...]
        o = jnp.dot(scores.astype(BF16), v, preferred_element_type=F32)
        x = jnp.dot(q, r_ref[c], preferred_element_type=F32)
        o = o + x[:, 0:dh] * wt_ref[0] + x[:, dh:2 * dh] * wt_ref[1]
        mu = jnp.mean(o, axis=-1, keepdims=True)
        oc = o - mu
        var = jnp.mean(oc * oc, axis=-1, keepdims=True)
        gt = gt_ref[rows, :].astype(F32)
        y = oc * lax.rsqrt(var + NORM_EPS) * gng
        o_ref[rows, :] = (y * (gt * jax.nn.sigmoid(gt))).astype(o_ref.dtype)
        return carry

    lax.fori_loop(0, nc, out_body, 0, unroll=unroll)


def _retention(lg, proj, kv_c, gn_g, *, batch, n, n_ctx, heads, dh):
    r = heads * dh
    nc = n // RET_CHUNK
    kern = functools.partial(_retention_kernel, chunk=RET_CHUNK, unroll=2)
    return pl.pallas_call(
        kern,
        grid=(batch, heads),
        in_specs=[
            pl.BlockSpec(memory_space=pltpu.SMEM),
            pl.BlockSpec((n, dh), lambda b, h: (b, h)),
            pl.BlockSpec((n, dh), lambda b, h: (b, heads + h)),
            pl.BlockSpec((n, dh), lambda b, h: (b, 2 * heads + h)),
            pl.BlockSpec((n, dh), lambda b, h: (b, 3 * heads + h)),
            pl.BlockSpec((n_ctx, dh), lambda b, h: (b, h)),
            pl.BlockSpec((n_ctx, dh), lambda b, h: (b, heads + h)),
            pl.BlockSpec((1, dh), lambda b, h: (0, h)),
        ],
        out_specs=pl.BlockSpec((n, dh), lambda b, h: (b, h)),
        out_shape=jax.ShapeDtypeStruct((batch * n, r), BF16),
        scratch_shapes=[
            pltpu.VMEM((nc, dh, 2 * dh), F32),
            pltpu.VMEM((nc, dh, 2 * dh), BF16),
            pltpu.VMEM((dh, 2 * dh), F32),
            pltpu.VMEM((RET_CHUNK, RET_CHUNK), F32),
            pltpu.VMEM((4, RET_CHUNK, dh), F32),
        ],
        compiler_params=_cparams(("arbitrary", "arbitrary")),
        name="retention",
    )(lg, proj, proj, proj, proj, kv_c, kv_c, gn_g)


def _conv_kernel(a_ref, b_ref, ap_ref, bp_ref, an_ref, bn_ref, gc_ref, w_ref, cb_ref, lg_ref, lb_ref,
                 o_ref, u_ref, y_ref, w3_ref, s1_ref, mu_ref, rs_ref, *, tiles_per_seq, row_chunk, conv_rows):
    tn, cw = a_ref.shape
    halo = ap_ref.shape[0]
    taps = w_ref.shape[0]
    pad = taps // 2
    n_groups = cw // V7X_LANES
    il = pl.program_id(0) % tiles_per_seq

    def glu(a, b):
        return a.astype(F32) * jax.nn.sigmoid(b.astype(F32))

    def lane_group(val, g):
        return val[:, g * V7X_LANES:(g + 1) * V7X_LANES]

    u_prev = jnp.where(il > 0, glu(ap_ref[...], bp_ref[...]), 0.0)
    u_next = jnp.where(il < tiles_per_seq - 1, glu(an_ref[...], bn_ref[...]), 0.0)
    for g in range(n_groups):
        u_ref[g, 0:halo, :] = lane_group(u_prev, g)
        u_ref[g, halo + tn:2 * halo + tn, :] = lane_group(u_next, g)
        w3_ref[g, 0:taps, :] = lane_group(w_ref[...], g)
        w3_ref[g, taps:taps + 1, :] = lane_group(cb_ref[...], g)

    def glu_body(r, carry):
        r0 = pl.multiple_of(r * row_chunk, row_chunk)
        val = glu(a_ref[pl.ds(r0, row_chunk), :], b_ref[pl.ds(r0, row_chunk), :])
        for g in range(n_groups):
            u_ref[g, pl.ds(halo + r0, row_chunk), :] = lane_group(val, g)
        return carry

    lax.fori_loop(0, tn // row_chunk, glu_body, 0)

    s1_ref[...] = jnp.zeros_like(s1_ref)

    def conv_body(g, carry):
        for r0 in range(0, tn, conv_rows):
            acc = jnp.broadcast_to(w3_ref[g, taps:taps + 1, :], (conv_rows, V7X_LANES))
            for t in range(taps):
                s0 = r0 + halo - pad + t
                acc = acc + u_ref[g, s0:s0 + conv_rows, :] * w3_ref[g, t:t + 1, :]
            y_ref[g, r0:r0 + conv_rows, :] = acc
            s1_ref[r0:r0 + conv_rows, :] += acc
        return carry

    lax.fori_loop(0, n_groups, conv_body, 0)

    inv_cw = 1.0 / cw
    for r0 in range(0, tn, conv_rows):
        rows = slice(r0, r0 + conv_rows)
        mu = jnp.broadcast_to(jnp.sum(s1_ref[rows, :], axis=-1, keepdims=True) * inv_cw, (conv_rows, V7X_LANES))
        sq = jnp.zeros((conv_rows, V7X_LANES), F32)
        for g in range(n_groups):
            dlt = y_ref[g, rows, :] - mu
            sq = sq + dlt * dlt
        var = jnp.sum(sq, axis=-1, keepdims=True) * inv_cw
        mu_ref[rows, :] = mu
        rs_ref[rows, :] = jnp.broadcast_to(lax.rsqrt(var + NORM_EPS), (conv_rows, V7X_LANES))

    def row_body(r, carry):
        r0 = pl.multiple_of(r * row_chunk, row_chunk)
        rows = pl.ds(r0, row_chunk)
        mu = mu_ref[rows, :]
        rs = rs_ref[rows, :]
        for g in range(n_groups):
            lanes = slice(g * V7X_LANES, (g + 1) * V7X_LANES)
            z = (y_ref[g, rows, :] - mu) * rs * lg_ref[:, lanes] + lb_ref[:, lanes]
            gate = gc_ref[rows, lanes].astype(F32)
            o_ref[rows, lanes] = (z * jax.nn.sigmoid(z) * (gate * jax.nn.sigmoid(gate))).astype(o_ref.dtype)
        return carry

    lax.fori_loop(0, tn // row_chunk, row_body, 0)


def _conv_branch(proj, dw_w, dw_b, ln_g, ln_b, *, n, cw, col_block0, tn=256):
    m = proj.shape[0]
    halo = CONV_HALO
    tiles_per_seq = n // tn
    hb = tn // halo
    last_hb = m // halo - 1
    n_groups = cw // V7X_LANES
    kern = functools.partial(_conv_kernel, tiles_per_seq=tiles_per_seq, row_chunk=16, conv_rows=64)
    main = lambda cb: pl.BlockSpec((tn, cw), lambda i: (i, cb))
    prev = lambda cb: pl.BlockSpec((halo, cw), lambda i: (jnp.maximum(i * hb - 1, 0), cb))
    nxt = lambda cb: pl.BlockSpec((halo, cw), lambda i: (jnp.minimum((i + 1) * hb, last_hb), cb))
    vec = lambda rows: pl.BlockSpec((rows, cw), lambda i: (0, 0))
    a_cb, b_cb, g_cb = col_block0, col_block0 + 1, col_block0 + 2
    return pl.pallas_call(
        kern,
        grid=(m // tn,),
        in_specs=[main(a_cb), main(b_cb), prev(a_cb), prev(b_cb), nxt(a_cb), nxt(b_cb), main(g_cb),
                  vec(dw_w.shape[0]), vec(1), vec(1), vec(1)],
        out_specs=pl.BlockSpec((tn, cw), lambda i: (i, 0)),
        out_shape=jax.ShapeDtypeStruct((m, cw), BF16),
        scratch_shapes=[
            pltpu.VMEM((n_groups, tn + 2 * halo, V7X_LANES), F32),
            pltpu.VMEM((n_groups, tn, V7X_LANES), F32),
            pltpu.VMEM((n_groups, dw_w.shape[0] + 1, V7X_LANES), F32),
            pltpu.VMEM((tn, V7X_LANES), F32),
            pltpu.VMEM((tn, V7X_LANES), F32),
            pltpu.VMEM((tn, V7X_LANES), F32),
        ],
        compiler_params=_cparams(("arbitrary",)),
        name="conv_branch",
    )(proj, proj, proj, proj, proj, proj, proj, dw_w, dw_b, ln_g, ln_b)


def _outproj_kernel(r_ref, c_ref, w1_ref, w2_ref, x_ref, mod_ref, pg_ref, o_ref, rs_ref, *, tn, row_chunk):
    tm, d = o_ref.shape
    ssq = jnp.zeros((tm, V7X_LANES), F32)
    for j0 in range(0, d, tn):
        cols = slice(j0, j0 + tn)
        y = jnp.dot(r_ref[...], w1_ref[:, cols], preferred_element_type=F32)
        y = y + jnp.dot(c_ref[...], w2_ref[:, cols], preferred_element_type=F32)
        o_ref[:, cols] = y
        for g0 in range(0, tn, V7X_LANES):
            yg = y[:, g0:g0 + V7X_LANES]
            ssq = ssq + yg * yg
    var = jnp.sum(ssq, axis=-1, keepdims=True) * (1.0 / d)
    rs_ref[...] = jnp.broadcast_to(lax.rsqrt(var + NORM_EPS), (tm, V7X_LANES))

    def body(r, carry):
        rows = pl.ds(pl.multiple_of(r * row_chunk, row_chunk), row_chunk)
        rs = rs_ref[rows, :]
        for g0 in range(0, d, V7X_LANES):
            lanes = slice(g0, g0 + V7X_LANES)
            gain = mod_ref[0, 2:3, lanes] * pg_ref[:, lanes]
            o_ref[rows, lanes] = x_ref[rows, lanes] + o_ref[rows, lanes] * rs * gain
        return carry

    lax.fori_loop(0, tm // row_chunk, body, 0)


def _out_proj(ret_b, conv_b, w_out_b, x2d, mod3, post_g, *, tm, tiles_per_batch):
    m, d = x2d.shape
    r = ret_b.shape[1]
    cw = conv_b.shape[1]
    assert r == cw and w_out_b.shape == (r + cw, d)
    kern = functools.partial(_outproj_kernel, tn=512, row_chunk=16)
    return pl.pallas_call(
        kern,
        grid=(m // tm,),
        in_specs=[
            pl.BlockSpec((tm, r), lambda i: (i, 0)),
            pl.BlockSpec((tm, cw), lambda i: (i, 0)),
            pl.BlockSpec((r, d), lambda i: (0, 0), pipeline_mode=pl.Buffered(1)),
            pl.BlockSpec((cw, d), lambda i: (1, 0), pipeline_mode=pl.Buffered(1)),
            pl.BlockSpec((tm, d), lambda i: (i, 0)),
            pl.BlockSpec((1, 3, d), lambda i: (i // tiles_per_batch, 0, 0)),
            pl.BlockSpec((1, d), lambda i: (0, 0)),
        ],
        out_specs=pl.BlockSpec((tm, d), lambda i: (i, 0)),
        out_shape=jax.ShapeDtypeStruct((m, d), F32),
        scratch_shapes=[pltpu.VMEM((tm, V7X_LANES), F32)],
        compiler_params=_cparams(("arbitrary",)),
        name="out_proj",
    )(ret_b, conv_b, w_out_b, w_out_b, x2d, mod3, post_g)


def kernel(x, c, ctx, c_ctx, ada_w, ada_b, pre_norm_g, post_norm_g, w_in, ret_log_decay_fwd,
           ret_log_decay_bwd, ret_gn_g, conv_dw_w, conv_dw_b, conv_ln_g, conv_ln_b, w_out):
    batch, n, d = x.shape
    n_ctx = ctx.shape[1]
    depth = ada_w.shape[0]
    r = ret_gn_g.shape[1]
    cw = conv_dw_w.shape[2]
    heads = RET_HEADS
    dh = r // heads
    assert depth == 1, "single-layer block: the context stream is never updated"
    assert w_in.shape[2] == 4 * r + 3 * cw and r == cw

    cos, sin = _rope_tables(n, dh)
    x2d = x.reshape(batch * n, d)
    ctx2d = ctx.reshape(batch * n_ctx, d)

    mod_rows = 8
    cc = jnp.concatenate([c, c_ctx[None, :], jnp.zeros((mod_rows - batch - 1, d), F32)], axis=0)
    mod3 = _ada_mod(cc, ada_w[0], ada_b).reshape(mod_rows, 3, d)

    w_in_b = w_in[0].astype(BF16)
    w_out_b = w_out[0].astype(BF16)

    tm_norm = 256
    h = _prenorm(x2d, pre_norm_g, mod3, tm=tm_norm, mod_row_fn=lambda i: i // (n // tm_norm), name="prenorm")
    h_ctx = _prenorm(ctx2d, pre_norm_g, mod3, tm=tm_norm, mod_row_fn=lambda i: batch, name="prenorm_ctx")

    tm_in, tn_in = 1024, 1024
    proj = _in_proj(h, w_in_b, tm=tm_in, tn=tn_in, col_block0=0, n_col_blocks=w_in.shape[2] // tn_in,
                    name="in_proj", rope=(cos, sin, n, r))
    kv_c = _in_proj(h_ctx, w_in_b, tm=batch * n_ctx, tn=tn_in, col_block0=r // tn_in,
                    n_col_blocks=2 * r // tn_in, name="in_proj_ctx")

    lg = jnp.concatenate([ret_log_decay_fwd, ret_log_decay_bwd], axis=0)
    ret_b = _retention(lg, proj, kv_c, ret_gn_g, batch=batch, n=n, n_ctx=n_ctx, heads=heads, dh=dh)
    conv_b = _conv_branch(proj, conv_dw_w[0], conv_dw_b, conv_ln_g, conv_ln_b, n=n, cw=cw,
                          col_block0=4 * r // cw)

    tm_out = 256
    out = _out_proj(ret_b, conv_b, w_out_b, x2d, mod3, post_norm_g,
                    tm=tm_out, tiles_per_batch=n // tm_out)
    return out.reshape(batch, n, d)
```

```python
import functools

import numpy as np
import jax
import jax.numpy as jnp
from jax import lax
from jax.experimental import pallas as pl
from jax.experimental.pallas import tpu as pltpu

F32 = jnp.float32
BF16 = jnp.bfloat16

GRID_W = 64
RET_HEADS = 8
CONV_KERNEL = 31
ROPE_BASE = 10000.0
NORM_EPS = 1e-6

V7X_LANES = 128
V7X_SUBLANES_BF16 = 16
V7X_VMEM_LIMIT_BYTES = 58 * 1024 * 1024

RET_CHUNK = 256
CONV_HALO = 16


def _cparams(semantics):
    return pltpu.CompilerParams(dimension_semantics=semantics, vmem_limit_bytes=V7X_VMEM_LIMIT_BYTES)


def _silu(x):
    h = 0.5 * x
    return h + h * jnp.tanh(h)


def _mul_sigmoid(a, b):
    ha = 0.5 * a
    return ha + ha * jnp.tanh(0.5 * b)


def _ada_kernel(s_ref, w_ref, b_ref, o_ref):
    s = s_ref[...]
    s = s * jax.nn.sigmoid(s)
    acc = jnp.dot(s.astype(BF16), w_ref[...].astype(BF16), preferred_element_type=F32)
    o_ref[...] = acc + b_ref[...]


def _ada_mod(cc, ada_w, ada_b, tn=512):
    rows, d = cc.shape
    cols = ada_w.shape[1]
    return pl.pallas_call(
        _ada_kernel,
        grid=(cols // tn,),
        in_specs=[
            pl.BlockSpec((rows, d), lambda j: (0, 0)),
            pl.BlockSpec((d, tn), lambda j: (0, j)),
            pl.BlockSpec((1, tn), lambda j: (0, j)),
        ],
        out_specs=pl.BlockSpec((rows, tn), lambda j: (0, j)),
        out_shape=jax.ShapeDtypeStruct((rows, cols), F32),
        compiler_params=_cparams(("arbitrary",)),
        name="ada_mod",
    )(cc, ada_w, ada_b)


def _prenorm_kernel(x_ref, g_ref, mod_ref, h_ref, *, row_chunk):
    gain = g_ref[...] * (1.0 + mod_ref[0, 1:2, :])
    shift = mod_ref[0, 0:1, :]

    def body(r, carry):
        rows = pl.ds(pl.multiple_of(r * row_chunk, row_chunk), row_chunk)
        xc = x_ref[rows, :]
        var = jnp.mean(xc * xc, axis=-1, keepdims=True)
        h_ref[rows, :] = (xc * lax.rsqrt(var + NORM_EPS) * gain + shift).astype(h_ref.dtype)
        return carry

    lax.fori_loop(0, x_ref.shape[0] // row_chunk, body, 0, unroll=4)


def _prenorm(x2d, pre_g, mod3, *, tm, mod_row_fn, name):
    m, d = x2d.shape
    return pl.pallas_call(
        functools.partial(_prenorm_kernel, row_chunk=16),
        grid=(m // tm,),
        in_specs=[
            pl.BlockSpec((tm, d), lambda i: (i, 0)),
            pl.BlockSpec((1, d), lambda i: (0, 0)),
            pl.BlockSpec((1, 3, d), lambda i: (mod_row_fn(i), 0, 0)),
        ],
        out_specs=pl.BlockSpec((tm, d), lambda i: (i, 0)),
        out_shape=jax.ShapeDtypeStruct((m, d), BF16),
        compiler_params=_cparams(("arbitrary",)),
        name=name,
    )(x2d, pre_g, mod3)


def _inproj_kernel(h_ref, w_ref, o_ref):
    o_ref[...] = jnp.dot(h_ref[...], w_ref[...], preferred_element_type=F32).astype(o_ref.dtype)


def _inproj_rope_kernel(h_ref, w_ref, cos_ref, sin_ref, *refs, q_tiles, k_tiles, head_dim):
    n_cast = len(refs) // 2
    o_ref = refs[n_cast]
    for src_ref, dst_ref in zip(refs[:n_cast], refs[n_cast + 1:]):
        dst_ref[...] = src_ref[...].astype(dst_ref.dtype)
    j = pl.program_id(1)
    is_rope = j < q_tiles + k_tiles
    scale = jnp.where(j >= q_tiles, head_dim ** -0.5, 1.0).astype(F32)
    half = V7X_LANES // 2
    groups_per_head = head_dim // V7X_LANES
    acc = jnp.dot(h_ref[...], w_ref[...], preferred_element_type=F32)
    for g in range(acc.shape[1] // V7X_LANES):
        lanes = slice(g * V7X_LANES, (g + 1) * V7X_LANES)
        tl = slice((g % groups_per_head) * V7X_LANES, (g % groups_per_head + 1) * V7X_LANES)
        t = acc[:, lanes]
        roped = (t * cos_ref[:, tl] + pltpu.roll(t, half, axis=1) * sin_ref[:, tl]) * scale
        o_ref[:, lanes] = jnp.where(is_rope, roped, t).astype(o_ref.dtype)


def _in_proj(h, w_bf16, *, tm, tn, col_block0, n_col_blocks, name, rope=None, side_casts=()):
    m, d = h.shape
    n_steps = (m // tm) * n_col_blocks
    in_specs = [
        pl.BlockSpec((tm, d), lambda i, j: (i, 0)),
        pl.BlockSpec((d, tn), lambda i, j: (0, j + col_block0)),
    ]
    args = [h, w_bf16]
    out_specs = [pl.BlockSpec((tm, tn), lambda i, j: (i, j))]
    out_shape = [jax.ShapeDtypeStruct((m, n_col_blocks * tn), BF16)]
    if rope is None:
        assert not side_casts
        kern = _inproj_kernel
    else:
        cos, sin, n_tokens, rope_cols = rope
        head_dim = cos.shape[1]
        tiles_per_seq = n_tokens // tm
        table = pl.BlockSpec((tm, head_dim), lambda i, j: (i % tiles_per_seq, 0))
        in_specs += [table, table]
        args += [cos, sin]
        for src, col_block, width in side_casts:
            rows = src.shape[0]
            assert rows % n_steps == 0 and (rows // n_steps) % V7X_SUBLANES_BF16 == 0
            slab = rows // n_steps
            in_specs.append(pl.BlockSpec((slab, width), lambda i, j, cb=col_block: (i * n_col_blocks + j, cb)))
            out_specs.append(pl.BlockSpec((slab, width), lambda i, j: (i * n_col_blocks + j, 0)))
            out_shape.append(jax.ShapeDtypeStruct((rows, width), BF16))
            args.append(src)
        kern = functools.partial(_inproj_rope_kernel, q_tiles=rope_cols // tn, k_tiles=rope_cols // tn,
                                 head_dim=head_dim)
    outs = pl.pallas_call(
        kern,
        grid=(m // tm, n_col_blocks),
        in_specs=in_specs,
        out_specs=out_specs,
        out_shape=out_shape,
        compiler_params=_cparams(("arbitrary", "arbitrary")),
        name=name,
    )(*args)
    return outs[0] if len(outs) == 1 else outs


def _rope_tables(n_tokens, head_dim):
    rows = n_tokens // GRID_W
    pos_r = np.repeat(np.arange(rows, dtype=np.float64), GRID_W)
    pos_c = np.tile(np.arange(GRID_W, dtype=np.float64), rows)
    n_freq = head_dim // 4
    inv_freq = ROPE_BASE ** (-np.arange(n_freq, dtype=np.float64) / n_freq)
    ang_r = pos_r[:, None] * inv_freq[None, :]
    ang_c = pos_c[:, None] * inv_freq[None, :]
    cos = np.concatenate([np.cos(ang_r), np.cos(ang_r), np.cos(ang_c), np.cos(ang_c)], axis=-1)
    sin = np.concatenate([-np.sin(ang_r), np.sin(ang_r), -np.sin(ang_c), np.sin(ang_c)], axis=-1)
    return jnp.asarray(cos, F32), jnp.asarray(sin, F32)


def _dot_tn(a, b):
    return lax.dot_general(a, b, (((0,), (0,)), ((), ())), preferred_element_type=F32)


def _dot_nt(a, b):
    return lax.dot_general(a, b, (((1,), (1,)), ((), ())), preferred_element_type=F32)


def _retention_kernel(lg_ref, q_ref, k_ref, v_ref, gt_ref, kc_ref, vc_ref, gng_ref,
                      o_ref, kv_ref, r_ref, st_ref, dm_ref, wt_ref, *, chunk, unroll):
    n, dh = q_ref.shape
    nc = n // chunk
    n_ctx = kc_ref.shape[0]
    head = pl.program_id(1)
    lgf = lg_ref[0, head]
    lgb = lg_ref[1, head]

    col = lax.broadcasted_iota(jnp.int32, (chunk, dh), 0).astype(F32)
    rel = (lax.broadcasted_iota(jnp.int32, (chunk, chunk), 0)
           - lax.broadcasted_iota(jnp.int32, (chunk, chunk), 1)).astype(F32)
    dm_ref[...] = (jnp.where(rel >= 0, jnp.exp(lgf * jnp.maximum(rel, 0.0)), 0.0)
                   + jnp.where(rel <= 0, jnp.exp(lgb * jnp.maximum(-rel, 0.0)), 0.0))
    wt_ref[0] = jnp.exp(lgf * (col + 1.0))
    wt_ref[1] = jnp.exp(lgb * (chunk - col))
    wt_ref[2] = jnp.exp(lgf * (chunk - 1.0 - col))
    wt_ref[3] = jnp.exp(lgb * col)
    dec = jnp.concatenate([jnp.exp(jnp.full((1, dh), lgf * chunk, F32)),
                           jnp.exp(jnp.full((1, dh), lgb * chunk, F32))], axis=1)

    def weighted_v(v, w_fwd, w_bwd):
        vf = v.astype(F32)
        return jnp.concatenate([vf * w_fwd, vf * w_bwd], axis=1).astype(BF16)

    pos_c = lax.broadcasted_iota(jnp.int32, (n_ctx, dh), 0).astype(F32)
    kc = (kc_ref[...].astype(F32) * dh ** -0.5).astype(BF16)
    st_ref[...] = _dot_tn(kc, weighted_v(vc_ref[...], jnp.exp(lgf * (n_ctx - 1.0 - pos_c)), jnp.exp(lgb * pos_c)))

    def kv_body(c, carry):
        rows = pl.ds(pl.multiple_of(c * chunk, chunk), chunk)
        kv_ref[c] = _dot_tn(k_ref[rows, :], weighted_v(v_ref[rows, :], wt_ref[2], wt_ref[3]))
        return carry

    lax.fori_loop(0, nc, kv_body, 0, unroll=unroll)

    def scan_body(t, carry):
        cf = t
        cb = nc - 1 - t
        state = st_ref[...]
        r_ref[cf, :, 0:dh] = state[:, 0:dh].astype(BF16)
        r_ref[cb, :, dh:2 * dh] = state[:, dh:2 * dh].astype(BF16)
        kv = jnp.concatenate([kv_ref[cf, :, 0:dh], kv_ref[cb, :, dh:2 * dh]], axis=1)
        st_ref[...] = state * dec + kv
        return carry

    lax.fori_loop(0, nc, scan_body, 0, unroll=unroll)

    gng = gng_ref[...]

    def out_body(c, carry):
        rows = pl.ds(pl.multiple_of(c * chunk, chunk), chunk)
        q = q_ref[rows, :]
        v = v_ref[rows, :]
        scores = _dot_nt(q, k_ref[rows, :]) * dm_ref[...]
        o = jnp.dot(scores.astype(BF16), v, preferred_element_type=F32)
        x = jnp.dot(q, r_ref[c], preferred_element_type=F32)
        o = o + x[:, 0:dh] * wt_ref[0] + x[:, dh:2 * dh] * wt_ref[1]
        mu = jnp.mean(o, axis=-1, keepdims=True)
        oc = o - mu
        var = jnp.mean(oc * oc, axis=-1, keepdims=True)
        gt = gt_ref[rows, :].astype(F32)
        y = oc * lax.rsqrt(var + NORM_EPS) * gng
        o_ref[rows, :] = (y * _silu(gt)).astype(o_ref.dtype)
        return carry

    lax.fori_loop(0, nc, out_body, 0, unroll=unroll)


def _retention(lg, proj, kv_c, gn_g, *, batch, n, n_ctx, heads, dh):
    r = heads * dh
    nc = n // RET_CHUNK
    kern = functools.partial(_retention_kernel, chunk=RET_CHUNK, unroll=2)
    return pl.pallas_call(
        kern,
        grid=(batch, heads),
        in_specs=[
            pl.BlockSpec(memory_space=pltpu.SMEM),
            pl.BlockSpec((n, dh), lambda b, h: (b, h)),
            pl.BlockSpec((n, dh), lambda b, h: (b, heads + h)),
            pl.BlockSpec((n, dh), lambda b, h: (b, 2 * heads + h)),
            pl.BlockSpec((n, dh), lambda b, h: (b, 3 * heads + h)),
            pl.BlockSpec((n_ctx, dh), lambda b, h: (b, h)),
            pl.BlockSpec((n_ctx, dh), lambda b, h: (b, heads + h)),
            pl.BlockSpec((1, dh), lambda b, h: (0, h)),
        ],
        out_specs=pl.BlockSpec((n, dh), lambda b, h: (b, h)),
        out_shape=jax.ShapeDtypeStruct((batch * n, r), BF16),
        scratch_shapes=[
            pltpu.VMEM((nc, dh, 2 * dh), F32),
            pltpu.VMEM((nc, dh, 2 * dh), BF16),
            pltpu.VMEM((dh, 2 * dh), F32),
            pltpu.VMEM((RET_CHUNK, RET_CHUNK), F32),
            pltpu.VMEM((4, RET_CHUNK, dh), F32),
        ],
        compiler_params=_cparams(("arbitrary", "arbitrary")),
        name="retention",
    )(lg, proj, proj, proj, proj, kv_c, kv_c, gn_g)


def _conv_kernel(a_ref, b_ref, ap_ref, bp_ref, an_ref, bn_ref, gc_ref, w_ref, cb_ref, lg_ref, lb_ref,
                 o_ref, u_ref, y_ref, w3_ref, s1_ref, mu_ref, rs_ref, *, tiles_per_seq, row_chunk, conv_rows):
    tn, cw = a_ref.shape
    halo = ap_ref.shape[0]
    taps = w_ref.shape[0]
    pad = taps // 2
    n_groups = cw // V7X_LANES
    il = pl.program_id(0) % tiles_per_seq

    def glu(a, b):
        return _mul_sigmoid(a.astype(F32), b.astype(F32))

    def lane_group(val, g):
        return val[:, g * V7X_LANES:(g + 1) * V7X_LANES]

    u_prev = jnp.where(il > 0, glu(ap_ref[...], bp_ref[...]), 0.0)
    u_next = jnp.where(il < tiles_per_seq - 1, glu(an_ref[...], bn_ref[...]), 0.0)
    for g in range(n_groups):
        u_ref[g, 0:halo, :] = lane_group(u_prev, g)
        u_ref[g, halo + tn:2 * halo + tn, :] = lane_group(u_next, g)
        w3_ref[g, 0:taps, :] = lane_group(w_ref[...], g)
        w3_ref[g, taps:taps + 1, :] = lane_group(cb_ref[...], g)

    def glu_body(r, carry):
        r0 = pl.multiple_of(r * row_chunk, row_chunk)
        val = glu(a_ref[pl.ds(r0, row_chunk), :], b_ref[pl.ds(r0, row_chunk), :])
        for g in range(n_groups):
            u_ref[g, pl.ds(halo + r0, row_chunk), :] = lane_group(val, g)
        return carry

    lax.fori_loop(0, tn // row_chunk, glu_body, 0)

    s1_ref[...] = jnp.zeros_like(s1_ref)

    def conv_body(g, carry):
        for r0 in range(0, tn, conv_rows):
            acc = jnp.broadcast_to(w3_ref[g, taps:taps + 1, :], (conv_rows, V7X_LANES))
            for t in range(taps):
                s0 = r0 + halo - pad + t
                acc = acc + u_ref[g, s0:s0 + conv_rows, :] * w3_ref[g, t:t + 1, :]
            y_ref[g, r0:r0 + conv_rows, :] = acc
            s1_ref[r0:r0 + conv_rows, :] += acc
        return carry

    lax.fori_loop(0, n_groups, conv_body, 0)

    inv_cw = 1.0 / cw
    for r0 in range(0, tn, conv_rows):
        rows = slice(r0, r0 + conv_rows)
        mu = jnp.broadcast_to(jnp.sum(s1_ref[rows, :], axis=-1, keepdims=True) * inv_cw, (conv_rows, V7X_LANES))
        sq = jnp.zeros((conv_rows, V7X_LANES), F32)
        for g in range(n_groups):
            dlt = y_ref[g, rows, :] - mu
            sq = sq + dlt * dlt
        var = jnp.sum(sq, axis=-1, keepdims=True) * inv_cw
        mu_ref[rows, :] = mu
        rs_ref[rows, :] = jnp.broadcast_to(lax.rsqrt(var + NORM_EPS), (conv_rows, V7X_LANES))

    def row_body(r, carry):
        r0 = pl.multiple_of(r * row_chunk, row_chunk)
        rows = pl.ds(r0, row_chunk)
        mu = mu_ref[rows, :]
        rs = rs_ref[rows, :]
        for g in range(n_groups):
            lanes = slice(g * V7X_LANES, (g + 1) * V7X_LANES)
            z = (y_ref[g, rows, :] - mu) * rs * lg_ref[:, lanes] + lb_ref[:, lanes]
            gate = gc_ref[rows, lanes].astype(F32)
            o_ref[rows, lanes] = (_silu(z) * _silu(gate)).astype(o_ref.dtype)
        return carry

    lax.fori_loop(0, tn // row_chunk, row_body, 0)


def _conv_branch(glu_a, glu_b, gate, dw_w, dw_b, ln_g, ln_b, *, n, tn=256):
    m, cw = glu_a.shape
    halo = CONV_HALO
    tiles_per_seq = n // tn
    hb = tn // halo
    last_hb = m // halo - 1
    n_groups = cw // V7X_LANES
    kern = functools.partial(_conv_kernel, tiles_per_seq=tiles_per_seq, row_chunk=16, conv_rows=64)
    main = pl.BlockSpec((tn, cw), lambda i: (i, 0))
    prev = pl.BlockSpec((halo, cw), lambda i: (jnp.maximum(i * hb - 1, 0), 0))
    nxt = pl.BlockSpec((halo, cw), lambda i: (jnp.minimum((i + 1) * hb, last_hb), 0))
    vec = lambda rows: pl.BlockSpec((rows, cw), lambda i: (0, 0))
    return pl.pallas_call(
        kern,
        grid=(m // tn,),
        in_specs=[main, main, prev, prev, nxt, nxt, main,
                  vec(dw_w.shape[0]), vec(1), vec(1), vec(1)],
        out_specs=pl.BlockSpec((tn, cw), lambda i: (i, 0)),
        out_shape=jax.ShapeDtypeStruct((m, cw), BF16),
        scratch_shapes=[
            pltpu.VMEM((n_groups, tn + 2 * halo, V7X_LANES), F32),
            pltpu.VMEM((n_groups, tn, V7X_LANES), F32),
            pltpu.VMEM((n_groups, dw_w.shape[0] + 1, V7X_LANES), F32),
            pltpu.VMEM((tn, V7X_LANES), F32),
            pltpu.VMEM((tn, V7X_LANES), F32),
            pltpu.VMEM((tn, V7X_LANES), F32),
        ],
        compiler_params=_cparams(("arbitrary",)),
        name="conv_branch",
    )(glu_a, glu_b, glu_a, glu_b, glu_a, glu_b, gate, dw_w, dw_b, ln_g, ln_b)


def _outproj_kernel(r_ref, c_ref, w1_ref, w2_ref, x_ref, mod_ref, pg_ref, o_ref, rs_ref, *, tn, row_chunk):
    tm, d = o_ref.shape
    ssq = jnp.zeros((tm, V7X_LANES), F32)
    for j0 in range(0, d, tn):
        cols = slice(j0, j0 + tn)
        y = jnp.dot(r_ref[...], w1_ref[:, cols], preferred_element_type=F32)
        y = y + jnp.dot(c_ref[...], w2_ref[:, cols], preferred_element_type=F32)
        o_ref[:, cols] = y
        for g0 in range(0, tn, V7X_LANES):
            yg = y[:, g0:g0 + V7X_LANES]
            ssq = ssq + yg * yg
    var = jnp.sum(ssq, axis=-1, keepdims=True) * (1.0 / d)
    rs_ref[...] = jnp.broadcast_to(lax.rsqrt(var + NORM_EPS), (tm, V7X_LANES))

    def body(r, carry):
        rows = pl.ds(pl.multiple_of(r * row_chunk, row_chunk), row_chunk)
        rs = rs_ref[rows, :]
        for g0 in range(0, d, V7X_LANES):
            lanes = slice(g0, g0 + V7X_LANES)
            gain = mod_ref[0, 2:3, lanes] * pg_ref[:, lanes]
            o_ref[rows, lanes] = x_ref[rows, lanes] + o_ref[rows, lanes] * rs * gain
        return carry

    lax.fori_loop(0, tm // row_chunk, body, 0)


def _out_proj(ret_b, conv_b, w_out_b, x2d, mod3, post_g, *, tm, tiles_per_batch):
    m, d = x2d.shape
    r = ret_b.shape[1]
    cw = conv_b.shape[1]
    assert r == cw and w_out_b.shape == (r + cw, d)
    kern = functools.partial(_outproj_kernel, tn=512, row_chunk=16)
    return pl.pallas_call(
        kern,
        grid=(m // tm,),
        in_specs=[
            pl.BlockSpec((tm, r), lambda i: (i, 0)),
            pl.BlockSpec((tm, cw), lambda i: (i, 0)),
            pl.BlockSpec((r, d), lambda i: (0, 0), pipeline_mode=pl.Buffered(1)),
            pl.BlockSpec((cw, d), lambda i: (1, 0), pipeline_mode=pl.Buffered(1)),
            pl.BlockSpec((tm, d), lambda i: (i, 0)),
            pl.BlockSpec((1, 3, d), lambda i: (i // tiles_per_batch, 0, 0)),
            pl.BlockSpec((1, d), lambda i: (0, 0)),
        ],
        out_specs=pl.BlockSpec((tm, d), lambda i: (i, 0)),
        out_shape=jax.ShapeDtypeStruct((m, d), F32),
        scratch_shapes=[pltpu.VMEM((tm, V7X_LANES), F32)],
        compiler_params=_cparams(("arbitrary",)),
        name="out_proj",
    )(ret_b, conv_b, w_out_b, w_out_b, x2d, mod3, post_g)


def kernel(x, c, ctx, c_ctx, ada_w, ada_b, pre_norm_g, post_norm_g, w_in, ret_log_decay_fwd,
           ret_log_decay_bwd, ret_gn_g, conv_dw_w, conv_dw_b, conv_ln_g, conv_ln_b, w_out):
    batch, n, d = x.shape
    n_ctx = ctx.shape[1]
    depth = ada_w.shape[0]
    r = ret_gn_g.shape[1]
    cw = conv_dw_w.shape[2]
    heads = RET_HEADS
    dh = r // heads
    assert depth == 1, "single-layer block: the context stream is never updated"
    assert w_in.shape[2] == 4 * r + 3 * cw and r == cw

    cos, sin = _rope_tables(n, dh)
    x2d = x.reshape(batch * n, d)
    ctx2d = ctx.reshape(batch * n_ctx, d)

    mod_rows = 8
    cc = jnp.concatenate([c, c_ctx[None, :], jnp.zeros((mod_rows - batch - 1, d), F32)], axis=0)
    mod3 = _ada_mod(cc, ada_w[0], ada_b).reshape(mod_rows, 3, d)

    ret_cols = 4 * r
    w_ret_b = w_in[0, :, :ret_cols].astype(BF16)

    tm_norm = 512
    h = _prenorm(x2d, pre_norm_g, mod3, tm=tm_norm, mod_row_fn=lambda i: i // (n // tm_norm), name="prenorm")
    h_ctx = _prenorm(ctx2d, pre_norm_g, mod3, tm=tm_norm, mod_row_fn=lambda i: batch, name="prenorm_ctx")

    tm_in, tn_in = 1024, 1024
    side_casts = [(w_in[0], ret_cols // cw + k, cw) for k in range(3)] + [(w_out[0], 0, d)]
    proj, w_glu_a, w_glu_b, w_gate, w_out_b = _in_proj(
        h, w_ret_b, tm=tm_in, tn=tn_in, col_block0=0, n_col_blocks=ret_cols // tn_in, name="in_proj",
        rope=(cos, sin, n, r), side_casts=side_casts)
    kv_c = _in_proj(h_ctx, w_ret_b, tm=batch * n_ctx, tn=tn_in, col_block0=r // tn_in,
                    n_col_blocks=2 * r // tn_in, name="in_proj_ctx")
    glu_a, glu_b, gate_c = [
        _in_proj(h, w_k, tm=tm_in, tn=tn_in, col_block0=0, n_col_blocks=cw // tn_in, name=name)
        for w_k, name in ((w_glu_a, "in_proj_glu_a"), (w_glu_b, "in_proj_glu_b"), (w_gate, "in_proj_conv_gate"))]

    lg = jnp.concatenate([ret_log_decay_fwd, ret_log_decay_bwd], axis=0)
    ret_b = _retention(lg, proj, kv_c, ret_gn_g, batch=batch, n=n, n_ctx=n_ctx, heads=heads, dh=dh)
    conv_b = _conv_branch(glu_a, glu_b, gate_c, conv_dw_w[0], conv_dw_b, conv_ln_g, conv_ln_b, n=n)

    tm_out = 256
    out = _out_proj(ret_b, conv_b, w_out_b, x2d, mod3, post_norm_g,
                    tm=tm_out, tiles_per_batch=n // tm_out)
    return out.reshape(batch, n, d)
```

```python
import functools

import numpy as np
import jax
import jax.numpy as jnp
from jax import lax
from jax.experimental import pallas as pl
from jax.experimental.pallas import tpu as pltpu

F32 = jnp.float32
BF16 = jnp.bfloat16

GRID_W = 64
RET_HEADS = 8
CONV_KERNEL = 31
ROPE_BASE = 10000.0
NORM_EPS = 1e-6

V7X_LANES = 128
V7X_SUBLANES_BF16 = 16
V7X_VMEM_LIMIT_BYTES = 58 * 1024 * 1024

RET_CHUNK = 256
CONV_HALO = 16


def _cparams(semantics):
    return pltpu.CompilerParams(dimension_semantics=semantics, vmem_limit_bytes=V7X_VMEM_LIMIT_BYTES)


def _silu(x):
    h = 0.5 * x
    return h + h * jnp.tanh(h)


def _mul_sigmoid(a, b):
    ha = 0.5 * a
    return ha + ha * jnp.tanh(0.5 * b)


def _ada_columns(s_ref, w_ref, b_ref):
    s = s_ref[...]
    s = s * jax.nn.sigmoid(s)
    acc = jnp.dot(s.astype(BF16), w_ref[...].astype(BF16), preferred_element_type=F32)
    return acc + b_ref[...]


def _ada_kernel(s_ref, w_ref, b_ref, o_ref):
    o_ref[...] = _ada_columns(s_ref, w_ref, b_ref)


def _ada_mod(cc, ada_w, ada_b, n_cols, tn=512):
    rows, d = cc.shape
    return pl.pallas_call(
        _ada_kernel,
        grid=(n_cols // tn,),
        in_specs=[
            pl.BlockSpec((rows, d), lambda j: (0, 0)),
            pl.BlockSpec((d, tn), lambda j: (0, j)),
            pl.BlockSpec((1, tn), lambda j: (0, j)),
        ],
        out_specs=pl.BlockSpec((rows, tn), lambda j: (0, j)),
        out_shape=jax.ShapeDtypeStruct((rows, n_cols), F32),
        compiler_params=_cparams(("arbitrary",)),
        name="ada_mod",
    )(cc, ada_w, ada_b)


def _prenorm_kernel(x_ref, g_ref, mod_ref, h_ref, *, row_chunk):
    gain = g_ref[...] * (1.0 + mod_ref[0, 1:2, :])
    shift = mod_ref[0, 0:1, :]

    def body(r, carry):
        rows = pl.ds(pl.multiple_of(r * row_chunk, row_chunk), row_chunk)
        xc = x_ref[rows, :]
        var = jnp.mean(xc * xc, axis=-1, keepdims=True)
        h_ref[rows, :] = (xc * lax.rsqrt(var + NORM_EPS) * gain + shift).astype(h_ref.dtype)
        return carry

    lax.fori_loop(0, x_ref.shape[0] // row_chunk, body, 0, unroll=4)


def _prenorm(x2d, pre_g, mod3, *, tm, mod_row_fn, name):
    m, d = x2d.shape
    return pl.pallas_call(
        functools.partial(_prenorm_kernel, row_chunk=16),
        grid=(m // tm,),
        in_specs=[
            pl.BlockSpec((tm, d), lambda i: (i, 0)),
            pl.BlockSpec((1, d), lambda i: (0, 0)),
            pl.BlockSpec((1, 2, d), lambda i: (mod_row_fn(i), 0, 0)),
        ],
        out_specs=pl.BlockSpec((tm, d), lambda i: (i, 0)),
        out_shape=jax.ShapeDtypeStruct((m, d), BF16),
        compiler_params=_cparams(("arbitrary",)),
        name=name,
    )(x2d, pre_g, mod3)


def _inproj_kernel(h_ref, w_ref, o_ref):
    o_ref[...] = jnp.dot(h_ref[...], w_ref[...], preferred_element_type=F32).astype(o_ref.dtype)


def _inproj_gate_kernel(h_ref, w_ref, s_ref, aw_ref, ab_ref, o_ref, g_ref):
    o_ref[...] = jnp.dot(h_ref[...], w_ref[...], preferred_element_type=F32).astype(o_ref.dtype)
    g_ref[...] = _ada_columns(s_ref, aw_ref, ab_ref)


def _inproj_rope_kernel(h_ref, w_ref, cos_ref, sin_ref, *refs, q_tiles, k_tiles, head_dim):
    n_cast = len(refs) // 2
    o_ref = refs[n_cast]
    for src_ref, dst_ref in zip(refs[:n_cast], refs[n_cast + 1:]):
        dst_ref[...] = src_ref[...].astype(dst_ref.dtype)
    j = pl.program_id(1)
    is_rope = j < q_tiles + k_tiles
    scale = jnp.where(j >= q_tiles, head_dim ** -0.5, 1.0).astype(F32)
    half = V7X_LANES // 2
    groups_per_head = head_dim // V7X_LANES
    acc = jnp.dot(h_ref[...], w_ref[...], preferred_element_type=F32)
    for g in range(acc.shape[1] // V7X_LANES):
        lanes = slice(g * V7X_LANES, (g + 1) * V7X_LANES)
        tl = slice((g % groups_per_head) * V7X_LANES, (g % groups_per_head + 1) * V7X_LANES)
        t = acc[:, lanes]
        roped = (t * cos_ref[:, tl] + pltpu.roll(t, half, axis=1) * sin_ref[:, tl]) * scale
        o_ref[:, lanes] = jnp.where(is_rope, roped, t).astype(o_ref.dtype)


def _in_proj(h, w_bf16, *, tm, tn, col_block0, n_col_blocks, name, rope=None, side_casts=(), ada_gate=None):
    m, d = h.shape
    n_steps = (m // tm) * n_col_blocks
    in_specs = [
        pl.BlockSpec((tm, d), lambda i, j: (i, 0)),
        pl.BlockSpec((d, tn), lambda i, j: (0, j + col_block0)),
    ]
    args = [h, w_bf16]
    out_specs = [pl.BlockSpec((tm, tn), lambda i, j: (i, j))]
    out_shape = [jax.ShapeDtypeStruct((m, n_col_blocks * tn), BF16)]
    if rope is None and ada_gate is not None:
        assert not side_casts
        cc, ada_w, ada_b, col0, n_cols = ada_gate
        gw = n_cols // n_steps
        assert gw * n_steps == n_cols and gw % V7X_LANES == 0 and col0 % gw == 0
        step = lambda i, j: i * n_col_blocks + j
        in_specs += [
            pl.BlockSpec(cc.shape, lambda i, j: (0, 0)),
            pl.BlockSpec((ada_w.shape[0], gw), lambda i, j: (0, col0 // gw + step(i, j))),
            pl.BlockSpec((1, gw), lambda i, j: (0, col0 // gw + step(i, j))),
        ]
        args += [cc, ada_w, ada_b]
        out_specs.append(pl.BlockSpec((cc.shape[0], gw), lambda i, j: (0, step(i, j))))
        out_shape.append(jax.ShapeDtypeStruct((cc.shape[0], n_cols), F32))
        kern = _inproj_gate_kernel
    elif rope is None:
        assert not side_casts
        kern = _inproj_kernel
    else:
        assert ada_gate is None
        cos, sin, n_tokens, rope_cols = rope
        head_dim = cos.shape[1]
        tiles_per_seq = n_tokens // tm
        table = pl.BlockSpec((tm, head_dim), lambda i, j: (i % tiles_per_seq, 0))
        in_specs += [table, table]
        args += [cos, sin]
        for src, col_block, width in side_casts:
            rows = src.shape[0]
            assert rows % n_steps == 0 and (rows // n_steps) % V7X_SUBLANES_BF16 == 0
            slab = rows // n_steps
            in_specs.append(pl.BlockSpec((slab, width), lambda i, j, cb=col_block: (i * n_col_blocks + j, cb)))
            out_specs.append(pl.BlockSpec((slab, width), lambda i, j: (i * n_col_blocks + j, 0)))
            out_shape.append(jax.ShapeDtypeStruct((rows, width), BF16))
            args.append(src)
        kern = functools.partial(_inproj_rope_kernel, q_tiles=rope_cols // tn, k_tiles=rope_cols // tn,
                                 head_dim=head_dim)
    outs = pl.pallas_call(
        kern,
        grid=(m // tm, n_col_blocks),
        in_specs=in_specs,
        out_specs=out_specs,
        out_shape=out_shape,
        compiler_params=_cparams(("arbitrary", "arbitrary")),
        name=name,
    )(*args)
    return outs[0] if len(outs) == 1 else outs


def _rope_tables(n_tokens, head_dim):
    rows = n_tokens // GRID_W
    pos_r = np.repeat(np.arange(rows, dtype=np.float64), GRID_W)
    pos_c = np.tile(np.arange(GRID_W, dtype=np.float64), rows)
    n_freq = head_dim // 4
    inv_freq = ROPE_BASE ** (-np.arange(n_freq, dtype=np.float64) / n_freq)
    ang_r = pos_r[:, None] * inv_freq[None, :]
    ang_c = pos_c[:, None] * inv_freq[None, :]
    cos = np.concatenate([np.cos(ang_r), np.cos(ang_r), np.cos(ang_c), np.cos(ang_c)], axis=-1)
    sin = np.concatenate([-np.sin(ang_r), np.sin(ang_r), -np.sin(ang_c), np.sin(ang_c)], axis=-1)
    return jnp.asarray(cos, F32), jnp.asarray(sin, F32)


def _dot_tn(a, b):
    return lax.dot_general(a, b, (((0,), (0,)), ((), ())), preferred_element_type=F32)


def _dot_nt(a, b):
    return lax.dot_general(a, b, (((1,), (1,)), ((), ())), preferred_element_type=F32)


def _retention_kernel(lg_ref, q_ref, k_ref, v_ref, gt_ref, kc_ref, vc_ref, gng_ref,
                      o_ref, c0_ref, r_ref, st_ref, dm_ref, wt_ref, *, chunk, unroll):
    n, dh = q_ref.shape
    nc = n // chunk
    n_ctx = kc_ref.shape[0]
    head = pl.program_id(1)
    lgf = lg_ref[0, head]
    lgb = lg_ref[1, head]

    col = lax.broadcasted_iota(jnp.int32, (chunk, dh), 0).astype(F32)
    rel = (lax.broadcasted_iota(jnp.int32, (chunk, chunk), 0)
           - lax.broadcasted_iota(jnp.int32, (chunk, chunk), 1)).astype(F32)
    dm_ref[...] = (jnp.where(rel >= 0, jnp.exp(lgf * jnp.maximum(rel, 0.0)), 0.0)
                   + jnp.where(rel <= 0, jnp.exp(lgb * jnp.maximum(-rel, 0.0)), 0.0))
    wt_ref[0] = jnp.exp(lgf * (col + 1.0))
    wt_ref[1] = jnp.exp(lgb * (chunk - col))
    wt_ref[2] = jnp.exp(lgf * (chunk - 1.0 - col))
    wt_ref[3] = jnp.exp(lgb * col)
    dec = jnp.concatenate([jnp.exp(jnp.full((1, dh), lgf * chunk, F32)),
                           jnp.exp(jnp.full((1, dh), lgb * chunk, F32))], axis=1)

    def weighted_v(v, w_fwd, w_bwd):
        vf = v.astype(F32)
        return jnp.concatenate([vf * w_fwd, vf * w_bwd], axis=1).astype(BF16)

    pos_c = lax.broadcasted_iota(jnp.int32, (n_ctx, dh), 0).astype(F32)
    kc = (kc_ref[...].astype(F32) * dh ** -0.5).astype(BF16)
    c0_ref[...] = _dot_tn(kc, weighted_v(vc_ref[...], jnp.exp(lgf * (n_ctx - 1.0 - pos_c)), jnp.exp(lgb * pos_c)))
    st_ref[...] = c0_ref[...]

    def scan_body(t, carry):
        cf = t
        cb = nc - 1 - t
        rows_f = pl.ds(pl.multiple_of(cf * chunk, chunk), chunk)
        rows_b = pl.ds(pl.multiple_of(cb * chunk, chunk), chunk)
        state = st_ref[...]
        r_ref[cf, :, 0:dh] = state[:, 0:dh].astype(BF16)
        r_ref[cb, :, dh:2 * dh] = state[:, dh:2 * dh].astype(BF16)
        kv_f = _dot_tn(k_ref[rows_f, :], (v_ref[rows_f, :].astype(F32) * wt_ref[2]).astype(BF16))
        kv_b = _dot_tn(k_ref[rows_b, :], (v_ref[rows_b, :].astype(F32) * wt_ref[3]).astype(BF16))
        st_ref[...] = state * dec + jnp.concatenate([kv_f, kv_b], axis=1)
        return carry

    lax.fori_loop(0, nc, scan_body, 0, unroll=unroll)

    gng = gng_ref[...]

    def out_body(c, carry):
        rows = pl.ds(pl.multiple_of(c * chunk, chunk), chunk)
        q = q_ref[rows, :]
        v = v_ref[rows, :]
        scores = _dot_nt(q, k_ref[rows, :]) * dm_ref[...]
        o = jnp.dot(scores.astype(BF16), v, preferred_element_type=F32)
        x = jnp.dot(q, r_ref[c], preferred_element_type=F32)
        o = o + x[:, 0:dh] * wt_ref[0] + x[:, dh:2 * dh] * wt_ref[1]
        mu = jnp.mean(o, axis=-1, keepdims=True)
        oc = o - mu
        var = jnp.mean(oc * oc, axis=-1, keepdims=True)
        gt = gt_ref[rows, :].astype(F32)
        y = oc * lax.rsqrt(var + NORM_EPS) * gng
        o_ref[rows, :] = (y * _silu(gt)).astype(o_ref.dtype)
        return carry

    lax.fori_loop(0, nc, out_body, 0, unroll=unroll)


def _retention(lg, proj, kv_c, gn_g, *, batch, n, n_ctx, heads, dh):
    r = heads * dh
    nc = n // RET_CHUNK
    kern = functools.partial(_retention_kernel, chunk=RET_CHUNK, unroll=2)
    return pl.pallas_call(
        kern,
        grid=(batch, heads),
        in_specs=[
            pl.BlockSpec(memory_space=pltpu.SMEM),
            pl.BlockSpec((n, dh), lambda b, h: (b, h)),
            pl.BlockSpec((n, dh), lambda b, h: (b, heads + h)),
            pl.BlockSpec((n, dh), lambda b, h: (b, 2 * heads + h)),
            pl.BlockSpec((n, dh), lambda b, h: (b, 3 * heads + h)),
            pl.BlockSpec((n_ctx, dh), lambda b, h: (b, h)),
            pl.BlockSpec((n_ctx, dh), lambda b, h: (b, heads + h)),
            pl.BlockSpec((1, dh), lambda b, h: (0, h)),
        ],
        out_specs=pl.BlockSpec((n, dh), lambda b, h: (b, h)),
        out_shape=jax.ShapeDtypeStruct((batch * n, r), BF16),
        scratch_shapes=[
            pltpu.VMEM((dh, 2 * dh), F32),
            pltpu.VMEM((nc, dh, 2 * dh), BF16),
            pltpu.VMEM((dh, 2 * dh), F32),
            pltpu.VMEM((RET_CHUNK, RET_CHUNK), F32),
            pltpu.VMEM((4, RET_CHUNK, dh), F32),
        ],
        compiler_params=_cparams(("arbitrary", "arbitrary")),
        name="retention",
    )(lg, proj, proj, proj, proj, kv_c, kv_c, gn_g)


def _conv_kernel(a_ref, b_ref, ap_ref, bp_ref, an_ref, bn_ref, gc_ref, w_ref, cb_ref, lg_ref, lb_ref,
                 o_ref, u_ref, y_ref, w3_ref, s1_ref, mu_ref, rs_ref, *, tiles_per_seq, row_chunk, conv_rows):
    tn, cw = a_ref.shape
    halo = ap_ref.shape[0]
    taps = w_ref.shape[0]
    pad = taps // 2
    n_groups = cw // V7X_LANES
    il = pl.program_id(0) % tiles_per_seq

    def glu(a, b):
        return _mul_sigmoid(a.astype(F32), b.astype(F32))

    def lane_group(val, g):
        return val[:, g * V7X_LANES:(g + 1) * V7X_LANES]

    u_prev = jnp.where(il > 0, glu(ap_ref[...], bp_ref[...]), 0.0)
    u_next = jnp.where(il < tiles_per_seq - 1, glu(an_ref[...], bn_ref[...]), 0.0)
    for g in range(n_groups):
        u_ref[g, 0:halo, :] = lane_group(u_prev, g)
        u_ref[g, halo + tn:2 * halo + tn, :] = lane_group(u_next, g)
        w3_ref[g, 0:taps, :] = lane_group(w_ref[...], g)
        w3_ref[g, taps:taps + 1, :] = lane_group(cb_ref[...], g)

    def glu_body(r, carry):
        r0 = pl.multiple_of(r * row_chunk, row_chunk)
        val = glu(a_ref[pl.ds(r0, row_chunk), :], b_ref[pl.ds(r0, row_chunk), :])
        for g in range(n_groups):
            u_ref[g, pl.ds(halo + r0, row_chunk), :] = lane_group(val, g)
        return carry

    lax.fori_loop(0, tn // row_chunk, glu_body, 0)

    s1_ref[...] = jnp.zeros_like(s1_ref)

    def conv_body(g, carry):
        for r0 in range(0, tn, conv_rows):
            acc = jnp.broadcast_to(w3_ref[g, taps:taps + 1, :], (conv_rows, V7X_LANES))
            for t in range(taps):
                s0 = r0 + halo - pad + t
                acc = acc + u_ref[g, s0:s0 + conv_rows, :] * w3_ref[g, t:t + 1, :]
            y_ref[g, r0:r0 + conv_rows, :] = acc
            s1_ref[r0:r0 + conv_rows, :] += acc
        return carry

    lax.fori_loop(0, n_groups, conv_body, 0)

    inv_cw = 1.0 / cw
    for r0 in range(0, tn, conv_rows):
        rows = slice(r0, r0 + conv_rows)
        mu = jnp.broadcast_to(jnp.sum(s1_ref[rows, :], axis=-1, keepdims=True) * inv_cw, (conv_rows, V7X_LANES))
        sq = jnp.zeros((conv_rows, V7X_LANES), F32)
        for g in range(n_groups):
            dlt = y_ref[g, rows, :] - mu
            sq = sq + dlt * dlt
        var = jnp.sum(sq, axis=-1, keepdims=True) * inv_cw
        mu_ref[rows, :] = mu
        rs_ref[rows, :] = jnp.broadcast_to(lax.rsqrt(var + NORM_EPS), (conv_rows, V7X_LANES))

    def row_body(r, carry):
        r0 = pl.multiple_of(r * row_chunk, row_chunk)
        rows = pl.ds(r0, row_chunk)
        mu = mu_ref[rows, :]
        rs = rs_ref[rows, :]
        for g in range(n_groups):
            lanes = slice(g * V7X_LANES, (g + 1) * V7X_LANES)
            z = (y_ref[g, rows, :] - mu) * rs * lg_ref[:, lanes] + lb_ref[:, lanes]
            gate = gc_ref[rows, lanes].astype(F32)
            o_ref[rows, lanes] = (_silu(z) * _silu(gate)).astype(o_ref.dtype)
        return carry

    lax.fori_loop(0, tn // row_chunk, row_body, 0)


def _conv_branch(glu_a, glu_b, gate, dw_w, dw_b, ln_g, ln_b, *, n, tn=256):
    m, cw = glu_a.shape
    halo = CONV_HALO
    tiles_per_seq = n // tn
    hb = tn // halo
    last_hb = m // halo - 1
    n_groups = cw // V7X_LANES
    kern = functools.partial(_conv_kernel, tiles_per_seq=tiles_per_seq, row_chunk=16, conv_rows=64)
    main = pl.BlockSpec((tn, cw), lambda i: (i, 0))
    prev = pl.BlockSpec((halo, cw), lambda i: (jnp.maximum(i * hb - 1, 0), 0))
    nxt = pl.BlockSpec((halo, cw), lambda i: (jnp.minimum((i + 1) * hb, last_hb), 0))
    vec = lambda rows: pl.BlockSpec((rows, cw), lambda i: (0, 0))
    return pl.pallas_call(
        kern,
        grid=(m // tn,),
        in_specs=[main, main, prev, prev, nxt, nxt, main,
                  vec(dw_w.shape[0]), vec(1), vec(1), vec(1)],
        out_specs=pl.BlockSpec((tn, cw), lambda i: (i, 0)),
        out_shape=jax.ShapeDtypeStruct((m, cw), BF16),
        scratch_shapes=[
            pltpu.VMEM((n_groups, tn + 2 * halo, V7X_LANES), F32),
            pltpu.VMEM((n_groups, tn, V7X_LANES), F32),
            pltpu.VMEM((n_groups, dw_w.shape[0] + 1, V7X_LANES), F32),
            pltpu.VMEM((tn, V7X_LANES), F32),
            pltpu.VMEM((tn, V7X_LANES), F32),
            pltpu.VMEM((tn, V7X_LANES), F32),
        ],
        compiler_params=_cparams(("arbitrary",)),
        name="conv_branch",
    )(glu_a, glu_b, glu_a, glu_b, glu_a, glu_b, gate, dw_w, dw_b, ln_g, ln_b)


def _outproj_kernel(r_ref, c_ref, w1_ref, w2_ref, x_ref, gate_ref, pg_ref, o_ref, rs_ref, *, tn, row_chunk):
    tm, d = o_ref.shape
    ssq = jnp.zeros((tm, V7X_LANES), F32)
    for j0 in range(0, d, tn):
        cols = slice(j0, j0 + tn)
        y = jnp.dot(r_ref[...], w1_ref[:, cols], preferred_element_type=F32)
        y = y + jnp.dot(c_ref[...], w2_ref[:, cols], preferred_element_type=F32)
        o_ref[:, cols] = y
        for g0 in range(0, tn, V7X_LANES):
            yg = y[:, g0:g0 + V7X_LANES]
            ssq = ssq + yg * yg
    var = jnp.sum(ssq, axis=-1, keepdims=True) * (1.0 / d)
    rs_ref[...] = jnp.broadcast_to(lax.rsqrt(var + NORM_EPS), (tm, V7X_LANES))

    def body(r, carry):
        rows = pl.ds(pl.multiple_of(r * row_chunk, row_chunk), row_chunk)
        rs = rs_ref[rows, :]
        for g0 in range(0, d, V7X_LANES):
            lanes = slice(g0, g0 + V7X_LANES)
            gain = gate_ref[0, :, lanes] * pg_ref[:, lanes]
            o_ref[rows, lanes] = x_ref[rows, lanes] + o_ref[rows, lanes] * rs * gain
        return carry

    lax.fori_loop(0, tm // row_chunk, body, 0)


def _out_proj(ret_b, conv_b, w_out_b, x2d, gate3, post_g, *, tm, tiles_per_batch):
    m, d = x2d.shape
    r = ret_b.shape[1]
    cw = conv_b.shape[1]
    assert r == cw and w_out_b.shape == (r + cw, d)
    kern = functools.partial(_outproj_kernel, tn=512, row_chunk=16)
    return pl.pallas_call(
        kern,
        grid=(m // tm,),
        in_specs=[
            pl.BlockSpec((tm, r), lambda i: (i, 0)),
            pl.BlockSpec((tm, cw), lambda i: (i, 0)),
            pl.BlockSpec((r, d), lambda i: (0, 0), pipeline_mode=pl.Buffered(1)),
            pl.BlockSpec((cw, d), lambda i: (1, 0), pipeline_mode=pl.Buffered(1)),
            pl.BlockSpec((tm, d), lambda i: (i, 0)),
            pl.BlockSpec((1, 1, d), lambda i: (i // tiles_per_batch, 0, 0)),
            pl.BlockSpec((1, d), lambda i: (0, 0)),
        ],
        out_specs=pl.BlockSpec((tm, d), lambda i: (i, 0)),
        out_shape=jax.ShapeDtypeStruct((m, d), F32),
        scratch_shapes=[pltpu.VMEM((tm, V7X_LANES), F32)],
        compiler_params=_cparams(("arbitrary",)),
        name="out_proj",
    )(ret_b, conv_b, w_out_b, w_out_b, x2d, gate3, post_g)


def kernel(x, c, ctx, c_ctx, ada_w, ada_b, pre_norm_g, post_norm_g, w_in, ret_log_decay_fwd,
           ret_log_decay_bwd, ret_gn_g, conv_dw_w, conv_dw_b, conv_ln_g, conv_ln_b, w_out):
    batch, n, d = x.shape
    n_ctx = ctx.shape[1]
    depth = ada_w.shape[0]
    r = ret_gn_g.shape[1]
    cw = conv_dw_w.shape[2]
    heads = RET_HEADS
    dh = r // heads
    assert depth == 1, "single-layer block: the context stream is never updated"
    assert w_in.shape[2] == 4 * r + 3 * cw and r == cw

    cos, sin = _rope_tables(n, dh)
    x2d = x.reshape(batch * n, d)
    ctx2d = ctx.reshape(batch * n_ctx, d)

    mod_rows = 8
    cc = jnp.concatenate([c, c_ctx[None, :], jnp.zeros((mod_rows - batch - 1, d), F32)], axis=0)
    mod2 = _ada_mod(cc, ada_w[0], ada_b, 2 * d).reshape(mod_rows, 2, d)

    ret_cols = 4 * r
    w_ret_b = w_in[0, :, :ret_cols].astype(BF16)

    tm_norm = 512
    h = _prenorm(x2d, pre_norm_g, mod2, tm=tm_norm, mod_row_fn=lambda i: i // (n // tm_norm), name="prenorm")
    h_ctx = _prenorm(ctx2d, pre_norm_g, mod2, tm=tm_norm, mod_row_fn=lambda i: batch, name="prenorm_ctx")

    tm_in, tn_in = 1024, 1024
    side_casts = [(w_in[0], ret_cols // cw + k, cw) for k in range(3)] + [(w_out[0], 0, d)]
    proj, w_glu_a, w_glu_b, w_gate, w_out_b = _in_proj(
        h, w_ret_b, tm=tm_in, tn=tn_in, col_block0=0, n_col_blocks=ret_cols // tn_in, name="in_proj",
        rope=(cos, sin, n, r), side_casts=side_casts)
    kv_c = _in_proj(h_ctx, w_ret_b, tm=batch * n_ctx, tn=tn_in, col_block0=r // tn_in,
                    n_col_blocks=2 * r // tn_in, name="in_proj_ctx")
    glu_a, ada_gate = _in_proj(h, w_glu_a, tm=tm_in, tn=tn_in, col_block0=0, n_col_blocks=cw // tn_in,
                               name="in_proj_glu_a", ada_gate=(cc, ada_w[0], ada_b, 2 * d, d))
    glu_b, gate_c = [
        _in_proj(h, w_k, tm=tm_in, tn=tn_in, col_block0=0, n_col_blocks=cw // tn_in, name=name)
        for w_k, name in ((w_glu_b, "in_proj_glu_b"), (w_gate, "in_proj_conv_gate"))]

    lg = jnp.concatenate([ret_log_decay_fwd, ret_log_decay_bwd], axis=0)
    ret_b = _retention(lg, proj, kv_c, ret_gn_g, batch=batch, n=n, n_ctx=n_ctx, heads=heads, dh=dh)
    conv_b = _conv_branch(glu_a, glu_b, gate_c, conv_dw_w[0], conv_dw_b, conv_ln_g, conv_ln_b, n=n)

    tm_out = 256
    out = _out_proj(ret_b, conv_b, w_out_b, x2d, ada_gate.reshape(mod_rows, 1, d), post_norm_g,
                    tm=tm_out, tiles_per_batch=n // tm_out)
    return out.reshape(batch, n, d)
```

```python
import functools

import numpy as np
import jax
import jax.numpy as jnp
from jax import lax
from jax.experimental import pallas as pl
from jax.experimental.pallas import tpu as pltpu

F32 = jnp.float32
BF16 = jnp.bfloat16

GRID_W = 64
RET_HEADS = 8
CONV_KERNEL = 31
ROPE_BASE = 10000.0
NORM_EPS = 1e-6

V7X_LANES = 128
V7X_SUBLANES_BF16 = 16
V7X_VMEM_LIMIT_BYTES = 58 * 1024 * 1024

RET_CHUNK = 256
CONV_HALO = 16


def _cparams(semantics):
    return pltpu.CompilerParams(dimension_semantics=semantics, vmem_limit_bytes=V7X_VMEM_LIMIT_BYTES)


def _silu(x):
    h = 0.5 * x
    return h + h * jnp.tanh(h)


def _mul_sigmoid(a, b):
    ha = 0.5 * a
    return ha + ha * jnp.tanh(0.5 * b)


def _ada_columns(s_ref, w_ref, b_ref):
    s = s_ref[...]
    s = s * jax.nn.sigmoid(s)
    acc = jnp.dot(s.astype(BF16), w_ref[...].astype(BF16), preferred_element_type=F32)
    return acc + b_ref[...]


def _ada_kernel(s_ref, w_ref, b_ref, o_ref):
    o_ref[...] = _ada_columns(s_ref, w_ref, b_ref)


def _ada_mod(cc, ada_w, ada_b, n_cols, tn=512):
    rows, d = cc.shape
    return pl.pallas_call(
        _ada_kernel,
        grid=(n_cols // tn,),
        in_specs=[
            pl.BlockSpec((rows, d), lambda j: (0, 0)),
            pl.BlockSpec((d, tn), lambda j: (0, j)),
            pl.BlockSpec((1, tn), lambda j: (0, j)),
        ],
        out_specs=pl.BlockSpec((rows, tn), lambda j: (0, j)),
        out_shape=jax.ShapeDtypeStruct((rows, n_cols), F32),
        compiler_params=_cparams(("arbitrary",)),
        name="ada_mod",
    )(cc, ada_w, ada_b)


def _prenorm_kernel(x_ref, g_ref, mod_ref, h_ref, *, row_chunk):
    gain = g_ref[...] * (1.0 + mod_ref[0, 1:2, :])
    shift = mod_ref[0, 0:1, :]

    def body(r, carry):
        rows = pl.ds(pl.multiple_of(r * row_chunk, row_chunk), row_chunk)
        xc = x_ref[rows, :]
        var = jnp.mean(xc * xc, axis=-1, keepdims=True)
        h_ref[rows, :] = (xc * lax.rsqrt(var + NORM_EPS) * gain + shift).astype(h_ref.dtype)
        return carry

    lax.fori_loop(0, x_ref.shape[0] // row_chunk, body, 0, unroll=4)


def _prenorm(x2d, pre_g, mod3, *, tm, mod_row_fn, name):
    m, d = x2d.shape
    return pl.pallas_call(
        functools.partial(_prenorm_kernel, row_chunk=16),
        grid=(m // tm,),
        in_specs=[
            pl.BlockSpec((tm, d), lambda i: (i, 0)),
            pl.BlockSpec((1, d), lambda i: (0, 0)),
            pl.BlockSpec((1, 2, d), lambda i: (mod_row_fn(i), 0, 0)),
        ],
        out_specs=pl.BlockSpec((tm, d), lambda i: (i, 0)),
        out_shape=jax.ShapeDtypeStruct((m, d), BF16),
        compiler_params=_cparams(("arbitrary",)),
        name=name,
    )(x2d, pre_g, mod3)


def _inproj_kernel(h_ref, w_ref, o_ref):
    o_ref[...] = jnp.dot(h_ref[...], w_ref[...], preferred_element_type=F32).astype(o_ref.dtype)


def _inproj_gate_kernel(h_ref, w_ref, s_ref, aw_ref, ab_ref, o_ref, g_ref):
    o_ref[...] = jnp.dot(h_ref[...], w_ref[...], preferred_element_type=F32).astype(o_ref.dtype)
    g_ref[...] = _ada_columns(s_ref, aw_ref, ab_ref)


def _inproj_rope_kernel(h_ref, w_ref, cos_ref, sin_ref, *refs, q_tiles, k_tiles, head_dim):
    n_cast = len(refs) // 2
    o_ref = refs[n_cast]
    for src_ref, dst_ref in zip(refs[:n_cast], refs[n_cast + 1:]):
        dst_ref[...] = src_ref[...].astype(dst_ref.dtype)
    j = pl.program_id(1)
    is_rope = j < q_tiles + k_tiles
    scale = jnp.where(j >= q_tiles, head_dim ** -0.5, 1.0).astype(F32)
    half = V7X_LANES // 2
    groups_per_head = head_dim // V7X_LANES
    acc = jnp.dot(h_ref[...], w_ref[...], preferred_element_type=F32)
    for g in range(acc.shape[1] // V7X_LANES):
        lanes = slice(g * V7X_LANES, (g + 1) * V7X_LANES)
        tl = slice((g % groups_per_head) * V7X_LANES, (g % groups_per_head + 1) * V7X_LANES)
        t = acc[:, lanes]
        roped = (t * cos_ref[:, tl] + pltpu.roll(t, half, axis=1) * sin_ref[:, tl]) * scale
        o_ref[g // groups_per_head, :, tl] = jnp.where(is_rope, roped, t).astype(o_ref.dtype)


def _in_proj(h, w_bf16, *, tm, tn, col_block0, n_col_blocks, name, rope=None, side_casts=(), ada_gate=None):
    m, d = h.shape
    n_steps = (m // tm) * n_col_blocks
    in_specs = [
        pl.BlockSpec((tm, d), lambda i, j: (i, 0)),
        pl.BlockSpec((d, tn), lambda i, j: (0, j + col_block0)),
    ]
    args = [h, w_bf16]
    out_specs = [pl.BlockSpec((tm, tn), lambda i, j: (i, j))]
    out_shape = [jax.ShapeDtypeStruct((m, n_col_blocks * tn), BF16)]
    if rope is None and ada_gate is not None:
        assert not side_casts
        cc, ada_w, ada_b, col0, n_cols = ada_gate
        gw = n_cols // n_steps
        assert gw * n_steps == n_cols and gw % V7X_LANES == 0 and col0 % gw == 0
        step = lambda i, j: i * n_col_blocks + j
        in_specs += [
            pl.BlockSpec(cc.shape, lambda i, j: (0, 0)),
            pl.BlockSpec((ada_w.shape[0], gw), lambda i, j: (0, col0 // gw + step(i, j))),
            pl.BlockSpec((1, gw), lambda i, j: (0, col0 // gw + step(i, j))),
        ]
        args += [cc, ada_w, ada_b]
        out_specs.append(pl.BlockSpec((cc.shape[0], gw), lambda i, j: (0, step(i, j))))
        out_shape.append(jax.ShapeDtypeStruct((cc.shape[0], n_cols), F32))
        kern = _inproj_gate_kernel
    elif rope is None:
        assert not side_casts
        kern = _inproj_kernel
    else:
        assert ada_gate is None
        cos, sin, n_tokens, rope_cols = rope
        head_dim = cos.shape[1]
        tiles_per_seq = n_tokens // tm
        table = pl.BlockSpec((tm, head_dim), lambda i, j: (i % tiles_per_seq, 0))
        in_specs += [table, table]
        args += [cos, sin]
        heads_per_tile = tn // head_dim
        out_specs = [pl.BlockSpec((heads_per_tile, tm, head_dim), lambda i, j: (j, i, 0))]
        out_shape = [jax.ShapeDtypeStruct((n_col_blocks * heads_per_tile, m, head_dim), BF16)]
        for src, col_block, width in side_casts:
            rows = src.shape[0]
            assert rows % n_steps == 0 and (rows // n_steps) % V7X_SUBLANES_BF16 == 0
            slab = rows // n_steps
            in_specs.append(pl.BlockSpec((slab, width), lambda i, j, cb=col_block: (i * n_col_blocks + j, cb)))
            out_specs.append(pl.BlockSpec((slab, width), lambda i, j: (i * n_col_blocks + j, 0)))
            out_shape.append(jax.ShapeDtypeStruct((rows, width), BF16))
            args.append(src)
        kern = functools.partial(_inproj_rope_kernel, q_tiles=rope_cols // tn, k_tiles=rope_cols // tn,
                                 head_dim=head_dim)
    outs = pl.pallas_call(
        kern,
        grid=(m // tm, n_col_blocks),
        in_specs=in_specs,
        out_specs=out_specs,
        out_shape=out_shape,
        compiler_params=_cparams(("arbitrary", "arbitrary")),
        name=name,
    )(*args)
    return outs[0] if len(outs) == 1 else outs


def _rope_tables(n_tokens, head_dim):
    rows = n_tokens // GRID_W
    pos_r = np.repeat(np.arange(rows, dtype=np.float64), GRID_W)
    pos_c = np.tile(np.arange(GRID_W, dtype=np.float64), rows)
    n_freq = head_dim // 4
    inv_freq = ROPE_BASE ** (-np.arange(n_freq, dtype=np.float64) / n_freq)
    ang_r = pos_r[:, None] * inv_freq[None, :]
    ang_c = pos_c[:, None] * inv_freq[None, :]
    cos = np.concatenate([np.cos(ang_r), np.cos(ang_r), np.cos(ang_c), np.cos(ang_c)], axis=-1)
    sin = np.concatenate([-np.sin(ang_r), np.sin(ang_r), -np.sin(ang_c), np.sin(ang_c)], axis=-1)
    return jnp.asarray(cos, F32), jnp.asarray(sin, F32)


def _dot_tn(a, b):
    return lax.dot_general(a, b, (((0,), (0,)), ((), ())), preferred_element_type=F32)


def _dot_nt(a, b):
    return lax.dot_general(a, b, (((1,), (1,)), ((), ())), preferred_element_type=F32)


def _retention_kernel(lg_ref, q_ref, k_ref, v_ref, gt_ref, kc_ref, vc_ref, gng_ref,
                      o_ref, c0_ref, r_ref, st_ref, dm_ref, wt_ref, *, chunk, unroll):
    n, dh = q_ref.shape
    nc = n // chunk
    n_ctx = kc_ref.shape[0]
    head = pl.program_id(1)
    lgf = lg_ref[0, head]
    lgb = lg_ref[1, head]

    col = lax.broadcasted_iota(jnp.int32, (chunk, dh), 0).astype(F32)
    rel = (lax.broadcasted_iota(jnp.int32, (chunk, chunk), 0)
           - lax.broadcasted_iota(jnp.int32, (chunk, chunk), 1)).astype(F32)
    dm_ref[...] = (jnp.where(rel >= 0, jnp.exp(lgf * jnp.maximum(rel, 0.0)), 0.0)
                   + jnp.where(rel <= 0, jnp.exp(lgb * jnp.maximum(-rel, 0.0)), 0.0))
    wt_ref[0] = jnp.exp(lgf * (col + 1.0))
    wt_ref[1] = jnp.exp(lgb * (chunk - col))
    wt_ref[2] = jnp.exp(lgf * (chunk - 1.0 - col))
    wt_ref[3] = jnp.exp(lgb * col)
    dec = jnp.concatenate([jnp.exp(jnp.full((1, dh), lgf * chunk, F32)),
                           jnp.exp(jnp.full((1, dh), lgb * chunk, F32))], axis=1)

    def weighted_v(v, w_fwd, w_bwd):
        vf = v.astype(F32)
        return jnp.concatenate([vf * w_fwd, vf * w_bwd], axis=1).astype(BF16)

    pos_c = lax.broadcasted_iota(jnp.int32, (n_ctx, dh), 0).astype(F32)
    kc = (kc_ref[...].astype(F32) * dh ** -0.5).astype(BF16)
    c0_ref[...] = _dot_tn(kc, weighted_v(vc_ref[...], jnp.exp(lgf * (n_ctx - 1.0 - pos_c)), jnp.exp(lgb * pos_c)))
    st_ref[...] = c0_ref[...]

    def scan_body(t, carry):
        cf = t
        cb = nc - 1 - t
        rows_f = pl.ds(pl.multiple_of(cf * chunk, chunk), chunk)
        rows_b = pl.ds(pl.multiple_of(cb * chunk, chunk), chunk)
        state = st_ref[...]
        r_ref[cf, :, 0:dh] = state[:, 0:dh].astype(BF16)
        r_ref[cb, :, dh:2 * dh] = state[:, dh:2 * dh].astype(BF16)
        kv_f = _dot_tn(k_ref[rows_f, :], (v_ref[rows_f, :].astype(F32) * wt_ref[2]).astype(BF16))
        kv_b = _dot_tn(k_ref[rows_b, :], (v_ref[rows_b, :].astype(F32) * wt_ref[3]).astype(BF16))
        st_ref[...] = state * dec + jnp.concatenate([kv_f, kv_b], axis=1)
        return carry

    lax.fori_loop(0, nc, scan_body, 0, unroll=unroll)

    gng = gng_ref[...]

    def out_body(c, carry):
        rows = pl.ds(pl.multiple_of(c * chunk, chunk), chunk)
        q = q_ref[rows, :]
        v = v_ref[rows, :]
        scores = _dot_nt(q, k_ref[rows, :]) * dm_ref[...]
        o = jnp.dot(scores.astype(BF16), v, preferred_element_type=F32)
        x = jnp.dot(q, r_ref[c], preferred_element_type=F32)
        o = o + x[:, 0:dh] * wt_ref[0] + x[:, dh:2 * dh] * wt_ref[1]
        mu = jnp.mean(o, axis=-1, keepdims=True)
        oc = o - mu
        var = jnp.mean(oc * oc, axis=-1, keepdims=True)
        gt = gt_ref[rows, :].astype(F32)
        y = oc * lax.rsqrt(var + NORM_EPS) * gng
        o_ref[rows, :] = (y * _silu(gt)).astype(o_ref.dtype)
        return carry

    lax.fori_loop(0, nc, out_body, 0, unroll=unroll)


def _retention(lg, proj, kv_c, gn_g, *, batch, n, n_ctx, heads, dh):
    r = heads * dh
    nc = n // RET_CHUNK
    kern = functools.partial(_retention_kernel, chunk=RET_CHUNK, unroll=2)
    return pl.pallas_call(
        kern,
        grid=(batch, heads),
        in_specs=[
            pl.BlockSpec(memory_space=pltpu.SMEM),
            pl.BlockSpec((None, n, dh), lambda b, h: (h, b, 0)),
            pl.BlockSpec((None, n, dh), lambda b, h: (heads + h, b, 0)),
            pl.BlockSpec((None, n, dh), lambda b, h: (2 * heads + h, b, 0)),
            pl.BlockSpec((None, n, dh), lambda b, h: (3 * heads + h, b, 0)),
            pl.BlockSpec((n_ctx, dh), lambda b, h: (b, h)),
            pl.BlockSpec((n_ctx, dh), lambda b, h: (b, heads + h)),
            pl.BlockSpec((1, dh), lambda b, h: (0, h)),
        ],
        out_specs=pl.BlockSpec((None, n, dh), lambda b, h: (h, b, 0)),
        out_shape=jax.ShapeDtypeStruct((heads, batch * n, dh), BF16),
        scratch_shapes=[
            pltpu.VMEM((dh, 2 * dh), F32),
            pltpu.VMEM((nc, dh, 2 * dh), BF16),
            pltpu.VMEM((dh, 2 * dh), F32),
            pltpu.VMEM((RET_CHUNK, RET_CHUNK), F32),
            pltpu.VMEM((4, RET_CHUNK, dh), F32),
        ],
        compiler_params=_cparams(("arbitrary", "arbitrary")),
        name="retention",
    )(lg, proj, proj, proj, proj, kv_c, kv_c, gn_g)


def _conv_kernel(a_ref, b_ref, ap_ref, bp_ref, an_ref, bn_ref, gc_ref, w_ref, cb_ref, lg_ref, lb_ref,
                 o_ref, u_ref, y_ref, w3_ref, s1_ref, mu_ref, rs_ref, *, tiles_per_seq, row_chunk, conv_rows):
    tn, cw = a_ref.shape
    halo = ap_ref.shape[0]
    taps = w_ref.shape[0]
    pad = taps // 2
    n_groups = cw // V7X_LANES
    il = pl.program_id(0) % tiles_per_seq

    def glu(a, b):
        return _mul_sigmoid(a.astype(F32), b.astype(F32))

    def lane_group(val, g):
        return val[:, g * V7X_LANES:(g + 1) * V7X_LANES]

    u_prev = jnp.where(il > 0, glu(ap_ref[...], bp_ref[...]), 0.0)
    u_next = jnp.where(il < tiles_per_seq - 1, glu(an_ref[...], bn_ref[...]), 0.0)
    for g in range(n_groups):
        u_ref[g, 0:halo, :] = lane_group(u_prev, g)
        u_ref[g, halo + tn:2 * halo + tn, :] = lane_group(u_next, g)
        w3_ref[g, 0:taps, :] = lane_group(w_ref[...], g)
        w3_ref[g, taps:taps + 1, :] = lane_group(cb_ref[...], g)

    def glu_body(r, carry):
        r0 = pl.multiple_of(r * row_chunk, row_chunk)
        val = glu(a_ref[pl.ds(r0, row_chunk), :], b_ref[pl.ds(r0, row_chunk), :])
        for g in range(n_groups):
            u_ref[g, pl.ds(halo + r0, row_chunk), :] = lane_group(val, g)
        return carry

    lax.fori_loop(0, tn // row_chunk, glu_body, 0)

    s1_ref[...] = jnp.zeros_like(s1_ref)

    def conv_body(g, carry):
        for r0 in range(0, tn, conv_rows):
            acc = jnp.broadcast_to(w3_ref[g, taps:taps + 1, :], (conv_rows, V7X_LANES))
            for t in range(taps):
                s0 = r0 + halo - pad + t
                acc = acc + u_ref[g, s0:s0 + conv_rows, :] * w3_ref[g, t:t + 1, :]
            y_ref[g, r0:r0 + conv_rows, :] = acc
            s1_ref[r0:r0 + conv_rows, :] += acc
        return carry

    lax.fori_loop(0, n_groups, conv_body, 0)

    inv_cw = 1.0 / cw
    for r0 in range(0, tn, conv_rows):
        rows = slice(r0, r0 + conv_rows)
        mu = jnp.broadcast_to(jnp.sum(s1_ref[rows, :], axis=-1, keepdims=True) * inv_cw, (conv_rows, V7X_LANES))
        sq = jnp.zeros((conv_rows, V7X_LANES), F32)
        for g in range(n_groups):
            dlt = y_ref[g, rows, :] - mu
            sq = sq + dlt * dlt
        var = jnp.sum(sq, axis=-1, keepdims=True) * inv_cw
        mu_ref[rows, :] = mu
        rs_ref[rows, :] = jnp.broadcast_to(lax.rsqrt(var + NORM_EPS), (conv_rows, V7X_LANES))

    def row_body(r, carry):
        r0 = pl.multiple_of(r * row_chunk, row_chunk)
        rows = pl.ds(r0, row_chunk)
        mu = mu_ref[rows, :]
        rs = rs_ref[rows, :]
        for g in range(n_groups):
            lanes = slice(g * V7X_LANES, (g + 1) * V7X_LANES)
            z = (y_ref[g, rows, :] - mu) * rs * lg_ref[:, lanes] + lb_ref[:, lanes]
            gate = gc_ref[rows, lanes].astype(F32)
            o_ref[rows, lanes] = (_silu(z) * _silu(gate)).astype(o_ref.dtype)
        return carry

    lax.fori_loop(0, tn // row_chunk, row_body, 0)


def _conv_branch(glu_a, glu_b, gate, dw_w, dw_b, ln_g, ln_b, *, n, tn=256):
    m, cw = glu_a.shape
    halo = CONV_HALO
    tiles_per_seq = n // tn
    hb = tn // halo
    last_hb = m // halo - 1
    n_groups = cw // V7X_LANES
    kern = functools.partial(_conv_kernel, tiles_per_seq=tiles_per_seq, row_chunk=16, conv_rows=64)
    main = pl.BlockSpec((tn, cw), lambda i: (i, 0))
    prev = pl.BlockSpec((halo, cw), lambda i: (jnp.maximum(i * hb - 1, 0), 0))
    nxt = pl.BlockSpec((halo, cw), lambda i: (jnp.minimum((i + 1) * hb, last_hb), 0))
    vec = lambda rows: pl.BlockSpec((rows, cw), lambda i: (0, 0))
    return pl.pallas_call(
        kern,
        grid=(m // tn,),
        in_specs=[main, main, prev, prev, nxt, nxt, main,
                  vec(dw_w.shape[0]), vec(1), vec(1), vec(1)],
        out_specs=pl.BlockSpec((tn, cw), lambda i: (i, 0)),
        out_shape=jax.ShapeDtypeStruct((m, cw), BF16),
        scratch_shapes=[
            pltpu.VMEM((n_groups, tn + 2 * halo, V7X_LANES), F32),
            pltpu.VMEM((n_groups, tn, V7X_LANES), F32),
            pltpu.VMEM((n_groups, dw_w.shape[0] + 1, V7X_LANES), F32),
            pltpu.VMEM((tn, V7X_LANES), F32),
            pltpu.VMEM((tn, V7X_LANES), F32),
            pltpu.VMEM((tn, V7X_LANES), F32),
        ],
        compiler_params=_cparams(("arbitrary",)),
        name="conv_branch",
    )(glu_a, glu_b, glu_a, glu_b, glu_a, glu_b, gate, dw_w, dw_b, ln_g, ln_b)


def _outproj_kernel(r_ref, c_ref, w1_ref, w2_ref, x_ref, gate_ref, pg_ref, o_ref, rs_ref, *, tn, row_chunk):
    tm, d = o_ref.shape
    ssq = jnp.zeros((tm, V7X_LANES), F32)
    ret = jnp.concatenate([r_ref[hd] for hd in range(r_ref.shape[0])], axis=1)
    for j0 in range(0, d, tn):
        cols = slice(j0, j0 + tn)
        y = jnp.dot(ret, w1_ref[:, cols], preferred_element_type=F32)
        y = y + jnp.dot(c_ref[...], w2_ref[:, cols], preferred_element_type=F32)
        o_ref[:, cols] = y
        for g0 in range(0, tn, V7X_LANES):
            yg = y[:, g0:g0 + V7X_LANES]
            ssq = ssq + yg * yg
    var = jnp.sum(ssq, axis=-1, keepdims=True) * (1.0 / d)
    rs_ref[...] = jnp.broadcast_to(lax.rsqrt(var + NORM_EPS), (tm, V7X_LANES))

    def body(r, carry):
        rows = pl.ds(pl.multiple_of(r * row_chunk, row_chunk), row_chunk)
        rs = rs_ref[rows, :]
        for g0 in range(0, d, V7X_LANES):
            lanes = slice(g0, g0 + V7X_LANES)
            gain = gate_ref[0, :, lanes] * pg_ref[:, lanes]
            o_ref[rows, lanes] = x_ref[rows, lanes] + o_ref[rows, lanes] * rs * gain
        return carry

    lax.fori_loop(0, tm // row_chunk, body, 0)


def _out_proj(ret_b, conv_b, w_out_b, x2d, gate3, post_g, *, tm, tiles_per_batch):
    m, d = x2d.shape
    heads, _, dh = ret_b.shape
    r = heads * dh
    cw = conv_b.shape[1]
    assert r == cw and w_out_b.shape == (r + cw, d)
    kern = functools.partial(_outproj_kernel, tn=512, row_chunk=16)
    return pl.pallas_call(
        kern,
        grid=(m // tm,),
        in_specs=[
            pl.BlockSpec((heads, tm, dh), lambda i: (0, i, 0)),
            pl.BlockSpec((tm, cw), lambda i: (i, 0)),
            pl.BlockSpec((r, d), lambda i: (0, 0), pipeline_mode=pl.Buffered(1)),
            pl.BlockSpec((cw, d), lambda i: (1, 0), pipeline_mode=pl.Buffered(1)),
            pl.BlockSpec((tm, d), lambda i: (i, 0)),
            pl.BlockSpec((1, 1, d), lambda i: (i // tiles_per_batch, 0, 0)),
            pl.BlockSpec((1, d), lambda i: (0, 0)),
        ],
        out_specs=pl.BlockSpec((tm, d), lambda i: (i, 0)),
        out_shape=jax.ShapeDtypeStruct((m, d), F32),
        scratch_shapes=[pltpu.VMEM((tm, V7X_LANES), F32)],
        compiler_params=_cparams(("arbitrary",)),
        name="out_proj",
    )(ret_b, conv_b, w_out_b, w_out_b, x2d, gate3, post_g)


def kernel(x, c, ctx, c_ctx, ada_w, ada_b, pre_norm_g, post_norm_g, w_in, ret_log_decay_fwd,
           ret_log_decay_bwd, ret_gn_g, conv_dw_w, conv_dw_b, conv_ln_g, conv_ln_b, w_out):
    batch, n, d = x.shape
    n_ctx = ctx.shape[1]
    depth = ada_w.shape[0]
    r = ret_gn_g.shape[1]
    cw = conv_dw_w.shape[2]
    heads = RET_HEADS
    dh = r // heads
    assert depth == 1, "single-layer block: the context stream is never updated"
    assert w_in.shape[2] == 4 * r + 3 * cw and r == cw

    cos, sin = _rope_tables(n, dh)
    x2d = x.reshape(batch * n, d)
    ctx2d = ctx.reshape(batch * n_ctx, d)

    mod_rows = 8
    cc = jnp.concatenate([c, c_ctx[None, :], jnp.zeros((mod_rows - batch - 1, d), F32)], axis=0)
    mod2 = _ada_mod(cc, ada_w[0], ada_b, 2 * d).reshape(mod_rows, 2, d)

    ret_cols = 4 * r
    w_ret_b = w_in[0, :, :ret_cols].astype(BF16)

    tm_norm = 512
    h = _prenorm(x2d, pre_norm_g, mod2, tm=tm_norm, mod_row_fn=lambda i: i // (n // tm_norm), name="prenorm")
    h_ctx = _prenorm(ctx2d, pre_norm_g, mod2, tm=tm_norm, mod_row_fn=lambda i: batch, name="prenorm_ctx")

    tm_in, tn_in = 1024, 1024
    side_casts = [(w_in[0], ret_cols // cw + k, cw) for k in range(3)] + [(w_out[0], 0, d)]
    proj, w_glu_a, w_glu_b, w_gate, w_out_b = _in_proj(
        h, w_ret_b, tm=tm_in, tn=tn_in, col_block0=0, n_col_blocks=ret_cols // tn_in, name="in_proj",
        rope=(cos, sin, n, r), side_casts=side_casts)
    kv_c = _in_proj(h_ctx, w_ret_b, tm=batch * n_ctx, tn=tn_in, col_block0=r // tn_in,
                    n_col_blocks=2 * r // tn_in, name="in_proj_ctx")
    glu_a, ada_gate = _in_proj(h, w_glu_a, tm=tm_in, tn=tn_in, col_block0=0, n_col_blocks=cw // tn_in,
                               name="in_proj_glu_a", ada_gate=(cc, ada_w[0], ada_b, 2 * d, d))
    glu_b, gate_c = [
        _in_proj(h, w_k, tm=tm_in, tn=tn_in, col_block0=0, n_col_blocks=cw // tn_in, name=name)
        for w_k, name in ((w_glu_b, "in_proj_glu_b"), (w_gate, "in_proj_conv_gate"))]

    lg = jnp.concatenate([ret_log_decay_fwd, ret_log_decay_bwd], axis=0)
    ret_b = _retention(lg, proj, kv_c, ret_gn_g, batch=batch, n=n, n_ctx=n_ctx, heads=heads, dh=dh)
    conv_b = _conv_branch(glu_a, glu_b, gate_c, conv_dw_w[0], conv_dw_b, conv_ln_g, conv_ln_b, n=n)

    tm_out = 256
    out = _out_proj(ret_b, conv_b, w_out_b, x2d, ada_gate.reshape(mod_rows, 1, d), post_norm_g,
                    tm=tm_out, tiles_per_batch=n // tm_out)
    return out.reshape(batch, n, d)
```

```python
import functools

import numpy as np
import jax
import jax.numpy as jnp
from jax import lax
from jax.experimental import pallas as pl
from jax.experimental.pallas import tpu as pltpu

F32 = jnp.float32
BF16 = jnp.bfloat16

GRID_W = 64
RET_HEADS = 8
ROPE_BASE = 10000.0
NORM_EPS = 1e-6

V7X_LANES = 128
V7X_SUBLANES_BF16 = 16
V7X_VMEM_LIMIT_BYTES = 58 * 1024 * 1024

RET_CHUNK = 256
CONV_HALO = 16


def _cparams(semantics):
    return pltpu.CompilerParams(dimension_semantics=semantics, vmem_limit_bytes=V7X_VMEM_LIMIT_BYTES)


def _silu(x):
    h = 0.5 * x
    return h + h * jnp.tanh(h)


def _mul_sigmoid(a, b):
    ha = 0.5 * a
    return ha + ha * jnp.tanh(0.5 * b)


def _ada_columns(s_ref, w_ref, b_ref):
    s = s_ref[...]
    s = s * jax.nn.sigmoid(s)
    acc = jnp.dot(s.astype(BF16), w_ref[...].astype(BF16), preferred_element_type=F32)
    return acc + b_ref[...]


def _ada_kernel(s_ref, w_ref, b_ref, o_ref):
    o_ref[...] = _ada_columns(s_ref, w_ref, b_ref)


def _ada_mod(cc, ada_w, ada_b, n_cols, tn=512):
    rows, d = cc.shape
    return pl.pallas_call(
        _ada_kernel,
        grid=(n_cols // tn,),
        in_specs=[
            pl.BlockSpec((rows, d), lambda j: (0, 0)),
            pl.BlockSpec((d, tn), lambda j: (0, j)),
            pl.BlockSpec((1, tn), lambda j: (0, j)),
        ],
        out_specs=pl.BlockSpec((rows, tn), lambda j: (0, j)),
        out_shape=jax.ShapeDtypeStruct((rows, n_cols), F32),
        compiler_params=_cparams(("arbitrary",)),
        name="ada_mod",
    )(cc, ada_w, ada_b)


def _prenorm_kernel(x_ref, g_ref, mod_ref, h_ref, *, row_chunk):
    gain = g_ref[...] * (1.0 + mod_ref[0, 1:2, :])
    shift = mod_ref[0, 0:1, :]

    def body(r, carry):
        rows = pl.ds(pl.multiple_of(r * row_chunk, row_chunk), row_chunk)
        xc = x_ref[rows, :]
        var = jnp.mean(xc * xc, axis=-1, keepdims=True)
        h_ref[rows, :] = (xc * lax.rsqrt(var + NORM_EPS) * gain + shift).astype(h_ref.dtype)
        return carry

    lax.fori_loop(0, x_ref.shape[0] // row_chunk, body, 0, unroll=4)


def _prenorm(x2d, pre_g, mod3, *, tm, mod_row_fn, name):
    m, d = x2d.shape
    return pl.pallas_call(
        functools.partial(_prenorm_kernel, row_chunk=16),
        grid=(m // tm,),
        in_specs=[
            pl.BlockSpec((tm, d), lambda i: (i, 0)),
            pl.BlockSpec((1, d), lambda i: (0, 0)),
            pl.BlockSpec((1, 2, d), lambda i: (mod_row_fn(i), 0, 0)),
        ],
        out_specs=pl.BlockSpec((tm, d), lambda i: (i, 0)),
        out_shape=jax.ShapeDtypeStruct((m, d), BF16),
        compiler_params=_cparams(("arbitrary",)),
        name=name,
    )(x2d, pre_g, mod3)


def _inproj_kernel(h_ref, w_ref, o_ref):
    o_ref[...] = jnp.dot(h_ref[...], w_ref[...], preferred_element_type=F32).astype(o_ref.dtype)


def _inproj_gate_kernel(h_ref, w_ref, s_ref, aw_ref, ab_ref, o_ref, g_ref):
    o_ref[...] = jnp.dot(h_ref[...], w_ref[...], preferred_element_type=F32).astype(o_ref.dtype)
    g_ref[...] = _ada_columns(s_ref, aw_ref, ab_ref)


def _inproj_rope_kernel(h_ref, w_ref, cos_ref, sin_ref, *refs, q_tiles, k_tiles, head_dim):
    n_cast = len(refs) // 2
    o_ref = refs[n_cast]
    for src_ref, dst_ref in zip(refs[:n_cast], refs[n_cast + 1:]):
        dst_ref[...] = src_ref[...].astype(dst_ref.dtype)
    j = pl.program_id(1)
    is_rope = j < q_tiles + k_tiles
    scale = jnp.where(j >= q_tiles, head_dim ** -0.5, 1.0).astype(F32)
    half = V7X_LANES // 2
    groups_per_head = head_dim // V7X_LANES
    acc = jnp.dot(h_ref[...], w_ref[...], preferred_element_type=F32)
    for g in range(acc.shape[1] // V7X_LANES):
        lanes = slice(g * V7X_LANES, (g + 1) * V7X_LANES)
        tl = slice((g % groups_per_head) * V7X_LANES, (g % groups_per_head + 1) * V7X_LANES)
        t = acc[:, lanes]
        roped = (t * cos_ref[:, tl] + pltpu.roll(t, half, axis=1) * sin_ref[:, tl]) * scale
        o_ref[g // groups_per_head, :, tl] = jnp.where(is_rope, roped, t).astype(o_ref.dtype)


def _in_proj(h, w_bf16, *, tm, tn, col_block0, n_col_blocks, name, rope=None, side_casts=(), ada_gate=None):
    m, d = h.shape
    n_steps = (m // tm) * n_col_blocks
    in_specs = [
        pl.BlockSpec((tm, d), lambda i, j: (i, 0)),
        pl.BlockSpec((d, tn), lambda i, j: (0, j + col_block0)),
    ]
    args = [h, w_bf16]
    out_specs = [pl.BlockSpec((tm, tn), lambda i, j: (i, j))]
    out_shape = [jax.ShapeDtypeStruct((m, n_col_blocks * tn), BF16)]
    if rope is None and ada_gate is not None:
        assert not side_casts
        cc, ada_w, ada_b, col0, n_cols = ada_gate
        gw = n_cols // n_steps
        assert gw * n_steps == n_cols and gw % V7X_LANES == 0 and col0 % gw == 0
        step = lambda i, j: i * n_col_blocks + j
        in_specs += [
            pl.BlockSpec(cc.shape, lambda i, j: (0, 0)),
            pl.BlockSpec((ada_w.shape[0], gw), lambda i, j: (0, col0 // gw + step(i, j))),
            pl.BlockSpec((1, gw), lambda i, j: (0, col0 // gw + step(i, j))),
        ]
        args += [cc, ada_w, ada_b]
        out_specs.append(pl.BlockSpec((cc.shape[0], gw), lambda i, j: (0, step(i, j))))
        out_shape.append(jax.ShapeDtypeStruct((cc.shape[0], n_cols), F32))
        kern = _inproj_gate_kernel
    elif rope is None:
        assert not side_casts
        kern = _inproj_kernel
    else:
        assert ada_gate is None
        cos, sin, n_tokens, rope_cols = rope
        head_dim = cos.shape[1]
        tiles_per_seq = n_tokens // tm
        table = pl.BlockSpec((tm, head_dim), lambda i, j: (i % tiles_per_seq, 0))
        in_specs += [table, table]
        args += [cos, sin]
        heads_per_tile = tn // head_dim
        out_specs = [pl.BlockSpec((heads_per_tile, tm, head_dim), lambda i, j: (j, i, 0))]
        out_shape = [jax.ShapeDtypeStruct((n_col_blocks * heads_per_tile, m, head_dim), BF16)]
        for src, col_block, width in side_casts:
            rows = src.shape[0]
            assert rows % n_steps == 0 and (rows // n_steps) % V7X_SUBLANES_BF16 == 0
            slab = rows // n_steps
            in_specs.append(pl.BlockSpec((slab, width), lambda i, j, cb=col_block: (i * n_col_blocks + j, cb)))
            out_specs.append(pl.BlockSpec((slab, width), lambda i, j: (i * n_col_blocks + j, 0)))
            out_shape.append(jax.ShapeDtypeStruct((rows, width), BF16))
            args.append(src)
        kern = functools.partial(_inproj_rope_kernel, q_tiles=rope_cols // tn, k_tiles=rope_cols // tn,
                                 head_dim=head_dim)
    outs = pl.pallas_call(
        kern,
        grid=(m // tm, n_col_blocks),
        in_specs=in_specs,
        out_specs=out_specs,
        out_shape=out_shape,
        compiler_params=_cparams(("arbitrary", "arbitrary")),
        name=name,
    )(*args)
    return outs[0] if len(outs) == 1 else outs


def _rope_tables(n_tokens, head_dim):
    rows = n_tokens // GRID_W
    pos_r = np.repeat(np.arange(rows, dtype=np.float64), GRID_W)
    pos_c = np.tile(np.arange(GRID_W, dtype=np.float64), rows)
    n_freq = head_dim // 4
    inv_freq = ROPE_BASE ** (-np.arange(n_freq, dtype=np.float64) / n_freq)
    ang_r = pos_r[:, None] * inv_freq[None, :]
    ang_c = pos_c[:, None] * inv_freq[None, :]
    cos = np.concatenate([np.cos(ang_r), np.cos(ang_r), np.cos(ang_c), np.cos(ang_c)], axis=-1)
    sin = np.concatenate([-np.sin(ang_r), np.sin(ang_r), -np.sin(ang_c), np.sin(ang_c)], axis=-1)
    return jnp.asarray(cos, F32), jnp.asarray(sin, F32)


def _dot_tn(a, b):
    return lax.dot_general(a, b, (((0,), (0,)), ((), ())), preferred_element_type=F32)


def _dot_nt(a, b):
    return lax.dot_general(a, b, (((1,), (1,)), ((), ())), preferred_element_type=F32)


def _retention_kernel(lg_ref, q_ref, k_ref, v_ref, gt_ref, kc_ref, vc_ref, gng_ref,
                      o_ref, c0_ref, r_ref, st_ref, dm_ref, wt_ref, raw_ref, *, chunk, unroll):
    n, dh = q_ref.shape
    nc = n // chunk
    n_ctx = kc_ref.shape[0]
    head = pl.program_id(1)
    lgf = lg_ref[0, head]
    lgb = lg_ref[1, head]

    col = lax.broadcasted_iota(jnp.int32, (chunk, dh), 0).astype(F32)
    rel = (lax.broadcasted_iota(jnp.int32, (chunk, chunk), 0)
           - lax.broadcasted_iota(jnp.int32, (chunk, chunk), 1)).astype(F32)
    dm_ref[...] = (jnp.where(rel >= 0, jnp.exp(lgf * jnp.maximum(rel, 0.0)), 0.0)
                   + jnp.where(rel <= 0, jnp.exp(lgb * jnp.maximum(-rel, 0.0)), 0.0))
    wt_ref[0] = jnp.exp(lgf * (col + 1.0))
    wt_ref[1] = jnp.exp(lgb * (chunk - col))
    wt_ref[2] = jnp.exp(lgf * (chunk - 1.0 - col))
    wt_ref[3] = jnp.exp(lgb * col)
    dec = jnp.concatenate([jnp.exp(jnp.full((1, dh), lgf * chunk, F32)),
                           jnp.exp(jnp.full((1, dh), lgb * chunk, F32))], axis=1)

    def weighted_v(v, w_fwd, w_bwd):
        vf = v.astype(F32)
        return jnp.concatenate([vf * w_fwd, vf * w_bwd], axis=1).astype(BF16)

    pos_c = lax.broadcasted_iota(jnp.int32, (n_ctx, dh), 0).astype(F32)
    kc = (kc_ref[...].astype(F32) * dh ** -0.5).astype(BF16)
    c0_ref[...] = _dot_tn(kc, weighted_v(vc_ref[...], jnp.exp(lgf * (n_ctx - 1.0 - pos_c)), jnp.exp(lgb * pos_c)))
    st_ref[...] = c0_ref[...]

    def scan_body(t, carry):
        cf = t
        cb = nc - 1 - t
        rows_f = pl.ds(pl.multiple_of(cf * chunk, chunk), chunk)
        rows_b = pl.ds(pl.multiple_of(cb * chunk, chunk), chunk)
        state = st_ref[...]
        r_ref[cf, :, 0:dh] = state[:, 0:dh].astype(BF16)
        r_ref[cb, :, dh:2 * dh] = state[:, dh:2 * dh].astype(BF16)
        kv_f = _dot_tn(k_ref[rows_f, :], (v_ref[rows_f, :].astype(F32) * wt_ref[2]).astype(BF16))
        kv_b = _dot_tn(k_ref[rows_b, :], (v_ref[rows_b, :].astype(F32) * wt_ref[3]).astype(BF16))
        st_ref[...] = state * dec + jnp.concatenate([kv_f, kv_b], axis=1)
        return carry

    lax.fori_loop(0, nc, scan_body, 0, unroll=unroll)

    gng = gng_ref[...]

    def raw_out(c):
        rows = pl.ds(pl.multiple_of(c * chunk, chunk), chunk)
        q = q_ref[rows, :]
        scores = _dot_nt(q, k_ref[rows, :]) * dm_ref[...]
        o = jnp.dot(scores.astype(BF16), v_ref[rows, :], preferred_element_type=F32)
        x = jnp.dot(q, r_ref[c], preferred_element_type=F32)
        return o + x[:, 0:dh] * wt_ref[0] + x[:, dh:2 * dh] * wt_ref[1]

    def finish(c, o):
        rows = pl.ds(pl.multiple_of(c * chunk, chunk), chunk)
        mu = jnp.mean(o, axis=-1, keepdims=True)
        oc = o - mu
        var = jnp.mean(oc * oc, axis=-1, keepdims=True)
        y = oc * lax.rsqrt(var + NORM_EPS) * gng
        o_ref[rows, :] = (y * _silu(gt_ref[rows, :].astype(F32))).astype(o_ref.dtype)

    raw_ref[0] = raw_out(0)

    def out_body(c, carry):
        prev = raw_ref[(c - 1) % 2]
        raw_ref[c % 2] = raw_out(c)
        finish(c - 1, prev)
        return carry

    lax.fori_loop(1, nc, out_body, 0, unroll=unroll)
    finish(nc - 1, raw_ref[(nc - 1) % 2])


def _retention(lg, proj, kv_c, gn_g, *, batch, n, n_ctx, heads, dh):
    r = heads * dh
    nc = n // RET_CHUNK
    kern = functools.partial(_retention_kernel, chunk=RET_CHUNK, unroll=2)
    return pl.pallas_call(
        kern,
        grid=(batch, heads),
        in_specs=[
            pl.BlockSpec(memory_space=pltpu.SMEM),
            pl.BlockSpec((None, n, dh), lambda b, h: (h, b, 0)),
            pl.BlockSpec((None, n, dh), lambda b, h: (heads + h, b, 0)),
            pl.BlockSpec((None, n, dh), lambda b, h: (2 * heads + h, b, 0)),
            pl.BlockSpec((None, n, dh), lambda b, h: (3 * heads + h, b, 0)),
            pl.BlockSpec((n_ctx, dh), lambda b, h: (b, h)),
            pl.BlockSpec((n_ctx, dh), lambda b, h: (b, heads + h)),
            pl.BlockSpec((1, dh), lambda b, h: (0, h)),
        ],
        out_specs=pl.BlockSpec((None, n, dh), lambda b, h: (h, b, 0)),
        out_shape=jax.ShapeDtypeStruct((heads, batch * n, dh), BF16),
        scratch_shapes=[
            pltpu.VMEM((dh, 2 * dh), F32),
            pltpu.VMEM((nc, dh, 2 * dh), BF16),
            pltpu.VMEM((dh, 2 * dh), F32),
            pltpu.VMEM((RET_CHUNK, RET_CHUNK), F32),
            pltpu.VMEM((4, RET_CHUNK, dh), F32),
            pltpu.VMEM((2, RET_CHUNK, dh), F32),
        ],
        compiler_params=_cparams(("arbitrary", "arbitrary")),
        name="retention",
    )(lg, proj, proj, proj, proj, kv_c, kv_c, gn_g)


def _conv_kernel(a_ref, b_ref, ap_ref, bp_ref, an_ref, bn_ref, gc_ref, w_ref, cb_ref, lg_ref, lb_ref,
                 o_ref, u_ref, y_ref, w3_ref, s1_ref, mu_ref, rs_ref, *, tiles_per_seq, row_chunk, conv_rows):
    tn, cw = a_ref.shape
    halo = ap_ref.shape[0]
    taps = w_ref.shape[0]
    pad = taps // 2
    n_groups = cw // V7X_LANES
    il = pl.program_id(0) % tiles_per_seq

    def glu(a, b):
        return _mul_sigmoid(a.astype(F32), b.astype(F32))

    def lane_group(val, g):
        return val[:, g * V7X_LANES:(g + 1) * V7X_LANES]

    u_prev = jnp.where(il > 0, glu(ap_ref[...], bp_ref[...]), 0.0)
    u_next = jnp.where(il < tiles_per_seq - 1, glu(an_ref[...], bn_ref[...]), 0.0)
    for g in range(n_groups):
        u_ref[g, 0:halo, :] = lane_group(u_prev, g)
        u_ref[g, halo + tn:2 * halo + tn, :] = lane_group(u_next, g)
        w3_ref[g, 0:taps, :] = lane_group(w_ref[...], g)
        w3_ref[g, taps:taps + 1, :] = lane_group(cb_ref[...], g)

    def glu_body(r, carry):
        r0 = pl.multiple_of(r * row_chunk, row_chunk)
        val = glu(a_ref[pl.ds(r0, row_chunk), :], b_ref[pl.ds(r0, row_chunk), :])
        for g in range(n_groups):
            u_ref[g, pl.ds(halo + r0, row_chunk), :] = lane_group(val, g)
        return carry

    lax.fori_loop(0, tn // row_chunk, glu_body, 0)

    s1_ref[...] = jnp.zeros_like(s1_ref)

    def conv_body(g, carry):
        for r0 in range(0, tn, conv_rows):
            acc = jnp.broadcast_to(w3_ref[g, taps:taps + 1, :], (conv_rows, V7X_LANES))
            for t in range(taps):
                s0 = r0 + halo - pad + t
                acc = acc + u_ref[g, s0:s0 + conv_rows, :] * w3_ref[g, t:t + 1, :]
            y_ref[g, r0:r0 + conv_rows, :] = acc
            s1_ref[r0:r0 + conv_rows, :] += acc
        return carry

    lax.fori_loop(0, n_groups, conv_body, 0)

    inv_cw = 1.0 / cw
    for r0 in range(0, tn, conv_rows):
        rows = slice(r0, r0 + conv_rows)
        mu = jnp.broadcast_to(jnp.sum(s1_ref[rows, :], axis=-1, keepdims=True) * inv_cw, (conv_rows, V7X_LANES))
        sq = jnp.zeros((conv_rows, V7X_LANES), F32)
        for g in range(n_groups):
            dlt = y_ref[g, rows, :] - mu
            sq = sq + dlt * dlt
        var = jnp.sum(sq, axis=-1, keepdims=True) * inv_cw
        mu_ref[rows, :] = mu
        rs_ref[rows, :] = jnp.broadcast_to(lax.rsqrt(var + NORM_EPS), (conv_rows, V7X_LANES))

    def row_body(r, carry):
        r0 = pl.multiple_of(r * row_chunk, row_chunk)
        rows = pl.ds(r0, row_chunk)
        mu = mu_ref[rows, :]
        rs = rs_ref[rows, :]
        for g in range(n_groups):
            lanes = slice(g * V7X_LANES, (g + 1) * V7X_LANES)
            z = (y_ref[g, rows, :] - mu) * rs * lg_ref[:, lanes] + lb_ref[:, lanes]
            gate = gc_ref[rows, lanes].astype(F32)
            o_ref[rows, lanes] = (_silu(z) * _silu(gate)).astype(o_ref.dtype)
        return carry

    lax.fori_loop(0, tn // row_chunk, row_body, 0)


def _conv_branch(glu_a, glu_b, gate, dw_w, dw_b, ln_g, ln_b, *, n, tn=512):
    m, cw = glu_a.shape
    halo = CONV_HALO
    assert n % tn == 0 and tn % halo == 0 and halo >= dw_w.shape[0] // 2 and cw % V7X_LANES == 0
    tiles_per_seq = n // tn
    hb = tn // halo
    last_hb = m // halo - 1
    n_groups = cw // V7X_LANES
    kern = functools.partial(_conv_kernel, tiles_per_seq=tiles_per_seq, row_chunk=16, conv_rows=64)
    main = pl.BlockSpec((tn, cw), lambda i: (i, 0))
    prev = pl.BlockSpec((halo, cw), lambda i: (jnp.maximum(i * hb - 1, 0), 0))
    nxt = pl.BlockSpec((halo, cw), lambda i: (jnp.minimum((i + 1) * hb, last_hb), 0))
    vec = lambda rows: pl.BlockSpec((rows, cw), lambda i: (0, 0))
    return pl.pallas_call(
        kern,
        grid=(m // tn,),
        in_specs=[main, main, prev, prev, nxt, nxt, main,
                  vec(dw_w.shape[0]), vec(1), vec(1), vec(1)],
        out_specs=pl.BlockSpec((tn, cw), lambda i: (i, 0)),
        out_shape=jax.ShapeDtypeStruct((m, cw), BF16),
        scratch_shapes=[
            pltpu.VMEM((n_groups, tn + 2 * halo, V7X_LANES), F32),
            pltpu.VMEM((n_groups, tn, V7X_LANES), F32),
            pltpu.VMEM((n_groups, dw_w.shape[0] + 1, V7X_LANES), F32),
            pltpu.VMEM((tn, V7X_LANES), F32),
            pltpu.VMEM((tn, V7X_LANES), F32),
            pltpu.VMEM((tn, V7X_LANES), F32),
        ],
        compiler_params=_cparams(("arbitrary",)),
        name="conv_branch",
    )(glu_a, glu_b, glu_a, glu_b, glu_a, glu_b, gate, dw_w, dw_b, ln_g, ln_b)


def _outproj_kernel(r_ref, c_ref, w1_ref, w2_ref, x_ref, gate_ref, pg_ref, o_ref, rs_ref, *, tn, row_chunk):
    tm, d = o_ref.shape
    ssq = jnp.zeros((tm, V7X_LANES), F32)
    ret = jnp.concatenate([r_ref[hd] for hd in range(r_ref.shape[0])], axis=1)
    for j0 in range(0, d, tn):
        cols = slice(j0, j0 + tn)
        y = jnp.dot(ret, w1_ref[:, cols], preferred_element_type=F32)
        y = y + jnp.dot(c_ref[...], w2_ref[:, cols], preferred_element_type=F32)
        o_ref[:, cols] = y
        for g0 in range(0, tn, V7X_LANES):
            yg = y[:, g0:g0 + V7X_LANES]
            ssq = ssq + yg * yg
    var = jnp.sum(ssq, axis=-1, keepdims=True) * (1.0 / d)
    rs_ref[...] = jnp.broadcast_to(lax.rsqrt(var + NORM_EPS), (tm, V7X_LANES))

    def body(r, carry):
        rows = pl.ds(pl.multiple_of(r * row_chunk, row_chunk), row_chunk)
        rs = rs_ref[rows, :]
        for g0 in range(0, d, V7X_LANES):
            lanes = slice(g0, g0 + V7X_LANES)
            gain = gate_ref[0, :, lanes] * pg_ref[:, lanes]
            o_ref[rows, lanes] = x_ref[rows, lanes] + o_ref[rows, lanes] * rs * gain
        return carry

    lax.fori_loop(0, tm // row_chunk, body, 0)


def _out_proj(ret_b, conv_b, w_out_b, x2d, gate3, post_g, *, tm, tiles_per_batch):
    m, d = x2d.shape
    heads, _, dh = ret_b.shape
    r = heads * dh
    cw = conv_b.shape[1]
    assert r == cw and w_out_b.shape == (r + cw, d)
    kern = functools.partial(_outproj_kernel, tn=512, row_chunk=16)
    return pl.pallas_call(
        kern,
        grid=(m // tm,),
        in_specs=[
            pl.BlockSpec((heads, tm, dh), lambda i: (0, i, 0)),
            pl.BlockSpec((tm, cw), lambda i: (i, 0)),
            pl.BlockSpec((r, d), lambda i: (0, 0), pipeline_mode=pl.Buffered(1)),
            pl.BlockSpec((cw, d), lambda i: (1, 0), pipeline_mode=pl.Buffered(1)),
            pl.BlockSpec((tm, d), lambda i: (i, 0)),
            pl.BlockSpec((1, 1, d), lambda i: (i // tiles_per_batch, 0, 0)),
            pl.BlockSpec((1, d), lambda i: (0, 0)),
        ],
        out_specs=pl.BlockSpec((tm, d), lambda i: (i, 0)),
        out_shape=jax.ShapeDtypeStruct((m, d), F32),
        scratch_shapes=[pltpu.VMEM((tm, V7X_LANES), F32)],
        compiler_params=_cparams(("arbitrary",)),
        name="out_proj",
    )(ret_b, conv_b, w_out_b, w_out_b, x2d, gate3, post_g)


def kernel(x, c, ctx, c_ctx, ada_w, ada_b, pre_norm_g, post_norm_g, w_in, ret_log_decay_fwd,
           ret_log_decay_bwd, ret_gn_g, conv_dw_w, conv_dw_b, conv_ln_g, conv_ln_b, w_out):
    batch, n, d = x.shape
    n_ctx = ctx.shape[1]
    depth = ada_w.shape[0]
    r = ret_gn_g.shape[1]
    cw = conv_dw_w.shape[2]
    heads = RET_HEADS
    dh = r // heads
    assert depth == 1, "single-layer block: the context stream is never updated"
    assert w_in.shape[2] == 4 * r + 3 * cw and r == cw

    cos, sin = _rope_tables(n, dh)
    x2d = x.reshape(batch * n, d)
    ctx2d = ctx.reshape(batch * n_ctx, d)

    mod_rows = 8
    cc = jnp.concatenate([c, c_ctx[None, :], jnp.zeros((mod_rows - batch - 1, d), F32)], axis=0)
    mod2 = _ada_mod(cc, ada_w[0], ada_b, 2 * d).reshape(mod_rows, 2, d)

    ret_cols = 4 * r
    w_ret_b = w_in[0, :, :ret_cols].astype(BF16)

    tm_norm = 512
    h = _prenorm(x2d, pre_norm_g, mod2, tm=tm_norm, mod_row_fn=lambda i: i // (n // tm_norm), name="prenorm")
    h_ctx = _prenorm(ctx2d, pre_norm_g, mod2, tm=tm_norm, mod_row_fn=lambda i: batch, name="prenorm_ctx")

    tm_in, tn_in = 1024, 1024
    side_casts = [(w_in[0], ret_cols // cw + k, cw) for k in range(3)] + [(w_out[0], 0, d)]
    proj, w_glu_a, w_glu_b, w_gate, w_out_b = _in_proj(
        h, w_ret_b, tm=tm_in, tn=tn_in, col_block0=0, n_col_blocks=ret_cols // tn_in, name="in_proj",
        rope=(cos, sin, n, r), side_casts=side_casts)
    kv_c = _in_proj(h_ctx, w_ret_b, tm=batch * n_ctx, tn=tn_in, col_block0=r // tn_in,
                    n_col_blocks=2 * r // tn_in, name="in_proj_ctx")
    glu_a, ada_gate = _in_proj(h, w_glu_a, tm=tm_in, tn=tn_in, col_block0=0, n_col_blocks=cw // tn_in,
                               name="in_proj_glu_a", ada_gate=(cc, ada_w[0], ada_b, 2 * d, d))
    glu_b, gate_c = [
        _in_proj(h, w_k, tm=tm_in, tn=tn_in, col_block0=0, n_col_blocks=cw // tn_in, name=name)
        for w_k, name in ((w_glu_b, "in_proj_glu_b"), (w_gate, "in_proj_conv_gate"))]

    lg = jnp.concatenate([ret_log_decay_fwd, ret_log_decay_bwd], axis=0)
    ret_b = _retention(lg, proj, kv_c, ret_gn_g, batch=batch, n=n, n_ctx=n_ctx, heads=heads, dh=dh)
    conv_b = _conv_branch(glu_a, glu_b, gate_c, conv_dw_w[0], conv_dw_b, conv_ln_g, conv_ln_b, n=n)

    tm_out = 256
    out = _out_proj(ret_b, conv_b, w_out_b, x2d, ada_gate.reshape(mod_rows, 1, d), post_norm_g,
                    tm=tm_out, tiles_per_batch=n // tm_out)
    return out.reshape(batch, n, d)
```

```python
import functools

import numpy as np
import jax
import jax.numpy as jnp
from jax import lax
from jax.experimental import pallas as pl
from jax.experimental.pallas import tpu as pltpu

F32 = jnp.float32
BF16 = jnp.bfloat16

GRID_W = 64
RET_HEADS = 8
ROPE_BASE = 10000.0
NORM_EPS = 1e-6

V7X_LANES = 128
V7X_SUBLANES_BF16 = 16
V7X_VMEM_LIMIT_BYTES = 58 * 1024 * 1024

RET_CHUNK = 256
CONV_HALO = 16


def _cparams(semantics):
    return pltpu.CompilerParams(dimension_semantics=semantics, vmem_limit_bytes=V7X_VMEM_LIMIT_BYTES)


def _silu(x):
    h = 0.5 * x
    return h + h * jnp.tanh(h)


def _mul_sigmoid(a, b):
    ha = 0.5 * a
    return ha + ha * jnp.tanh(0.5 * b)


def _ada_columns(s_ref, w_ref, b_ref):
    s = s_ref[...]
    s = s * jax.nn.sigmoid(s)
    acc = jnp.dot(s.astype(BF16), w_ref[...].astype(BF16), preferred_element_type=F32)
    return acc + b_ref[...]


def _ada_kernel(s_ref, w_ref, b_ref, o_ref):
    o_ref[...] = _ada_columns(s_ref, w_ref, b_ref)


def _ada_mod(cc, ada_w, ada_b, n_cols, tn=512):
    rows, d = cc.shape
    return pl.pallas_call(
        _ada_kernel,
        grid=(n_cols // tn,),
        in_specs=[
            pl.BlockSpec((rows, d), lambda j: (0, 0)),
            pl.BlockSpec((d, tn), lambda j: (0, j)),
            pl.BlockSpec((1, tn), lambda j: (0, j)),
        ],
        out_specs=pl.BlockSpec((rows, tn), lambda j: (0, j)),
        out_shape=jax.ShapeDtypeStruct((rows, n_cols), F32),
        compiler_params=_cparams(("arbitrary",)),
        name="ada_mod",
    )(cc, ada_w, ada_b)


def _prenorm_kernel(x_ref, g_ref, mod_ref, h_ref, *, row_chunk):
    gain = g_ref[...] * (1.0 + mod_ref[0, 1:2, :])
    shift = mod_ref[0, 0:1, :]

    def body(r, carry):
        rows = pl.ds(pl.multiple_of(r * row_chunk, row_chunk), row_chunk)
        xc = x_ref[rows, :]
        var = jnp.mean(xc * xc, axis=-1, keepdims=True)
        h_ref[rows, :] = (xc * lax.rsqrt(var + NORM_EPS) * gain + shift).astype(h_ref.dtype)
        return carry

    lax.fori_loop(0, x_ref.shape[0] // row_chunk, body, 0, unroll=4)


def _prenorm(x2d, pre_g, mod3, *, tm, mod_row_fn, name):
    m, d = x2d.shape
    return pl.pallas_call(
        functools.partial(_prenorm_kernel, row_chunk=16),
        grid=(m // tm,),
        in_specs=[
            pl.BlockSpec((tm, d), lambda i: (i, 0)),
            pl.BlockSpec((1, d), lambda i: (0, 0)),
            pl.BlockSpec((1, 2, d), lambda i: (mod_row_fn(i), 0, 0)),
        ],
        out_specs=pl.BlockSpec((tm, d), lambda i: (i, 0)),
        out_shape=jax.ShapeDtypeStruct((m, d), BF16),
        compiler_params=_cparams(("arbitrary",)),
        name=name,
    )(x2d, pre_g, mod3)


def _inproj_kernel(h_ref, w_ref, o_ref):
    o_ref[...] = jnp.dot(h_ref[...], w_ref[...], preferred_element_type=F32).astype(o_ref.dtype)


def _inproj_gate_kernel(h_ref, w_ref, s_ref, aw_ref, ab_ref, o_ref, g_ref):
    o_ref[...] = jnp.dot(h_ref[...], w_ref[...], preferred_element_type=F32).astype(o_ref.dtype)
    g_ref[...] = _ada_columns(s_ref, aw_ref, ab_ref)


def _inproj_rope_kernel(h_ref, w_ref, cos_ref, sin_ref, *refs, q_tiles, k_tiles, head_dim):
    n_cast = len(refs) // 2
    o_ref = refs[n_cast]
    for src_ref, dst_ref in zip(refs[:n_cast], refs[n_cast + 1:]):
        dst_ref[...] = src_ref[...].astype(dst_ref.dtype)
    j = pl.program_id(1)
    is_rope = j < q_tiles + k_tiles
    scale = jnp.where(j >= q_tiles, head_dim ** -0.5, 1.0).astype(F32)
    half = V7X_LANES // 2
    groups_per_head = head_dim // V7X_LANES
    acc = jnp.dot(h_ref[...], w_ref[...], preferred_element_type=F32)
    for g in range(acc.shape[1] // V7X_LANES):
        lanes = slice(g * V7X_LANES, (g + 1) * V7X_LANES)
        tl = slice((g % groups_per_head) * V7X_LANES, (g % groups_per_head + 1) * V7X_LANES)
        t = acc[:, lanes]
        roped = (t * cos_ref[:, tl] + pltpu.roll(t, half, axis=1) * sin_ref[:, tl]) * scale
        o_ref[g // groups_per_head, :, tl] = jnp.where(is_rope, roped, t).astype(o_ref.dtype)


def _in_proj(h, w_bf16, *, tm, tn, col_block0, n_col_blocks, name, rope=None, side_casts=(), ada_gate=None):
    m, d = h.shape
    n_steps = (m // tm) * n_col_blocks
    in_specs = [
        pl.BlockSpec((tm, d), lambda i, j: (i, 0)),
        pl.BlockSpec((d, tn), lambda i, j: (0, j + col_block0)),
    ]
    args = [h, w_bf16]
    out_specs = [pl.BlockSpec((tm, tn), lambda i, j: (i, j))]
    out_shape = [jax.ShapeDtypeStruct((m, n_col_blocks * tn), BF16)]
    if rope is None and ada_gate is not None:
        assert not side_casts
        cc, ada_w, ada_b, col0, n_cols = ada_gate
        gw = n_cols // n_steps
        assert gw * n_steps == n_cols and gw % V7X_LANES == 0 and col0 % gw == 0
        step = lambda i, j: i * n_col_blocks + j
        in_specs += [
            pl.BlockSpec(cc.shape, lambda i, j: (0, 0)),
            pl.BlockSpec((ada_w.shape[0], gw), lambda i, j: (0, col0 // gw + step(i, j))),
            pl.BlockSpec((1, gw), lambda i, j: (0, col0 // gw + step(i, j))),
        ]
        args += [cc, ada_w, ada_b]
        out_specs.append(pl.BlockSpec((cc.shape[0], gw), lambda i, j: (0, step(i, j))))
        out_shape.append(jax.ShapeDtypeStruct((cc.shape[0], n_cols), F32))
        kern = _inproj_gate_kernel
    elif rope is None:
        assert not side_casts
        kern = _inproj_kernel
    else:
        assert ada_gate is None
        cos, sin, n_tokens, rope_cols = rope
        head_dim = cos.shape[1]
        tiles_per_seq = n_tokens // tm
        table = pl.BlockSpec((tm, head_dim), lambda i, j: (i % tiles_per_seq, 0))
        in_specs += [table, table]
        args += [cos, sin]
        heads_per_tile = tn // head_dim
        out_specs = [pl.BlockSpec((heads_per_tile, tm, head_dim), lambda i, j: (j, i, 0))]
        out_shape = [jax.ShapeDtypeStruct((n_col_blocks * heads_per_tile, m, head_dim), BF16)]
        for src, col_block, width in side_casts:
            rows = src.shape[0]
            assert rows % n_steps == 0 and (rows // n_steps) % V7X_SUBLANES_BF16 == 0
            slab = rows // n_steps
            in_specs.append(pl.BlockSpec((slab, width), lambda i, j, cb=col_block: (i * n_col_blocks + j, cb)))
            out_specs.append(pl.BlockSpec((slab, width), lambda i, j: (i * n_col_blocks + j, 0)))
            out_shape.append(jax.ShapeDtypeStruct((rows, width), BF16))
            args.append(src)
        kern = functools.partial(_inproj_rope_kernel, q_tiles=rope_cols // tn, k_tiles=rope_cols // tn,
                                 head_dim=head_dim)
    outs = pl.pallas_call(
        kern,
        grid=(m // tm, n_col_blocks),
        in_specs=in_specs,
        out_specs=out_specs,
        out_shape=out_shape,
        compiler_params=_cparams(("arbitrary", "arbitrary")),
        name=name,
    )(*args)
    return outs[0] if len(outs) == 1 else outs


def _rope_tables(n_tokens, head_dim):
    rows = n_tokens // GRID_W
    pos_r = np.repeat(np.arange(rows, dtype=np.float64), GRID_W)
    pos_c = np.tile(np.arange(GRID_W, dtype=np.float64), rows)
    n_freq = head_dim // 4
    inv_freq = ROPE_BASE ** (-np.arange(n_freq, dtype=np.float64) / n_freq)
    ang_r = pos_r[:, None] * inv_freq[None, :]
    ang_c = pos_c[:, None] * inv_freq[None, :]
    cos = np.concatenate([np.cos(ang_r), np.cos(ang_r), np.cos(ang_c), np.cos(ang_c)], axis=-1)
    sin = np.concatenate([-np.sin(ang_r), np.sin(ang_r), -np.sin(ang_c), np.sin(ang_c)], axis=-1)
    return jnp.asarray(cos, F32), jnp.asarray(sin, F32)


def _dot_tn(a, b):
    return lax.dot_general(a, b, (((0,), (0,)), ((), ())), preferred_element_type=F32)


def _dot_nt(a, b):
    return lax.dot_general(a, b, (((1,), (1,)), ((), ())), preferred_element_type=F32)


def _retention_kernel(lg_ref, q_ref, k_ref, v_ref, gt_ref, kc_ref, vc_ref, gng_ref,
                      o_ref, c0_ref, r_ref, st_ref, dm_ref, wt_ref, raw_ref, *, chunk, unroll):
    n, dh = q_ref.shape
    nc = n // chunk
    n_ctx = kc_ref.shape[0]
    head = pl.program_id(1)
    lgf = lg_ref[0, head]
    lgb = lg_ref[1, head]

    col = lax.broadcasted_iota(jnp.int32, (chunk, dh), 0).astype(F32)
    rel = (lax.broadcasted_iota(jnp.int32, (chunk, chunk), 0)
           - lax.broadcasted_iota(jnp.int32, (chunk, chunk), 1)).astype(F32)
    dm_ref[...] = (jnp.where(rel >= 0, jnp.exp(lgf * jnp.maximum(rel, 0.0)), 0.0)
                   + jnp.where(rel <= 0, jnp.exp(lgb * jnp.maximum(-rel, 0.0)), 0.0))
    wt_ref[0] = jnp.exp(lgf * (col + 1.0))
    wt_ref[1] = jnp.exp(lgb * (chunk - col))
    wt_ref[2] = jnp.exp(lgf * (chunk - 1.0 - col))
    wt_ref[3] = jnp.exp(lgb * col)
    dec = jnp.concatenate([jnp.exp(jnp.full((1, dh), lgf * chunk, F32)),
                           jnp.exp(jnp.full((1, dh), lgb * chunk, F32))], axis=1)

    def weighted_v(v, w_fwd, w_bwd):
        vf = v.astype(F32)
        return jnp.concatenate([vf * w_fwd, vf * w_bwd], axis=1).astype(BF16)

    pos_c = lax.broadcasted_iota(jnp.int32, (n_ctx, dh), 0).astype(F32)
    kc = (kc_ref[...].astype(F32) * dh ** -0.5).astype(BF16)
    c0_ref[...] = _dot_tn(kc, weighted_v(vc_ref[...], jnp.exp(lgf * (n_ctx - 1.0 - pos_c)), jnp.exp(lgb * pos_c)))
    st_ref[...] = c0_ref[...]

    def scan_body(t, carry):
        cf = t
        cb = nc - 1 - t
        rows_f = pl.ds(pl.multiple_of(cf * chunk, chunk), chunk)
        rows_b = pl.ds(pl.multiple_of(cb * chunk, chunk), chunk)
        state = st_ref[...]
        r_ref[cf, :, 0:dh] = state[:, 0:dh].astype(BF16)
        r_ref[cb, :, dh:2 * dh] = state[:, dh:2 * dh].astype(BF16)
        kv_f = _dot_tn(k_ref[rows_f, :], (v_ref[rows_f, :].astype(F32) * wt_ref[2]).astype(BF16))
        kv_b = _dot_tn(k_ref[rows_b, :], (v_ref[rows_b, :].astype(F32) * wt_ref[3]).astype(BF16))
        st_ref[...] = state * dec + jnp.concatenate([kv_f, kv_b], axis=1)
        return carry

    lax.fori_loop(0, nc, scan_body, 0, unroll=unroll)

    gng = gng_ref[...]

    def raw_out(c):
        rows = pl.ds(pl.multiple_of(c * chunk, chunk), chunk)
        q = q_ref[rows, :]
        scores = _dot_nt(q, k_ref[rows, :]) * dm_ref[...]
        o = jnp.dot(scores.astype(BF16), v_ref[rows, :], preferred_element_type=F32)
        x = jnp.dot(q, r_ref[c], preferred_element_type=F32)
        return o + x[:, 0:dh] * wt_ref[0] + x[:, dh:2 * dh] * wt_ref[1]

    def finish(c, o):
        rows = pl.ds(pl.multiple_of(c * chunk, chunk), chunk)
        mu = jnp.mean(o, axis=-1, keepdims=True)
        oc = o - mu
        var = jnp.mean(oc * oc, axis=-1, keepdims=True)
        y = oc * lax.rsqrt(var + NORM_EPS) * gng
        o_ref[rows, :] = (y * _silu(gt_ref[rows, :].astype(F32))).astype(o_ref.dtype)

    raw_ref[0] = raw_out(0)

    def out_body(c, carry):
        prev = raw_ref[(c - 1) % 2]
        raw_ref[c % 2] = raw_out(c)
        finish(c - 1, prev)
        return carry

    lax.fori_loop(1, nc, out_body, 0, unroll=unroll)
    finish(nc - 1, raw_ref[(nc - 1) % 2])


def _retention(lg, proj, kv_c, gn_g, *, batch, n, n_ctx, heads, dh):
    r = heads * dh
    nc = n // RET_CHUNK
    kern = functools.partial(_retention_kernel, chunk=RET_CHUNK, unroll=1)
    return pl.pallas_call(
        kern,
        grid=(batch, heads),
        in_specs=[
            pl.BlockSpec(memory_space=pltpu.SMEM),
            pl.BlockSpec((None, n, dh), lambda b, h: (h, b, 0)),
            pl.BlockSpec((None, n, dh), lambda b, h: (heads + h, b, 0)),
            pl.BlockSpec((None, n, dh), lambda b, h: (2 * heads + h, b, 0)),
            pl.BlockSpec((None, n, dh), lambda b, h: (3 * heads + h, b, 0)),
            pl.BlockSpec((n_ctx, dh), lambda b, h: (b, h)),
            pl.BlockSpec((n_ctx, dh), lambda b, h: (b, heads + h)),
            pl.BlockSpec((1, dh), lambda b, h: (0, h)),
        ],
        out_specs=pl.BlockSpec((None, n, dh), lambda b, h: (h, b, 0)),
        out_shape=jax.ShapeDtypeStruct((heads, batch * n, dh), BF16),
        scratch_shapes=[
            pltpu.VMEM((dh, 2 * dh), F32),
            pltpu.VMEM((nc, dh, 2 * dh), BF16),
            pltpu.VMEM((dh, 2 * dh), F32),
            pltpu.VMEM((RET_CHUNK, RET_CHUNK), F32),
            pltpu.VMEM((4, RET_CHUNK, dh), F32),
            pltpu.VMEM((2, RET_CHUNK, dh), F32),
        ],
        compiler_params=_cparams(("arbitrary", "arbitrary")),
        name="retention",
    )(lg, proj, proj, proj, proj, kv_c, kv_c, gn_g)


def _conv_kernel(a_ref, b_ref, ap_ref, bp_ref, an_ref, bn_ref, gc_ref, w_ref, cb_ref, lg_ref, lb_ref,
                 o_ref, u_ref, y_ref, w3_ref, s1_ref, mu_ref, rs_ref, *, tiles_per_seq, row_chunk, conv_rows):
    tn, cw = a_ref.shape
    halo = ap_ref.shape[0]
    taps = w_ref.shape[0]
    pad = taps // 2
    n_groups = cw // V7X_LANES
    il = pl.program_id(0) % tiles_per_seq

    def glu(a, b):
        return _mul_sigmoid(a.astype(F32), b.astype(F32))

    def lane_group(val, g):
        return val[:, g * V7X_LANES:(g + 1) * V7X_LANES]

    u_prev = jnp.where(il > 0, glu(ap_ref[...], bp_ref[...]), 0.0)
    u_next = jnp.where(il < tiles_per_seq - 1, glu(an_ref[...], bn_ref[...]), 0.0)
    for g in range(n_groups):
        u_ref[g, 0:halo, :] = lane_group(u_prev, g)
        u_ref[g, halo + tn:2 * halo + tn, :] = lane_group(u_next, g)
        w3_ref[g, 0:taps, :] = lane_group(w_ref[...], g)
        w3_ref[g, taps:taps + 1, :] = lane_group(cb_ref[...], g)

    def glu_body(r, carry):
        r0 = pl.multiple_of(r * row_chunk, row_chunk)
        val = glu(a_ref[pl.ds(r0, row_chunk), :], b_ref[pl.ds(r0, row_chunk), :])
        for g in range(n_groups):
            u_ref[g, pl.ds(halo + r0, row_chunk), :] = lane_group(val, g)
        return carry

    lax.fori_loop(0, tn // row_chunk, glu_body, 0)

    s1_ref[...] = jnp.zeros_like(s1_ref)

    def conv_body(g, carry):
        for r0 in range(0, tn, conv_rows):
            acc = jnp.broadcast_to(w3_ref[g, taps:taps + 1, :], (conv_rows, V7X_LANES))
            for t in range(taps):
                s0 = r0 + halo - pad + t
                acc = acc + u_ref[g, s0:s0 + conv_rows, :] * w3_ref[g, t:t + 1, :]
            y_ref[g, r0:r0 + conv_rows, :] = acc
            s1_ref[r0:r0 + conv_rows, :] += acc
        return carry

    lax.fori_loop(0, n_groups, conv_body, 0)

    inv_cw = 1.0 / cw
    for r0 in range(0, tn, conv_rows):
        rows = slice(r0, r0 + conv_rows)
        mu = jnp.broadcast_to(jnp.sum(s1_ref[rows, :], axis=-1, keepdims=True) * inv_cw, (conv_rows, V7X_LANES))
        sq = jnp.zeros((conv_rows, V7X_LANES), F32)
        for g in range(n_groups):
            dlt = y_ref[g, rows, :] - mu
            sq = sq + dlt * dlt
        var = jnp.sum(sq, axis=-1, keepdims=True) * inv_cw
        mu_ref[rows, :] = mu
        rs_ref[rows, :] = jnp.broadcast_to(lax.rsqrt(var + NORM_EPS), (conv_rows, V7X_LANES))

    def row_body(r, carry):
        r0 = pl.multiple_of(r * row_chunk, row_chunk)
        rows = pl.ds(r0, row_chunk)
        mu = mu_ref[rows, :]
        rs = rs_ref[rows, :]
        for g in range(n_groups):
            lanes = slice(g * V7X_LANES, (g + 1) * V7X_LANES)
            z = (y_ref[g, rows, :] - mu) * rs * lg_ref[:, lanes] + lb_ref[:, lanes]
            gate = gc_ref[rows, lanes].astype(F32)
            o_ref[rows, lanes] = (_silu(z) * _silu(gate)).astype(o_ref.dtype)
        return carry

    lax.fori_loop(0, tn // row_chunk, row_body, 0)


def _conv_branch(glu_a, glu_b, gate, dw_w, dw_b, ln_g, ln_b, *, n, tn=512):
    m, cw = glu_a.shape
    halo = CONV_HALO
    assert n % tn == 0 and tn % halo == 0 and halo >= dw_w.shape[0] // 2 and cw % V7X_LANES == 0
    tiles_per_seq = n // tn
    hb = tn // halo
    last_hb = m // halo - 1
    n_groups = cw // V7X_LANES
    kern = functools.partial(_conv_kernel, tiles_per_seq=tiles_per_seq, row_chunk=16, conv_rows=64)
    main = pl.BlockSpec((tn, cw), lambda i: (i, 0))
    prev = pl.BlockSpec((halo, cw), lambda i: (jnp.maximum(i * hb - 1, 0), 0))
    nxt = pl.BlockSpec((halo, cw), lambda i: (jnp.minimum((i + 1) * hb, last_hb), 0))
    vec = lambda rows: pl.BlockSpec((rows, cw), lambda i: (0, 0))
    return pl.pallas_call(
        kern,
        grid=(m // tn,),
        in_specs=[main, main, prev, prev, nxt, nxt, main,
                  vec(dw_w.shape[0]), vec(1), vec(1), vec(1)],
        out_specs=pl.BlockSpec((tn, cw), lambda i: (i, 0)),
        out_shape=jax.ShapeDtypeStruct((m, cw), BF16),
        scratch_shapes=[
            pltpu.VMEM((n_groups, tn + 2 * halo, V7X_LANES), F32),
            pltpu.VMEM((n_groups, tn, V7X_LANES), F32),
            pltpu.VMEM((n_groups, dw_w.shape[0] + 1, V7X_LANES), F32),
            pltpu.VMEM((tn, V7X_LANES), F32),
            pltpu.VMEM((tn, V7X_LANES), F32),
            pltpu.VMEM((tn, V7X_LANES), F32),
        ],
        compiler_params=_cparams(("arbitrary",)),
        name="conv_branch",
    )(glu_a, glu_b, glu_a, glu_b, glu_a, glu_b, gate, dw_w, dw_b, ln_g, ln_b)


def _outproj_kernel(r_ref, c_ref, w1_ref, w2_ref, x_ref, gate_ref, pg_ref, o_ref, rs_ref, *, tn, row_chunk):
    tm, d = o_ref.shape
    ssq = jnp.zeros((tm, V7X_LANES), F32)
    ret = jnp.concatenate([r_ref[hd] for hd in range(r_ref.shape[0])], axis=1)
    for j0 in range(0, d, tn):
        cols = slice(j0, j0 + tn)
        y = jnp.dot(ret, w1_ref[:, cols], preferred_element_type=F32)
        y = y + jnp.dot(c_ref[...], w2_ref[:, cols], preferred_element_type=F32)
        o_ref[:, cols] = y
        for g0 in range(0, tn, V7X_LANES):
            yg = y[:, g0:g0 + V7X_LANES]
            ssq = ssq + yg * yg
    var = jnp.sum(ssq, axis=-1, keepdims=True) * (1.0 / d)
    rs_ref[...] = jnp.broadcast_to(lax.rsqrt(var + NORM_EPS), (tm, V7X_LANES))

    def body(r, carry):
        rows = pl.ds(pl.multiple_of(r * row_chunk, row_chunk), row_chunk)
        rs = rs_ref[rows, :]
        for g0 in range(0, d, V7X_LANES):
            lanes = slice(g0, g0 + V7X_LANES)
            gain = gate_ref[0, :, lanes] * pg_ref[:, lanes]
            o_ref[rows, lanes] = x_ref[rows, lanes] + o_ref[rows, lanes] * rs * gain
        return carry

    lax.fori_loop(0, tm // row_chunk, body, 0)


def _out_proj(ret_b, conv_b, w_out_b, x2d, gate3, post_g, *, tm, tiles_per_batch):
    m, d = x2d.shape
    heads, _, dh = ret_b.shape
    r = heads * dh
    cw = conv_b.shape[1]
    assert r == cw and w_out_b.shape == (r + cw, d)
    kern = functools.partial(_outproj_kernel, tn=512, row_chunk=16)
    return pl.pallas_call(
        kern,
        grid=(m // tm,),
        in_specs=[
            pl.BlockSpec((heads, tm, dh), lambda i: (0, i, 0)),
            pl.BlockSpec((tm, cw), lambda i: (i, 0)),
            pl.BlockSpec((r, d), lambda i: (0, 0), pipeline_mode=pl.Buffered(1)),
            pl.BlockSpec((cw, d), lambda i: (1, 0), pipeline_mode=pl.Buffered(1)),
            pl.BlockSpec((tm, d), lambda i: (i, 0)),
            pl.BlockSpec((1, 1, d), lambda i: (i // tiles_per_batch, 0, 0)),
            pl.BlockSpec((1, d), lambda i: (0, 0)),
        ],
        out_specs=pl.BlockSpec((tm, d), lambda i: (i, 0)),
        out_shape=jax.ShapeDtypeStruct((m, d), F32),
        scratch_shapes=[pltpu.VMEM((tm, V7X_LANES), F32)],
        compiler_params=_cparams(("arbitrary",)),
        name="out_proj",
    )(ret_b, conv_b, w_out_b, w_out_b, x2d, gate3, post_g)


def kernel(x, c, ctx, c_ctx, ada_w, ada_b, pre_norm_g, post_norm_g, w_in, ret_log_decay_fwd,
           ret_log_decay_bwd, ret_gn_g, conv_dw_w, conv_dw_b, conv_ln_g, conv_ln_b, w_out):
    batch, n, d = x.shape
    n_ctx = ctx.shape[1]
    depth = ada_w.shape[0]
    r = ret_gn_g.shape[1]
    cw = conv_dw_w.shape[2]
    heads = RET_HEADS
    dh = r // heads
    assert depth == 1, "single-layer block: the context stream is never updated"
    assert w_in.shape[2] == 4 * r + 3 * cw and r == cw

    cos, sin = _rope_tables(n, dh)
    x2d = x.reshape(batch * n, d)
    ctx2d = ctx.reshape(batch * n_ctx, d)

    mod_rows = 8
    cc = jnp.concatenate([c, c_ctx[None, :], jnp.zeros((mod_rows - batch - 1, d), F32)], axis=0)
    mod2 = _ada_mod(cc, ada_w[0], ada_b, 2 * d).reshape(mod_rows, 2, d)

    ret_cols = 4 * r
    w_ret_b = w_in[0, :, :ret_cols].astype(BF16)

    tm_norm = 512
    h = _prenorm(x2d, pre_norm_g, mod2, tm=tm_norm, mod_row_fn=lambda i: i // (n // tm_norm), name="prenorm")
    h_ctx = _prenorm(ctx2d, pre_norm_g, mod2, tm=tm_norm, mod_row_fn=lambda i: batch, name="prenorm_ctx")

    tm_in, tn_in = 1024, 1024
    side_casts = [(w_in[0], ret_cols // cw + k, cw) for k in range(3)] + [(w_out[0], 0, d)]
    proj, w_glu_a, w_glu_b, w_gate, w_out_b = _in_proj(
        h, w_ret_b, tm=tm_in, tn=tn_in, col_block0=0, n_col_blocks=ret_cols // tn_in, name="in_proj",
        rope=(cos, sin, n, r), side_casts=side_casts)
    kv_c = _in_proj(h_ctx, w_ret_b, tm=batch * n_ctx, tn=tn_in, col_block0=r // tn_in,
                    n_col_blocks=2 * r // tn_in, name="in_proj_ctx")
    glu_a, ada_gate = _in_proj(h, w_glu_a, tm=tm_in, tn=tn_in, col_block0=0, n_col_blocks=cw // tn_in,
                               name="in_proj_glu_a", ada_gate=(cc, ada_w[0], ada_b, 2 * d, d))
    glu_b, gate_c = [
        _in_proj(h, w_k, tm=tm_in, tn=tn_in, col_block0=0, n_col_blocks=cw // tn_in, name=name)
        for w_k, name in ((w_glu_b, "in_proj_glu_b"), (w_gate, "in_proj_conv_gate"))]

    lg = jnp.concatenate([ret_log_decay_fwd, ret_log_decay_bwd], axis=0)
    ret_b = _retention(lg, proj, kv_c, ret_gn_g, batch=batch, n=n, n_ctx=n_ctx, heads=heads, dh=dh)
    conv_b = _conv_branch(glu_a, glu_b, gate_c, conv_dw_w[0], conv_dw_b, conv_ln_g, conv_ln_b, n=n)

    tm_out = 256
    out = _out_proj(ret_b, conv_b, w_out_b, x2d, ada_gate.reshape(mod_rows, 1, d), post_norm_g,
                    tm=tm_out, tiles_per_batch=n // tm_out)
    return out.reshape(batch, n, d)
```

```python
import functools

import numpy as np
import jax
import jax.numpy as jnp
from jax import lax
from jax.experimental import pallas as pl
from jax.experimental.pallas import tpu as pltpu

F32 = jnp.float32
BF16 = jnp.bfloat16

GRID_W = 64
RET_HEADS = 8
ROPE_BASE = 10000.0
NORM_EPS = 1e-6

V7X_LANES = 128
V7X_SUBLANES_BF16 = 16
V7X_VMEM_LIMIT_BYTES = 58 * 1024 * 1024

RET_CHUNK = 256
CONV_HALO = 16


def _cparams(semantics):
    return pltpu.CompilerParams(dimension_semantics=semantics, vmem_limit_bytes=V7X_VMEM_LIMIT_BYTES)


def _silu(x):
    h = 0.5 * x
    return h + h * jnp.tanh(h)


def _mul_sigmoid(a, b):
    ha = 0.5 * a
    return ha + ha * jnp.tanh(0.5 * b)


def _ada_columns(s_ref, w_ref, b_ref):
    s = s_ref[...]
    s = s * jax.nn.sigmoid(s)
    acc = jnp.dot(s.astype(BF16), w_ref[...].astype(BF16), preferred_element_type=F32)
    return acc + b_ref[...]


def _ada_kernel(s_ref, w_ref, b_ref, o_ref):
    o_ref[...] = _ada_columns(s_ref, w_ref, b_ref)


def _ada_mod(cc, ada_w, ada_b, n_cols, tn=512):
    rows, d = cc.shape
    return pl.pallas_call(
        _ada_kernel,
        grid=(n_cols // tn,),
        in_specs=[
            pl.BlockSpec((rows, d), lambda j: (0, 0)),
            pl.BlockSpec((d, tn), lambda j: (0, j)),
            pl.BlockSpec((1, tn), lambda j: (0, j)),
        ],
        out_specs=pl.BlockSpec((rows, tn), lambda j: (0, j)),
        out_shape=jax.ShapeDtypeStruct((rows, n_cols), F32),
        compiler_params=_cparams(("arbitrary",)),
        name="ada_mod",
    )(cc, ada_w, ada_b)


def _prenorm_kernel(x_ref, g_ref, mod_ref, h_ref, *, row_chunk):
    gain = g_ref[...] * (1.0 + mod_ref[0, 1:2, :])
    shift = mod_ref[0, 0:1, :]

    def body(r, carry):
        rows = pl.ds(pl.multiple_of(r * row_chunk, row_chunk), row_chunk)
        xc = x_ref[rows, :]
        var = jnp.mean(xc * xc, axis=-1, keepdims=True)
        h_ref[rows, :] = (xc * lax.rsqrt(var + NORM_EPS) * gain + shift).astype(h_ref.dtype)
        return carry

    lax.fori_loop(0, x_ref.shape[0] // row_chunk, body, 0, unroll=4)


def _prenorm(x2d, pre_g, mod3, *, tm, mod_row_fn, name):
    m, d = x2d.shape
    return pl.pallas_call(
        functools.partial(_prenorm_kernel, row_chunk=16),
        grid=(m // tm,),
        in_specs=[
            pl.BlockSpec((tm, d), lambda i: (i, 0)),
            pl.BlockSpec((1, d), lambda i: (0, 0)),
            pl.BlockSpec((1, 2, d), lambda i: (mod_row_fn(i), 0, 0)),
        ],
        out_specs=pl.BlockSpec((tm, d), lambda i: (i, 0)),
        out_shape=jax.ShapeDtypeStruct((m, d), BF16),
        compiler_params=_cparams(("arbitrary",)),
        name=name,
    )(x2d, pre_g, mod3)


def _inproj_kernel(h_ref, w_ref, o_ref):
    o_ref[...] = jnp.dot(h_ref[...], w_ref[...], preferred_element_type=F32).astype(o_ref.dtype)


def _inproj_gate_kernel(h_ref, w_ref, s_ref, aw_ref, ab_ref, o_ref, g_ref):
    o_ref[...] = jnp.dot(h_ref[...], w_ref[...], preferred_element_type=F32).astype(o_ref.dtype)
    g_ref[...] = _ada_columns(s_ref, aw_ref, ab_ref)


def _inproj_rope_kernel(h_ref, w_ref, cos_ref, sin_ref, *refs, q_tiles, k_tiles, head_dim):
    n_cast = len(refs) // 2
    o_ref = refs[n_cast]
    for src_ref, dst_ref in zip(refs[:n_cast], refs[n_cast + 1:]):
        dst_ref[...] = src_ref[...].astype(dst_ref.dtype)
    j = pl.program_id(1)
    is_rope = j < q_tiles + k_tiles
    scale = jnp.where(j >= q_tiles, head_dim ** -0.5, 1.0).astype(F32)
    half = V7X_LANES // 2
    groups_per_head = head_dim // V7X_LANES
    acc = jnp.dot(h_ref[...], w_ref[...], preferred_element_type=F32)
    for g in range(acc.shape[1] // V7X_LANES):
        lanes = slice(g * V7X_LANES, (g + 1) * V7X_LANES)
        tl = slice((g % groups_per_head) * V7X_LANES, (g % groups_per_head + 1) * V7X_LANES)
        t = acc[:, lanes]
        roped = (t * cos_ref[:, tl] + pltpu.roll(t, half, axis=1) * sin_ref[:, tl]) * scale
        o_ref[g // groups_per_head, :, tl] = jnp.where(is_rope, roped, t).astype(o_ref.dtype)


def _in_proj(h, w_bf16, *, tm, tn, col_block0, n_col_blocks, name, rope=None, side_casts=(), ada_gate=None):
    m, d = h.shape
    n_steps = (m // tm) * n_col_blocks
    in_specs = [
        pl.BlockSpec((tm, d), lambda i, j: (i, 0)),
        pl.BlockSpec((d, tn), lambda i, j: (0, j + col_block0)),
    ]
    args = [h, w_bf16]
    out_specs = [pl.BlockSpec((tm, tn), lambda i, j: (i, j))]
    out_shape = [jax.ShapeDtypeStruct((m, n_col_blocks * tn), BF16)]
    if rope is None and ada_gate is not None:
        assert not side_casts
        cc, ada_w, ada_b, col0, n_cols = ada_gate
        gw = n_cols // n_steps
        assert gw * n_steps == n_cols and gw % V7X_LANES == 0 and col0 % gw == 0
        step = lambda i, j: i * n_col_blocks + j
        in_specs += [
            pl.BlockSpec(cc.shape, lambda i, j: (0, 0)),
            pl.BlockSpec((ada_w.shape[0], gw), lambda i, j: (0, col0 // gw + step(i, j))),
            pl.BlockSpec((1, gw), lambda i, j: (0, col0 // gw + step(i, j))),
        ]
        args += [cc, ada_w, ada_b]
        out_specs.append(pl.BlockSpec((cc.shape[0], gw), lambda i, j: (0, step(i, j))))
        out_shape.append(jax.ShapeDtypeStruct((cc.shape[0], n_cols), F32))
        kern = _inproj_gate_kernel
    elif rope is None:
        assert not side_casts
        kern = _inproj_kernel
    else:
        assert ada_gate is None
        cos, sin, n_tokens, rope_cols = rope
        head_dim = cos.shape[1]
        tiles_per_seq = n_tokens // tm
        table = pl.BlockSpec((tm, head_dim), lambda i, j: (i % tiles_per_seq, 0))
        in_specs += [table, table]
        args += [cos, sin]
        heads_per_tile = tn // head_dim
        out_specs = [pl.BlockSpec((heads_per_tile, tm, head_dim), lambda i, j: (j, i, 0))]
        out_shape = [jax.ShapeDtypeStruct((n_col_blocks * heads_per_tile, m, head_dim), BF16)]
        for src, col_block, width in side_casts:
            rows = src.shape[0]
            assert rows % n_steps == 0 and (rows // n_steps) % V7X_SUBLANES_BF16 == 0
            slab = rows // n_steps
            in_specs.append(pl.BlockSpec((slab, width), lambda i, j, cb=col_block: (i * n_col_blocks + j, cb)))
            out_specs.append(pl.BlockSpec((slab, width), lambda i, j: (i * n_col_blocks + j, 0)))
            out_shape.append(jax.ShapeDtypeStruct((rows, width), BF16))
            args.append(src)
        kern = functools.partial(_inproj_rope_kernel, q_tiles=rope_cols // tn, k_tiles=rope_cols // tn,
                                 head_dim=head_dim)
    outs = pl.pallas_call(
        kern,
        grid=(m // tm, n_col_blocks),
        in_specs=in_specs,
        out_specs=out_specs,
        out_shape=out_shape,
        compiler_params=_cparams(("arbitrary", "arbitrary")),
        name=name,
    )(*args)
    return outs[0] if len(outs) == 1 else outs


def _rope_tables(n_tokens, head_dim):
    rows = n_tokens // GRID_W
    pos_r = np.repeat(np.arange(rows, dtype=np.float64), GRID_W)
    pos_c = np.tile(np.arange(GRID_W, dtype=np.float64), rows)
    n_freq = head_dim // 4
    inv_freq = ROPE_BASE ** (-np.arange(n_freq, dtype=np.float64) / n_freq)
    ang_r = pos_r[:, None] * inv_freq[None, :]
    ang_c = pos_c[:, None] * inv_freq[None, :]
    cos = np.concatenate([np.cos(ang_r), np.cos(ang_r), np.cos(ang_c), np.cos(ang_c)], axis=-1)
    sin = np.concatenate([-np.sin(ang_r), np.sin(ang_r), -np.sin(ang_c), np.sin(ang_c)], axis=-1)
    return jnp.asarray(cos, F32), jnp.asarray(sin, F32)


def _dot_tn(a, b):
    return lax.dot_general(a, b, (((0,), (0,)), ((), ())), preferred_element_type=F32)


def _retention_kernel(lg_ref, q_ref, k_ref, v_ref, gt_ref, kc_ref, vc_ref, gng_ref,
                      o_ref, c0_ref, r_ref, st_ref, dm_ref, wt_ref, raw_ref, kt_ref, *, chunk, unroll):
    n, dh = q_ref.shape
    nc = n // chunk
    n_ctx = kc_ref.shape[0]
    head = pl.program_id(1)
    lgf = lg_ref[0, head]
    lgb = lg_ref[1, head]

    col = lax.broadcasted_iota(jnp.int32, (chunk, dh), 0).astype(F32)
    rel = (lax.broadcasted_iota(jnp.int32, (chunk, chunk), 0)
           - lax.broadcasted_iota(jnp.int32, (chunk, chunk), 1)).astype(F32)
    dm_ref[...] = (jnp.where(rel >= 0, jnp.exp(lgf * jnp.maximum(rel, 0.0)), 0.0)
                   + jnp.where(rel <= 0, jnp.exp(lgb * jnp.maximum(-rel, 0.0)), 0.0))
    wt_ref[0] = jnp.exp(lgf * (col + 1.0))
    wt_ref[1] = jnp.exp(lgb * (chunk - col))
    wt_ref[2] = jnp.exp(lgf * (chunk - 1.0 - col))
    wt_ref[3] = jnp.exp(lgb * col)
    dec = jnp.concatenate([jnp.exp(jnp.full((1, dh), lgf * chunk, F32)),
                           jnp.exp(jnp.full((1, dh), lgb * chunk, F32))], axis=1)

    def weighted_v(v, w_fwd, w_bwd):
        vf = v.astype(F32)
        return jnp.concatenate([vf * w_fwd, vf * w_bwd], axis=1).astype(BF16)

    pos_c = lax.broadcasted_iota(jnp.int32, (n_ctx, dh), 0).astype(F32)
    kc = (kc_ref[...].astype(F32) * dh ** -0.5).astype(BF16)
    c0_ref[...] = _dot_tn(kc, weighted_v(vc_ref[...], jnp.exp(lgf * (n_ctx - 1.0 - pos_c)), jnp.exp(lgb * pos_c)))
    st_ref[...] = c0_ref[...]

    def kt_body(c, carry):
        rows = pl.ds(pl.multiple_of(c * chunk, chunk), chunk)
        kt_ref[c] = k_ref[rows, :].T
        return carry

    lax.fori_loop(0, nc, kt_body, 0, unroll=unroll)

    def scan_body(t, carry):
        cf = t
        cb = nc - 1 - t
        rows_f = pl.ds(pl.multiple_of(cf * chunk, chunk), chunk)
        rows_b = pl.ds(pl.multiple_of(cb * chunk, chunk), chunk)
        state = st_ref[...]
        r_ref[cf, :, 0:dh] = state[:, 0:dh].astype(BF16)
        r_ref[cb, :, dh:2 * dh] = state[:, dh:2 * dh].astype(BF16)
        kv_f = jnp.dot(kt_ref[cf], (v_ref[rows_f, :].astype(F32) * wt_ref[2]).astype(BF16),
                       preferred_element_type=F32)
        kv_b = jnp.dot(kt_ref[cb], (v_ref[rows_b, :].astype(F32) * wt_ref[3]).astype(BF16),
                       preferred_element_type=F32)
        st_ref[...] = state * dec + jnp.concatenate([kv_f, kv_b], axis=1)
        return carry

    lax.fori_loop(0, nc, scan_body, 0, unroll=unroll)

    gng = gng_ref[...]

    def raw_out(c):
        rows = pl.ds(pl.multiple_of(c * chunk, chunk), chunk)
        q = q_ref[rows, :]
        scores = jnp.dot(q, kt_ref[c], preferred_element_type=F32) * dm_ref[...]
        o = jnp.dot(scores.astype(BF16), v_ref[rows, :], preferred_element_type=F32)
        x = jnp.dot(q, r_ref[c], preferred_element_type=F32)
        return o + x[:, 0:dh] * wt_ref[0] + x[:, dh:2 * dh] * wt_ref[1]

    def finish(c, o):
        rows = pl.ds(pl.multiple_of(c * chunk, chunk), chunk)
        mu = jnp.mean(o, axis=-1, keepdims=True)
        oc = o - mu
        var = jnp.mean(oc * oc, axis=-1, keepdims=True)
        y = oc * lax.rsqrt(var + NORM_EPS) * gng
        o_ref[rows, :] = (y * _silu(gt_ref[rows, :].astype(F32))).astype(o_ref.dtype)

    raw_ref[0] = raw_out(0)

    def out_body(c, carry):
        prev = raw_ref[(c - 1) % 2]
        raw_ref[c % 2] = raw_out(c)
        finish(c - 1, prev)
        return carry

    lax.fori_loop(1, nc, out_body, 0, unroll=unroll)
    finish(nc - 1, raw_ref[(nc - 1) % 2])


def _retention(lg, proj, kv_c, gn_g, *, batch, n, n_ctx, heads, dh):
    r = heads * dh
    nc = n // RET_CHUNK
    kern = functools.partial(_retention_kernel, chunk=RET_CHUNK, unroll=2)
    return pl.pallas_call(
        kern,
        grid=(batch, heads),
        in_specs=[
            pl.BlockSpec(memory_space=pltpu.SMEM),
            pl.BlockSpec((None, n, dh), lambda b, h: (h, b, 0)),
            pl.BlockSpec((None, n, dh), lambda b, h: (heads + h, b, 0)),
            pl.BlockSpec((None, n, dh), lambda b, h: (2 * heads + h, b, 0)),
            pl.BlockSpec((None, n, dh), lambda b, h: (3 * heads + h, b, 0)),
            pl.BlockSpec((n_ctx, dh), lambda b, h: (b, h)),
            pl.BlockSpec((n_ctx, dh), lambda b, h: (b, heads + h)),
            pl.BlockSpec((1, dh), lambda b, h: (0, h)),
        ],
        out_specs=pl.BlockSpec((None, n, dh), lambda b, h: (h, b, 0)),
        out_shape=jax.ShapeDtypeStruct((heads, batch * n, dh), BF16),
        scratch_shapes=[
            pltpu.VMEM((dh, 2 * dh), F32),
            pltpu.VMEM((nc, dh, 2 * dh), BF16),
            pltpu.VMEM((dh, 2 * dh), F32),
            pltpu.VMEM((RET_CHUNK, RET_CHUNK), F32),
            pltpu.VMEM((4, RET_CHUNK, dh), F32),
            pltpu.VMEM((2, RET_CHUNK, dh), F32),
            pltpu.VMEM((nc, dh, RET_CHUNK), BF16),
        ],
        compiler_params=_cparams(("arbitrary", "arbitrary")),
        name="retention",
    )(lg, proj, proj, proj, proj, kv_c, kv_c, gn_g)


def _conv_kernel(a_ref, b_ref, ap_ref, bp_ref, an_ref, bn_ref, gc_ref, w_ref, cb_ref, lg_ref, lb_ref,
                 o_ref, u_ref, y_ref, w3_ref, s1_ref, mu_ref, rs_ref, *, tiles_per_seq, row_chunk, conv_rows):
    tn, cw = a_ref.shape
    halo = ap_ref.shape[0]
    taps = w_ref.shape[0]
    pad = taps // 2
    n_groups = cw // V7X_LANES
    il = pl.program_id(0) % tiles_per_seq

    def glu(a, b):
        return _mul_sigmoid(a.astype(F32), b.astype(F32))

    def lane_group(val, g):
        return val[:, g * V7X_LANES:(g + 1) * V7X_LANES]

    u_prev = jnp.where(il > 0, glu(ap_ref[...], bp_ref[...]), 0.0)
    u_next = jnp.where(il < tiles_per_seq - 1, glu(an_ref[...], bn_ref[...]), 0.0)
    for g in range(n_groups):
        u_ref[g, 0:halo, :] = lane_group(u_prev, g)
        u_ref[g, halo + tn:2 * halo + tn, :] = lane_group(u_next, g)
        w3_ref[g, 0:taps, :] = lane_group(w_ref[...], g)
        w3_ref[g, taps:taps + 1, :] = lane_group(cb_ref[...], g)

    def glu_body(r, carry):
        r0 = pl.multiple_of(r * row_chunk, row_chunk)
        val = glu(a_ref[pl.ds(r0, row_chunk), :], b_ref[pl.ds(r0, row_chunk), :])
        for g in range(n_groups):
            u_ref[g, pl.ds(halo + r0, row_chunk), :] = lane_group(val, g)
        return carry

    lax.fori_loop(0, tn // row_chunk, glu_body, 0)

    s1_ref[...] = jnp.zeros_like(s1_ref)

    def conv_body(g, carry):
        for r0 in range(0, tn, conv_rows):
            acc = jnp.broadcast_to(w3_ref[g, taps:taps + 1, :], (conv_rows, V7X_LANES))
            for t in range(taps):
                s0 = r0 + halo - pad + t
                acc = acc + u_ref[g, s0:s0 + conv_rows, :] * w3_ref[g, t:t + 1, :]
            y_ref[g, r0:r0 + conv_rows, :] = acc
            s1_ref[r0:r0 + conv_rows, :] += acc
        return carry

    lax.fori_loop(0, n_groups, conv_body, 0)

    inv_cw = 1.0 / cw
    for r0 in range(0, tn, conv_rows):
        rows = slice(r0, r0 + conv_rows)
        mu = jnp.broadcast_to(jnp.sum(s1_ref[rows, :], axis=-1, keepdims=True) * inv_cw, (conv_rows, V7X_LANES))
        sq = jnp.zeros((conv_rows, V7X_LANES), F32)
        for g in range(n_groups):
            dlt = y_ref[g, rows, :] - mu
            sq = sq + dlt * dlt
        var = jnp.sum(sq, axis=-1, keepdims=True) * inv_cw
        mu_ref[rows, :] = mu
        rs_ref[rows, :] = jnp.broadcast_to(lax.rsqrt(var + NORM_EPS), (conv_rows, V7X_LANES))

    def row_body(r, carry):
        r0 = pl.multiple_of(r * row_chunk, row_chunk)
        rows = pl.ds(r0, row_chunk)
        mu = mu_ref[rows, :]
        rs = rs_ref[rows, :]
        for g in range(n_groups):
            lanes = slice(g * V7X_LANES, (g + 1) * V7X_LANES)
            z = (y_ref[g, rows, :] - mu) * rs * lg_ref[:, lanes] + lb_ref[:, lanes]
            gate = gc_ref[rows, lanes].astype(F32)
            o_ref[rows, lanes] = (_silu(z) * _silu(gate)).astype(o_ref.dtype)
        return carry

    lax.fori_loop(0, tn // row_chunk, row_body, 0)


def _conv_branch(glu_a, glu_b, gate, dw_w, dw_b, ln_g, ln_b, *, n, tn=512):
    m, cw = glu_a.shape
    halo = CONV_HALO
    assert n % tn == 0 and tn % halo == 0 and halo >= dw_w.shape[0] // 2 and cw % V7X_LANES == 0
    tiles_per_seq = n // tn
    hb = tn // halo
    last_hb = m // halo - 1
    n_groups = cw // V7X_LANES
    kern = functools.partial(_conv_kernel, tiles_per_seq=tiles_per_seq, row_chunk=16, conv_rows=64)
    main = pl.BlockSpec((tn, cw), lambda i: (i, 0))
    prev = pl.BlockSpec((halo, cw), lambda i: (jnp.maximum(i * hb - 1, 0), 0))
    nxt = pl.BlockSpec((halo, cw), lambda i: (jnp.minimum((i + 1) * hb, last_hb), 0))
    vec = lambda rows: pl.BlockSpec((rows, cw), lambda i: (0, 0))
    return pl.pallas_call(
        kern,
        grid=(m // tn,),
        in_specs=[main, main, prev, prev, nxt, nxt, main,
                  vec(dw_w.shape[0]), vec(1), vec(1), vec(1)],
        out_specs=pl.BlockSpec((tn, cw), lambda i: (i, 0)),
        out_shape=jax.ShapeDtypeStruct((m, cw), BF16),
        scratch_shapes=[
            pltpu.VMEM((n_groups, tn + 2 * halo, V7X_LANES), F32),
            pltpu.VMEM((n_groups, tn, V7X_LANES), F32),
            pltpu.VMEM((n_groups, dw_w.shape[0] + 1, V7X_LANES), F32),
            pltpu.VMEM((tn, V7X_LANES), F32),
            pltpu.VMEM((tn, V7X_LANES), F32),
            pltpu.VMEM((tn, V7X_LANES), F32),
        ],
        compiler_params=_cparams(("arbitrary",)),
        name="conv_branch",
    )(glu_a, glu_b, glu_a, glu_b, glu_a, glu_b, gate, dw_w, dw_b, ln_g, ln_b)


def _outproj_kernel(r_ref, c_ref, w1_ref, w2_ref, x_ref, gate_ref, pg_ref, o_ref, rs_ref, *, tn, row_chunk):
    tm, d = o_ref.shape
    ssq = jnp.zeros((tm, V7X_LANES), F32)
    ret = jnp.concatenate([r_ref[hd] for hd in range(r_ref.shape[0])], axis=1)
    for j0 in range(0, d, tn):
        cols = slice(j0, j0 + tn)
        y = jnp.dot(ret, w1_ref[:, cols], preferred_element_type=F32)
        y = y + jnp.dot(c_ref[...], w2_ref[:, cols], preferred_element_type=F32)
        o_ref[:, cols] = y
        for g0 in range(0, tn, V7X_LANES):
            yg = y[:, g0:g0 + V7X_LANES]
            ssq = ssq + yg * yg
    var = jnp.sum(ssq, axis=-1, keepdims=True) * (1.0 / d)
    rs_ref[...] = jnp.broadcast_to(lax.rsqrt(var + NORM_EPS), (tm, V7X_LANES))

    def body(r, carry):
        rows = pl.ds(pl.multiple_of(r * row_chunk, row_chunk), row_chunk)
        rs = rs_ref[rows, :]
        for g0 in range(0, d, V7X_LANES):
            lanes = slice(g0, g0 + V7X_LANES)
            gain = gate_ref[0, :, lanes] * pg_ref[:, lanes]
            o_ref[rows, lanes] = x_ref[rows, lanes] + o_ref[rows, lanes] * rs * gain
        return carry

    lax.fori_loop(0, tm // row_chunk, body, 0)


def _out_proj(ret_b, conv_b, w_out_b, x2d, gate3, post_g, *, tm, tiles_per_batch):
    m, d = x2d.shape
    heads, _, dh = ret_b.shape
    r = heads * dh
    cw = conv_b.shape[1]
    assert r == cw and w_out_b.shape == (r + cw, d)
    kern = functools.partial(_outproj_kernel, tn=512, row_chunk=16)
    return pl.pallas_call(
        kern,
        grid=(m // tm,),
        in_specs=[
            pl.BlockSpec((heads, tm, dh), lambda i: (0, i, 0)),
            pl.BlockSpec((tm, cw), lambda i: (i, 0)),
            pl.BlockSpec((r, d), lambda i: (0, 0), pipeline_mode=pl.Buffered(1)),
            pl.BlockSpec((cw, d), lambda i: (1, 0), pipeline_mode=pl.Buffered(1)),
            pl.BlockSpec((tm, d), lambda i: (i, 0)),
            pl.BlockSpec((1, 1, d), lambda i: (i // tiles_per_batch, 0, 0)),
            pl.BlockSpec((1, d), lambda i: (0, 0)),
        ],
        out_specs=pl.BlockSpec((tm, d), lambda i: (i, 0)),
        out_shape=jax.ShapeDtypeStruct((m, d), F32),
        scratch_shapes=[pltpu.VMEM((tm, V7X_LANES), F32)],
        compiler_params=_cparams(("arbitrary",)),
        name="out_proj",
    )(ret_b, conv_b, w_out_b, w_out_b, x2d, gate3, post_g)


def kernel(x, c, ctx, c_ctx, ada_w, ada_b, pre_norm_g, post_norm_g, w_in, ret_log_decay_fwd,
           ret_log_decay_bwd, ret_gn_g, conv_dw_w, conv_dw_b, conv_ln_g, conv_ln_b, w_out):
    batch, n, d = x.shape
    n_ctx = ctx.shape[1]
    depth = ada_w.shape[0]
    r = ret_gn_g.shape[1]
    cw = conv_dw_w.shape[2]
    heads = RET_HEADS
    dh = r // heads
    assert depth == 1, "single-layer block: the context stream is never updated"
    assert w_in.shape[2] == 4 * r + 3 * cw and r == cw

    cos, sin = _rope_tables(n, dh)
    x2d = x.reshape(batch * n, d)
    ctx2d = ctx.reshape(batch * n_ctx, d)

    mod_rows = 8
    cc = jnp.concatenate([c, c_ctx[None, :], jnp.zeros((mod_rows - batch - 1, d), F32)], axis=0)
    mod2 = _ada_mod(cc, ada_w[0], ada_b, 2 * d).reshape(mod_rows, 2, d)

    ret_cols = 4 * r
    w_ret_b = w_in[0, :, :ret_cols].astype(BF16)

    tm_norm = 512
    h = _prenorm(x2d, pre_norm_g, mod2, tm=tm_norm, mod_row_fn=lambda i: i // (n // tm_norm), name="prenorm")
    h_ctx = _prenorm(ctx2d, pre_norm_g, mod2, tm=tm_norm, mod_row_fn=lambda i: batch, name="prenorm_ctx")

    tm_in, tn_in = 1024, 1024
    side_casts = [(w_in[0], ret_cols // cw + k, cw) for k in range(3)] + [(w_out[0], 0, d)]
    proj, w_glu_a, w_glu_b, w_gate, w_out_b = _in_proj(
        h, w_ret_b, tm=tm_in, tn=tn_in, col_block0=0, n_col_blocks=ret_cols // tn_in, name="in_proj",
        rope=(cos, sin, n, r), side_casts=side_casts)
    kv_c = _in_proj(h_ctx, w_ret_b, tm=batch * n_ctx, tn=tn_in, col_block0=r // tn_in,
                    n_col_blocks=2 * r // tn_in, name="in_proj_ctx")
    glu_a, ada_gate = _in_proj(h, w_glu_a, tm=tm_in, tn=tn_in, col_block0=0, n_col_blocks=cw // tn_in,
                               name="in_proj_glu_a", ada_gate=(cc, ada_w[0], ada_b, 2 * d, d))
    glu_b, gate_c = [
        _in_proj(h, w_k, tm=tm_in, tn=tn_in, col_block0=0, n_col_blocks=cw // tn_in, name=name)
        for w_k, name in ((w_glu_b, "in_proj_glu_b"), (w_gate, "in_proj_conv_gate"))]

    lg = jnp.concatenate([ret_log_decay_fwd, ret_log_decay_bwd], axis=0)
    ret_b = _retention(lg, proj, kv_c, ret_gn_g, batch=batch, n=n, n_ctx=n_ctx, heads=heads, dh=dh)
    conv_b = _conv_branch(glu_a, glu_b, gate_c, conv_dw_w[0], conv_dw_b, conv_ln_g, conv_ln_b, n=n)

    tm_out = 256
    out = _out_proj(ret_b, conv_b, w_out_b, x2d, ada_gate.reshape(mod_rows, 1, d), post_norm_g,
                    tm=tm_out, tiles_per_batch=n // tm_out)
    return out.reshape(batch, n, d)
```

```python
import functools

import numpy as np
import jax
import jax.numpy as jnp
from jax import lax
from jax.experimental import pallas as pl
from jax.experimental.pallas import tpu as pltpu

F32 = jnp.float32
BF16 = jnp.bfloat16

GRID_W = 64
RET_HEADS = 8
ROPE_BASE = 10000.0
NORM_EPS = 1e-6

V7X_LANES = 128
V7X_SUBLANES_BF16 = 16
V7X_VMEM_LIMIT_BYTES = 58 * 1024 * 1024

RET_CHUNK = 256
CONV_HALO = 16


def _cparams(semantics):
    return pltpu.CompilerParams(dimension_semantics=semantics, vmem_limit_bytes=V7X_VMEM_LIMIT_BYTES)


def _silu(x):
    h = 0.5 * x
    return h + h * jnp.tanh(h)


def _mul_sigmoid(a, b):
    ha = 0.5 * a
    return ha + ha * jnp.tanh(0.5 * b)


def _ada_columns(s_ref, w_ref, b_ref):
    s = s_ref[...]
    s = s * jax.nn.sigmoid(s)
    acc = jnp.dot(s.astype(BF16), w_ref[...].astype(BF16), preferred_element_type=F32)
    return acc + b_ref[...]


def _ada_kernel(s_ref, w_ref, b_ref, o_ref):
    o_ref[...] = _ada_columns(s_ref, w_ref, b_ref)


def _ada_mod(cc, ada_w, ada_b, n_cols, tn=512):
    rows, d = cc.shape
    return pl.pallas_call(
        _ada_kernel,
        grid=(n_cols // tn,),
        in_specs=[
            pl.BlockSpec((rows, d), lambda j: (0, 0)),
            pl.BlockSpec((d, tn), lambda j: (0, j)),
            pl.BlockSpec((1, tn), lambda j: (0, j)),
        ],
        out_specs=pl.BlockSpec((rows, tn), lambda j: (0, j)),
        out_shape=jax.ShapeDtypeStruct((rows, n_cols), F32),
        compiler_params=_cparams(("arbitrary",)),
        name="ada_mod",
    )(cc, ada_w, ada_b)


def _prenorm_kernel(x_ref, g_ref, mod_ref, h_ref, *, row_chunk):
    gain = g_ref[...] * (1.0 + mod_ref[0, 1:2, :])
    shift = mod_ref[0, 0:1, :]

    def body(r, carry):
        rows = pl.ds(pl.multiple_of(r * row_chunk, row_chunk), row_chunk)
        xc = x_ref[rows, :]
        var = jnp.mean(xc * xc, axis=-1, keepdims=True)
        h_ref[rows, :] = (xc * lax.rsqrt(var + NORM_EPS) * gain + shift).astype(h_ref.dtype)
        return carry

    lax.fori_loop(0, x_ref.shape[0] // row_chunk, body, 0, unroll=4)


def _prenorm(x2d, pre_g, mod3, *, tm, mod_row_fn, name):
    m, d = x2d.shape
    return pl.pallas_call(
        functools.partial(_prenorm_kernel, row_chunk=16),
        grid=(m // tm,),
        in_specs=[
            pl.BlockSpec((tm, d), lambda i: (i, 0)),
            pl.BlockSpec((1, d), lambda i: (0, 0)),
            pl.BlockSpec((1, 2, d), lambda i: (mod_row_fn(i), 0, 0)),
        ],
        out_specs=pl.BlockSpec((tm, d), lambda i: (i, 0)),
        out_shape=jax.ShapeDtypeStruct((m, d), BF16),
        compiler_params=_cparams(("arbitrary",)),
        name=name,
    )(x2d, pre_g, mod3)


def _inproj_kernel(h_ref, w_ref, o_ref):
    o_ref[...] = jnp.dot(h_ref[...], w_ref[...], preferred_element_type=F32).astype(o_ref.dtype)


def _inproj_gate_kernel(h_ref, w_ref, s_ref, aw_ref, ab_ref, o_ref, g_ref):
    o_ref[...] = jnp.dot(h_ref[...], w_ref[...], preferred_element_type=F32).astype(o_ref.dtype)
    g_ref[...] = _ada_columns(s_ref, aw_ref, ab_ref)


def _inproj_rope_kernel(h_ref, w_ref, cos_ref, sin_ref, *refs, q_tiles, k_tiles, head_dim):
    n_cast = len(refs) // 2
    o_ref = refs[n_cast]
    for src_ref, dst_ref in zip(refs[:n_cast], refs[n_cast + 1:]):
        dst_ref[...] = src_ref[...].astype(dst_ref.dtype)
    j = pl.program_id(1)
    is_rope = j < q_tiles + k_tiles
    scale = jnp.where(j >= q_tiles, head_dim ** -0.5, 1.0).astype(F32)
    half = V7X_LANES // 2
    groups_per_head = head_dim // V7X_LANES
    acc = jnp.dot(h_ref[...], w_ref[...], preferred_element_type=F32)
    for g in range(acc.shape[1] // V7X_LANES):
        lanes = slice(g * V7X_LANES, (g + 1) * V7X_LANES)
        tl = slice((g % groups_per_head) * V7X_LANES, (g % groups_per_head + 1) * V7X_LANES)
        t = acc[:, lanes]
        roped = (t * cos_ref[:, tl] + pltpu.roll(t, half, axis=1) * sin_ref[:, tl]) * scale
        o_ref[g // groups_per_head, :, tl] = jnp.where(is_rope, roped, t).astype(o_ref.dtype)


def _in_proj(h, w_bf16, *, tm, tn, col_block0, n_col_blocks, name, rope=None, side_casts=(), ada_gate=None):
    m, d = h.shape
    n_steps = (m // tm) * n_col_blocks
    in_specs = [
        pl.BlockSpec((tm, d), lambda i, j: (i, 0)),
        pl.BlockSpec((d, tn), lambda i, j: (0, j + col_block0)),
    ]
    args = [h, w_bf16]
    out_specs = [pl.BlockSpec((tm, tn), lambda i, j: (i, j))]
    out_shape = [jax.ShapeDtypeStruct((m, n_col_blocks * tn), BF16)]
    if rope is None and ada_gate is not None:
        assert not side_casts
        cc, ada_w, ada_b, col0, n_cols = ada_gate
        gw = n_cols // n_steps
        assert gw * n_steps == n_cols and gw % V7X_LANES == 0 and col0 % gw == 0
        step = lambda i, j: i * n_col_blocks + j
        in_specs += [
            pl.BlockSpec(cc.shape, lambda i, j: (0, 0)),
            pl.BlockSpec((ada_w.shape[0], gw), lambda i, j: (0, col0 // gw + step(i, j))),
            pl.BlockSpec((1, gw), lambda i, j: (0, col0 // gw + step(i, j))),
        ]
        args += [cc, ada_w, ada_b]
        out_specs.append(pl.BlockSpec((cc.shape[0], gw), lambda i, j: (0, step(i, j))))
        out_shape.append(jax.ShapeDtypeStruct((cc.shape[0], n_cols), F32))
        kern = _inproj_gate_kernel
    elif rope is None:
        assert not side_casts
        kern = _inproj_kernel
    else:
        assert ada_gate is None
        cos, sin, n_tokens, rope_cols = rope
        head_dim = cos.shape[1]
        tiles_per_seq = n_tokens // tm
        table = pl.BlockSpec((tm, head_dim), lambda i, j: (i % tiles_per_seq, 0))
        in_specs += [table, table]
        args += [cos, sin]
        heads_per_tile = tn // head_dim
        out_specs = [pl.BlockSpec((heads_per_tile, tm, head_dim), lambda i, j: (j, i, 0))]
        out_shape = [jax.ShapeDtypeStruct((n_col_blocks * heads_per_tile, m, head_dim), BF16)]
        for src, col_block, width in side_casts:
            rows = src.shape[0]
            assert rows % n_steps == 0 and (rows // n_steps) % V7X_SUBLANES_BF16 == 0
            slab = rows // n_steps
            in_specs.append(pl.BlockSpec((slab, width), lambda i, j, cb=col_block: (i * n_col_blocks + j, cb)))
            out_specs.append(pl.BlockSpec((slab, width), lambda i, j: (i * n_col_blocks + j, 0)))
            out_shape.append(jax.ShapeDtypeStruct((rows, width), BF16))
            args.append(src)
        kern = functools.partial(_inproj_rope_kernel, q_tiles=rope_cols // tn, k_tiles=rope_cols // tn,
                                 head_dim=head_dim)
    outs = pl.pallas_call(
        kern,
        grid=(m // tm, n_col_blocks),
        in_specs=in_specs,
        out_specs=out_specs,
        out_shape=out_shape,
        compiler_params=_cparams(("arbitrary", "arbitrary")),
        name=name,
    )(*args)
    return outs[0] if len(outs) == 1 else outs


def _rope_tables(n_tokens, head_dim):
    rows = n_tokens // GRID_W
    pos_r = np.repeat(np.arange(rows, dtype=np.float64), GRID_W)
    pos_c = np.tile(np.arange(GRID_W, dtype=np.float64), rows)
    n_freq = head_dim // 4
    inv_freq = ROPE_BASE ** (-np.arange(n_freq, dtype=np.float64) / n_freq)
    ang_r = pos_r[:, None] * inv_freq[None, :]
    ang_c = pos_c[:, None] * inv_freq[None, :]
    cos = np.concatenate([np.cos(ang_r), np.cos(ang_r), np.cos(ang_c), np.cos(ang_c)], axis=-1)
    sin = np.concatenate([-np.sin(ang_r), np.sin(ang_r), -np.sin(ang_c), np.sin(ang_c)], axis=-1)
    return jnp.asarray(cos, F32), jnp.asarray(sin, F32)


def _dot_tn(a, b):
    return lax.dot_general(a, b, (((0,), (0,)), ((), ())), preferred_element_type=F32)


def _dot_nt(a, b):
    return lax.dot_general(a, b, (((1,), (1,)), ((), ())), preferred_element_type=F32)


def _retention_kernel(lg_ref, q_ref, k_ref, v_ref, gt_ref, kc_ref, vc_ref, gng_ref,
                      o_ref, c0_ref, r_ref, st_ref, dm_ref, wt_ref, raw_ref, *, chunk, unroll):
    n, dh = q_ref.shape
    nc = n // chunk
    n_ctx = kc_ref.shape[0]
    head = pl.program_id(1)
    lgf = lg_ref[0, head]
    lgb = lg_ref[1, head]

    col = lax.broadcasted_iota(jnp.int32, (chunk, dh), 0).astype(F32)
    rel = (lax.broadcasted_iota(jnp.int32, (chunk, chunk), 0)
           - lax.broadcasted_iota(jnp.int32, (chunk, chunk), 1)).astype(F32)
    dm_ref[...] = (jnp.where(rel >= 0, jnp.exp(lgf * jnp.maximum(rel, 0.0)), 0.0)
                   + jnp.where(rel <= 0, jnp.exp(lgb * jnp.maximum(-rel, 0.0)), 0.0))
    wt_ref[0] = jnp.exp(lgf * (col + 1.0))
    wt_ref[1] = jnp.exp(lgb * (chunk - col))
    wt_ref[2] = jnp.exp(lgf * (chunk - 1.0 - col))
    wt_ref[3] = jnp.exp(lgb * col)
    dec = jnp.concatenate([jnp.exp(jnp.full((1, dh), lgf * chunk, F32)),
                           jnp.exp(jnp.full((1, dh), lgb * chunk, F32))], axis=1)

    def weighted_v(v, w_fwd, w_bwd):
        vf = v.astype(F32)
        return jnp.concatenate([vf * w_fwd, vf * w_bwd], axis=1).astype(BF16)

    pos_c = lax.broadcasted_iota(jnp.int32, (n_ctx, dh), 0).astype(F32)
    kc = (kc_ref[...].astype(F32) * dh ** -0.5).astype(BF16)
    c0_ref[...] = _dot_tn(kc, weighted_v(vc_ref[...], jnp.exp(lgf * (n_ctx - 1.0 - pos_c)), jnp.exp(lgb * pos_c)))
    st_ref[...] = c0_ref[...]

    def scan_body(t, carry):
        cf = t
        cb = nc - 1 - t
        rows_f = pl.ds(pl.multiple_of(cf * chunk, chunk), chunk)
        rows_b = pl.ds(pl.multiple_of(cb * chunk, chunk), chunk)
        state = st_ref[...]
        r_ref[cf, :, 0:dh] = state[:, 0:dh].astype(BF16)
        r_ref[cb, :, dh:2 * dh] = state[:, dh:2 * dh].astype(BF16)
        kv_f = _dot_tn(k_ref[rows_f, :], (v_ref[rows_f, :].astype(F32) * wt_ref[2]).astype(BF16))
        kv_b = _dot_tn(k_ref[rows_b, :], (v_ref[rows_b, :].astype(F32) * wt_ref[3]).astype(BF16))
        st_ref[...] = state * dec + jnp.concatenate([kv_f, kv_b], axis=1)
        return carry

    lax.fori_loop(0, nc, scan_body, 0, unroll=unroll)

    gng = gng_ref[...]

    def raw_out(c):
        rows = pl.ds(pl.multiple_of(c * chunk, chunk), chunk)
        q = q_ref[rows, :]
        scores = _dot_nt(q, k_ref[rows, :]) * dm_ref[...]
        o = jnp.dot(scores.astype(BF16), v_ref[rows, :], preferred_element_type=F32)
        x = jnp.dot(q, r_ref[c], preferred_element_type=F32)
        return o + x[:, 0:dh] * wt_ref[0] + x[:, dh:2 * dh] * wt_ref[1]

    def finish(c, o):
        rows = pl.ds(pl.multiple_of(c * chunk, chunk), chunk)
        mu = jnp.mean(o, axis=-1, keepdims=True)
        oc = o - mu
        var = jnp.mean(oc * oc, axis=-1, keepdims=True)
        y = oc * lax.rsqrt(var + NORM_EPS) * gng
        o_ref[rows, :] = (y * _silu(gt_ref[rows, :].astype(F32))).astype(o_ref.dtype)

    raw_ref[0] = raw_out(0)

    def out_body(c, carry):
        prev = raw_ref[(c - 1) % 2]
        raw_ref[c % 2] = raw_out(c)
        finish(c - 1, prev)
        return carry

    lax.fori_loop(1, nc, out_body, 0, unroll=unroll)
    finish(nc - 1, raw_ref[(nc - 1) % 2])


def _retention(lg, proj, kv_c, gn_g, *, batch, n, n_ctx, heads, dh):
    r = heads * dh
    nc = n // RET_CHUNK
    kern = functools.partial(_retention_kernel, chunk=RET_CHUNK, unroll=2)
    return pl.pallas_call(
        kern,
        grid=(batch, heads),
        in_specs=[
            pl.BlockSpec(memory_space=pltpu.SMEM),
            pl.BlockSpec((None, n, dh), lambda b, h: (h, b, 0)),
            pl.BlockSpec((None, n, dh), lambda b, h: (heads + h, b, 0)),
            pl.BlockSpec((None, n, dh), lambda b, h: (2 * heads + h, b, 0)),
            pl.BlockSpec((None, n, dh), lambda b, h: (3 * heads + h, b, 0)),
            pl.BlockSpec((n_ctx, dh), lambda b, h: (b, h)),
            pl.BlockSpec((n_ctx, dh), lambda b, h: (b, heads + h)),
            pl.BlockSpec((1, dh), lambda b, h: (0, h)),
        ],
        out_specs=pl.BlockSpec((None, n, dh), lambda b, h: (h, b, 0)),
        out_shape=jax.ShapeDtypeStruct((heads, batch * n, dh), BF16),
        scratch_shapes=[
            pltpu.VMEM((dh, 2 * dh), F32),
            pltpu.VMEM((nc, dh, 2 * dh), BF16),
            pltpu.VMEM((dh, 2 * dh), F32),
            pltpu.VMEM((RET_CHUNK, RET_CHUNK), F32),
            pltpu.VMEM((4, RET_CHUNK, dh), F32),
            pltpu.VMEM((2, RET_CHUNK, dh), F32),
        ],
        compiler_params=_cparams(("arbitrary", "arbitrary")),
        name="retention",
    )(lg, proj, proj, proj, proj, kv_c, kv_c, gn_g)


def _conv_kernel(a_ref, b_ref, ap_ref, bp_ref, an_ref, bn_ref, gc_ref, w_ref, cb_ref, lg_ref, lb_ref,
                 o_ref, u_ref, y_ref, w3_ref, s1_ref, mu_ref, rs_ref, *, tiles_per_seq, row_chunk, conv_rows):
    tn, cw = a_ref.shape
    halo = ap_ref.shape[0]
    taps = w_ref.shape[0]
    pad = taps // 2
    n_groups = cw // V7X_LANES
    il = pl.program_id(0) % tiles_per_seq

    def glu(a, b):
        return _mul_sigmoid(a.astype(F32), b.astype(F32))

    def lane_group(val, g):
        return val[:, g * V7X_LANES:(g + 1) * V7X_LANES]

    u_prev = jnp.where(il > 0, glu(ap_ref[...], bp_ref[...]), 0.0)
    u_next = jnp.where(il < tiles_per_seq - 1, glu(an_ref[...], bn_ref[...]), 0.0)
    for g in range(n_groups):
        u_ref[g, 0:halo, :] = lane_group(u_prev, g)
        u_ref[g, halo + tn:2 * halo + tn, :] = lane_group(u_next, g)
        w3_ref[g, 0:taps, :] = lane_group(w_ref[...], g)
        w3_ref[g, taps:taps + 1, :] = lane_group(cb_ref[...], g)

    def glu_body(r, carry):
        r0 = pl.multiple_of(r * row_chunk, row_chunk)
        val = glu(a_ref[pl.ds(r0, row_chunk), :], b_ref[pl.ds(r0, row_chunk), :])
        for g in range(n_groups):
            u_ref[g, pl.ds(halo + r0, row_chunk), :] = lane_group(val, g)
        return carry

    lax.fori_loop(0, tn // row_chunk, glu_body, 0)

    s1_ref[...] = jnp.zeros_like(s1_ref)

    def conv_body(g, carry):
        for r0 in range(0, tn, conv_rows):
            acc = jnp.broadcast_to(w3_ref[g, taps:taps + 1, :], (conv_rows, V7X_LANES))
            for t in range(taps):
                s0 = r0 + halo - pad + t
                acc = acc + u_ref[g, s0:s0 + conv_rows, :] * w3_ref[g, t:t + 1, :]
            y_ref[g, r0:r0 + conv_rows, :] = acc
            s1_ref[r0:r0 + conv_rows, :] += acc
        return carry

    lax.fori_loop(0, n_groups, conv_body, 0)

    inv_cw = 1.0 / cw
    for r0 in range(0, tn, conv_rows):
        rows = slice(r0, r0 + conv_rows)
        mu = jnp.broadcast_to(jnp.sum(s1_ref[rows, :], axis=-1, keepdims=True) * inv_cw, (conv_rows, V7X_LANES))
        sq = jnp.zeros((conv_rows, V7X_LANES), F32)
        for g in range(n_groups):
            dlt = y_ref[g, rows, :] - mu
            sq = sq + dlt * dlt
        var = jnp.sum(sq, axis=-1, keepdims=True) * inv_cw
        mu_ref[rows, :] = mu
        rs_ref[rows, :] = jnp.broadcast_to(lax.rsqrt(var + NORM_EPS), (conv_rows, V7X_LANES))

    def row_body(r, carry):
        r0 = pl.multiple_of(r * row_chunk, row_chunk)
        rows = pl.ds(r0, row_chunk)
        mu = mu_ref[rows, :]
        rs = rs_ref[rows, :]
        for g in range(n_groups):
            lanes = slice(g * V7X_LANES, (g + 1) * V7X_LANES)
            z = (y_ref[g, rows, :] - mu) * rs * lg_ref[:, lanes] + lb_ref[:, lanes]
            gate = gc_ref[rows, lanes].astype(F32)
            o_ref[rows, lanes] = (_silu(z) * _silu(gate)).astype(o_ref.dtype)
        return carry

    lax.fori_loop(0, tn // row_chunk, row_body, 0)


def _conv_branch(glu_a, glu_b, gate, dw_w, dw_b, ln_g, ln_b, *, n, tn=512):
    m, cw = glu_a.shape
    halo = CONV_HALO
    assert n % tn == 0 and tn % halo == 0 and halo >= dw_w.shape[0] // 2 and cw % V7X_LANES == 0
    tiles_per_seq = n // tn
    hb = tn // halo
    last_hb = m // halo - 1
    n_groups = cw // V7X_LANES
    kern = functools.partial(_conv_kernel, tiles_per_seq=tiles_per_seq, row_chunk=16, conv_rows=64)
    main = pl.BlockSpec((tn, cw), lambda i: (i, 0))
    prev = pl.BlockSpec((halo, cw), lambda i: (jnp.maximum(i * hb - 1, 0), 0))
    nxt = pl.BlockSpec((halo, cw), lambda i: (jnp.minimum((i + 1) * hb, last_hb), 0))
    vec = lambda rows: pl.BlockSpec((rows, cw), lambda i: (0, 0))
    return pl.pallas_call(
        kern,
        grid=(m // tn,),
        in_specs=[main, main, prev, prev, nxt, nxt, main,
                  vec(dw_w.shape[0]), vec(1), vec(1), vec(1)],
        out_specs=pl.BlockSpec((tn, cw), lambda i: (i, 0)),
        out_shape=jax.ShapeDtypeStruct((m, cw), BF16),
        scratch_shapes=[
            pltpu.VMEM((n_groups, tn + 2 * halo, V7X_LANES), F32),
            pltpu.VMEM((n_groups, tn, V7X_LANES), F32),
            pltpu.VMEM((n_groups, dw_w.shape[0] + 1, V7X_LANES), F32),
            pltpu.VMEM((tn, V7X_LANES), F32),
            pltpu.VMEM((tn, V7X_LANES), F32),
            pltpu.VMEM((tn, V7X_LANES), F32),
        ],
        compiler_params=_cparams(("arbitrary",)),
        name="conv_branch",
    )(glu_a, glu_b, glu_a, glu_b, glu_a, glu_b, gate, dw_w, dw_b, ln_g, ln_b)


def _outproj_kernel(r_ref, c_ref, w1_ref, w2_ref, x_ref, gate_ref, pg_ref, o_ref, rs_ref, *, tn, row_chunk):
    tm, d = o_ref.shape
    ssq = jnp.zeros((tm, V7X_LANES), F32)
    ret = jnp.concatenate([r_ref[hd] for hd in range(r_ref.shape[0])], axis=1)
    for j0 in range(0, d, tn):
        cols = slice(j0, j0 + tn)
        y = jnp.dot(ret, w1_ref[:, cols], preferred_element_type=F32)
        y = y + jnp.dot(c_ref[...], w2_ref[:, cols], preferred_element_type=F32)
        o_ref[:, cols] = y
        for g0 in range(0, tn, V7X_LANES):
            yg = y[:, g0:g0 + V7X_LANES]
            ssq = ssq + yg * yg
    var = jnp.sum(ssq, axis=-1, keepdims=True) * (1.0 / d)
    rs_ref[...] = jnp.broadcast_to(lax.rsqrt(var + NORM_EPS), (tm, V7X_LANES))

    def body(r, carry):
        rows = pl.ds(pl.multiple_of(r * row_chunk, row_chunk), row_chunk)
        rs = rs_ref[rows, :]
        for g0 in range(0, d, V7X_LANES):
            lanes = slice(g0, g0 + V7X_LANES)
            gain = gate_ref[0, :, lanes] * pg_ref[:, lanes]
            o_ref[rows, lanes] = x_ref[rows, lanes] + o_ref[rows, lanes] * rs * gain
        return carry

    lax.fori_loop(0, tm // row_chunk, body, 0)


def _out_proj(ret_b, conv_b, w_out_b, x2d, gate3, post_g, *, tm, tiles_per_batch):
    m, d = x2d.shape
    heads, _, dh = ret_b.shape
    r = heads * dh
    cw = conv_b.shape[1]
    assert r == cw and w_out_b.shape == (r + cw, d)
    kern = functools.partial(_outproj_kernel, tn=512, row_chunk=16)
    return pl.pallas_call(
        kern,
        grid=(m // tm,),
        in_specs=[
            pl.BlockSpec((heads, tm, dh), lambda i: (0, i, 0)),
            pl.BlockSpec((tm, cw), lambda i: (i, 0)),
            pl.BlockSpec((r, d), lambda i: (0, 0), pipeline_mode=pl.Buffered(1)),
            pl.BlockSpec((cw, d), lambda i: (1, 0), pipeline_mode=pl.Buffered(1)),
            pl.BlockSpec((tm, d), lambda i: (i, 0)),
            pl.BlockSpec((1, 1, d), lambda i: (i // tiles_per_batch, 0, 0)),
            pl.BlockSpec((1, d), lambda i: (0, 0)),
        ],
        out_specs=pl.BlockSpec((tm, d), lambda i: (i, 0)),
        out_shape=jax.ShapeDtypeStruct((m, d), F32),
        scratch_shapes=[pltpu.VMEM((tm, V7X_LANES), F32)],
        compiler_params=_cparams(("arbitrary",)),
        name="out_proj",
    )(ret_b, conv_b, w_out_b, w_out_b, x2d, gate3, post_g)


def kernel(x, c, ctx, c_ctx, ada_w, ada_b, pre_norm_g, post_norm_g, w_in, ret_log_decay_fwd,
           ret_log_decay_bwd, ret_gn_g, conv_dw_w, conv_dw_b, conv_ln_g, conv_ln_b, w_out):
    batch, n, d = x.shape
    n_ctx = ctx.shape[1]
    depth = ada_w.shape[0]
    r = ret_gn_g.shape[1]
    cw = conv_dw_w.shape[2]
    heads = RET_HEADS
    dh = r // heads
    assert depth == 1, "single-layer block: the context stream is never updated"
    assert w_in.shape[2] == 4 * r + 3 * cw and r == cw

    cos, sin = _rope_tables(n, dh)
    x2d = x.reshape(batch * n, d)
    ctx2d = ctx.reshape(batch * n_ctx, d)

    mod_rows = 8
    cc = jnp.concatenate([c, c_ctx[None, :], jnp.zeros((mod_rows - batch - 1, d), F32)], axis=0)
    mod2 = _ada_mod(cc, ada_w[0], ada_b, 2 * d).reshape(mod_rows, 2, d)

    ret_cols = 4 * r
    w_ret_b = w_in[0, :, :ret_cols].astype(BF16)

    tm_norm = 512
    h = _prenorm(x2d, pre_norm_g, mod2, tm=tm_norm, mod_row_fn=lambda i: i // (n // tm_norm), name="prenorm")
    h_ctx = _prenorm(ctx2d, pre_norm_g, mod2, tm=tm_norm, mod_row_fn=lambda i: batch, name="prenorm_ctx")

    tm_in, tn_in = 1024, 1024
    side_casts = [(w_in[0], ret_cols // cw + k, cw) for k in range(3)] + [(w_out[0], 0, d)]
    proj, w_glu_a, w_glu_b, w_gate, w_out_b = _in_proj(
        h, w_ret_b, tm=tm_in, tn=tn_in, col_block0=0, n_col_blocks=ret_cols // tn_in, name="in_proj",
        rope=(cos, sin, n, r), side_casts=side_casts)
    kv_c = _in_proj(h_ctx, w_ret_b, tm=batch * n_ctx, tn=tn_in, col_block0=r // tn_in,
                    n_col_blocks=2 * r // tn_in, name="in_proj_ctx")
    glu_a, ada_gate = _in_proj(h, w_glu_a, tm=tm_in, tn=tn_in, col_block0=0, n_col_blocks=cw // tn_in,
                               name="in_proj_glu_a", ada_gate=(cc, ada_w[0], ada_b, 2 * d, d))
    glu_b, gate_c = [
        _in_proj(h, w_k, tm=tm_in, tn=cw, col_block0=0, n_col_blocks=1, name=name)
        for w_k, name in ((w_glu_b, "in_proj_glu_b"), (w_gate, "in_proj_conv_gate"))]

    lg = jnp.concatenate([ret_log_decay_fwd, ret_log_decay_bwd], axis=0)
    ret_b = _retention(lg, proj, kv_c, ret_gn_g, batch=batch, n=n, n_ctx=n_ctx, heads=heads, dh=dh)
    conv_b = _conv_branch(glu_a, glu_b, gate_c, conv_dw_w[0], conv_dw_b, conv_ln_g, conv_ln_b, n=n)

    tm_out = 256
    out = _out_proj(ret_b, conv_b, w_out_b, x2d, ada_gate.reshape(mod_rows, 1, d), post_norm_g,
                    tm=tm_out, tiles_per_batch=n // tm_out)
    return out.reshape(batch, n, d)
```

```python
import functools

import numpy as np
import jax
import jax.numpy as jnp
from jax import lax
from jax.experimental import pallas as pl
from jax.experimental.pallas import tpu as pltpu

F32 = jnp.float32
BF16 = jnp.bfloat16

GRID_W = 64
RET_HEADS = 8
ROPE_BASE = 10000.0
NORM_EPS = 1e-6

V7X_LANES = 128
V7X_SUBLANES_BF16 = 16
V7X_VMEM_LIMIT_BYTES = 58 * 1024 * 1024

RET_CHUNK = 256
CONV_HALO = 16


def _cparams(semantics):
    return pltpu.CompilerParams(dimension_semantics=semantics, vmem_limit_bytes=V7X_VMEM_LIMIT_BYTES)


def _silu(x):
    h = 0.5 * x
    return h + h * jnp.tanh(h)


def _mul_sigmoid(a, b):
    ha = 0.5 * a
    return ha + ha * jnp.tanh(0.5 * b)


def _ada_columns(s_ref, w_ref, b_ref):
    s = s_ref[...]
    s = s * jax.nn.sigmoid(s)
    acc = jnp.dot(s.astype(BF16), w_ref[...].astype(BF16), preferred_element_type=F32)
    return acc + b_ref[...]


def _ada_kernel(s_ref, w_ref, b_ref, o_ref):
    o_ref[...] = _ada_columns(s_ref, w_ref, b_ref)


def _ada_mod(cc, ada_w, ada_b, n_cols, tn=512):
    rows, d = cc.shape
    return pl.pallas_call(
        _ada_kernel,
        grid=(n_cols // tn,),
        in_specs=[
            pl.BlockSpec((rows, d), lambda j: (0, 0)),
            pl.BlockSpec((d, tn), lambda j: (0, j)),
            pl.BlockSpec((1, tn), lambda j: (0, j)),
        ],
        out_specs=pl.BlockSpec((rows, tn), lambda j: (0, j)),
        out_shape=jax.ShapeDtypeStruct((rows, n_cols), F32),
        compiler_params=_cparams(("arbitrary",)),
        name="ada_mod",
    )(cc, ada_w, ada_b)


def _prenorm_kernel(x_ref, g_ref, mod_ref, h_ref, *, row_chunk):
    gain = g_ref[...] * (1.0 + mod_ref[0, 1:2, :])
    shift = mod_ref[0, 0:1, :]

    def body(r, carry):
        rows = pl.ds(pl.multiple_of(r * row_chunk, row_chunk), row_chunk)
        xc = x_ref[rows, :]
        var = jnp.mean(xc * xc, axis=-1, keepdims=True)
        h_ref[rows, :] = (xc * lax.rsqrt(var + NORM_EPS) * gain + shift).astype(h_ref.dtype)
        return carry

    lax.fori_loop(0, x_ref.shape[0] // row_chunk, body, 0, unroll=4)


def _prenorm(x2d, pre_g, mod3, *, tm, mod_row_fn, name):
    m, d = x2d.shape
    return pl.pallas_call(
        functools.partial(_prenorm_kernel, row_chunk=16),
        grid=(m // tm,),
        in_specs=[
            pl.BlockSpec((tm, d), lambda i: (i, 0)),
            pl.BlockSpec((1, d), lambda i: (0, 0)),
            pl.BlockSpec((1, 2, d), lambda i: (mod_row_fn(i), 0, 0)),
        ],
        out_specs=pl.BlockSpec((tm, d), lambda i: (i, 0)),
        out_shape=jax.ShapeDtypeStruct((m, d), BF16),
        compiler_params=_cparams(("arbitrary",)),
        name=name,
    )(x2d, pre_g, mod3)


def _inproj_kernel(h_ref, w_ref, o_ref):
    o_ref[...] = jnp.dot(h_ref[...], w_ref[...], preferred_element_type=F32).astype(o_ref.dtype)


def _inproj_gate_kernel(h_ref, w_ref, s_ref, aw_ref, ab_ref, o_ref, g_ref):
    o_ref[...] = jnp.dot(h_ref[...], w_ref[...], preferred_element_type=F32).astype(o_ref.dtype)
    g_ref[...] = _ada_columns(s_ref, aw_ref, ab_ref)


def _inproj_rope_kernel(h_ref, w_ref, cos_ref, sin_ref, *refs, q_tiles, k_tiles, head_dim):
    n_cast = len(refs) // 2
    o_ref = refs[n_cast]
    for src_ref, dst_ref in zip(refs[:n_cast], refs[n_cast + 1:]):
        dst_ref[...] = src_ref[...].astype(dst_ref.dtype)
    j = pl.program_id(1)
    is_rope = j < q_tiles + k_tiles
    scale = jnp.where(j >= q_tiles, head_dim ** -0.5, 1.0).astype(F32)
    half = V7X_LANES // 2
    groups_per_head = head_dim // V7X_LANES
    acc = jnp.dot(h_ref[...], w_ref[...], preferred_element_type=F32)
    for g in range(acc.shape[1] // V7X_LANES):
        lanes = slice(g * V7X_LANES, (g + 1) * V7X_LANES)
        tl = slice((g % groups_per_head) * V7X_LANES, (g % groups_per_head + 1) * V7X_LANES)
        t = acc[:, lanes]
        roped = (t * cos_ref[:, tl] + pltpu.roll(t, half, axis=1) * sin_ref[:, tl]) * scale
        o_ref[g // groups_per_head, :, tl] = jnp.where(is_rope, roped, t).astype(o_ref.dtype)


def _in_proj(h, w_bf16, *, tm, tn, col_block0, n_col_blocks, name, rope=None, side_casts=(), ada_gate=None):
    m, d = h.shape
    n_steps = (m // tm) * n_col_blocks
    in_specs = [
        pl.BlockSpec((tm, d), lambda i, j: (i, 0)),
        pl.BlockSpec((d, tn), lambda i, j: (0, j + col_block0)),
    ]
    args = [h, w_bf16]
    out_specs = [pl.BlockSpec((tm, tn), lambda i, j: (i, j))]
    out_shape = [jax.ShapeDtypeStruct((m, n_col_blocks * tn), BF16)]
    if rope is None and ada_gate is not None:
        assert not side_casts
        cc, ada_w, ada_b, col0, n_cols = ada_gate
        gw = n_cols // n_steps
        assert gw * n_steps == n_cols and gw % V7X_LANES == 0 and col0 % gw == 0
        step = lambda i, j: i * n_col_blocks + j
        in_specs += [
            pl.BlockSpec(cc.shape, lambda i, j: (0, 0)),
            pl.BlockSpec((ada_w.shape[0], gw), lambda i, j: (0, col0 // gw + step(i, j))),
            pl.BlockSpec((1, gw), lambda i, j: (0, col0 // gw + step(i, j))),
        ]
        args += [cc, ada_w, ada_b]
        out_specs.append(pl.BlockSpec((cc.shape[0], gw), lambda i, j: (0, step(i, j))))
        out_shape.append(jax.ShapeDtypeStruct((cc.shape[0], n_cols), F32))
        kern = _inproj_gate_kernel
    elif rope is None:
        assert not side_casts
        kern = _inproj_kernel
    else:
        assert ada_gate is None
        cos, sin, n_tokens, rope_cols = rope
        head_dim = cos.shape[1]
        tiles_per_seq = n_tokens // tm
        table = pl.BlockSpec((tm, head_dim), lambda i, j: (i % tiles_per_seq, 0))
        in_specs += [table, table]
        args += [cos, sin]
        heads_per_tile = tn // head_dim
        out_specs = [pl.BlockSpec((heads_per_tile, tm, head_dim), lambda i, j: (j, i, 0))]
        out_shape = [jax.ShapeDtypeStruct((n_col_blocks * heads_per_tile, m, head_dim), BF16)]
        for src, col_block, width in side_casts:
            rows = src.shape[0]
            assert rows % n_steps == 0 and (rows // n_steps) % V7X_SUBLANES_BF16 == 0
            slab = rows // n_steps
            in_specs.append(pl.BlockSpec((slab, width), lambda i, j, cb=col_block: (i * n_col_blocks + j, cb)))
            out_specs.append(pl.BlockSpec((slab, width), lambda i, j: (i * n_col_blocks + j, 0)))
            out_shape.append(jax.ShapeDtypeStruct((rows, width), BF16))
            args.append(src)
        kern = functools.partial(_inproj_rope_kernel, q_tiles=rope_cols // tn, k_tiles=rope_cols // tn,
                                 head_dim=head_dim)
    outs = pl.pallas_call(
        kern,
        grid=(m // tm, n_col_blocks),
        in_specs=in_specs,
        out_specs=out_specs,
        out_shape=out_shape,
        compiler_params=_cparams(("arbitrary", "arbitrary")),
        name=name,
    )(*args)
    return outs[0] if len(outs) == 1 else outs


def _rope_tables(n_tokens, head_dim):
    rows = n_tokens // GRID_W
    pos_r = np.repeat(np.arange(rows, dtype=np.float64), GRID_W)
    pos_c = np.tile(np.arange(GRID_W, dtype=np.float64), rows)
    n_freq = head_dim // 4
    inv_freq = ROPE_BASE ** (-np.arange(n_freq, dtype=np.float64) / n_freq)
    ang_r = pos_r[:, None] * inv_freq[None, :]
    ang_c = pos_c[:, None] * inv_freq[None, :]
    cos = np.concatenate([np.cos(ang_r), np.cos(ang_r), np.cos(ang_c), np.cos(ang_c)], axis=-1)
    sin = np.concatenate([-np.sin(ang_r), np.sin(ang_r), -np.sin(ang_c), np.sin(ang_c)], axis=-1)
    return jnp.asarray(cos, F32), jnp.asarray(sin, F32)


def _dot_tn(a, b):
    return lax.dot_general(a, b, (((0,), (0,)), ((), ())), preferred_element_type=F32)


def _dot_nt(a, b):
    return lax.dot_general(a, b, (((1,), (1,)), ((), ())), preferred_element_type=F32)


def _retention_kernel(lg_ref, q_ref, k_ref, v_ref, gt_ref, kc_ref, vc_ref, gng_ref,
                      o_ref, c0_ref, r_ref, st_ref, dm_ref, wt_ref, raw_ref, *, chunk, unroll):
    n, dh = q_ref.shape
    nc = n // chunk
    n_ctx = kc_ref.shape[0]
    head = pl.program_id(1)
    lgf = lg_ref[0, head]
    lgb = lg_ref[1, head]

    col = lax.broadcasted_iota(jnp.int32, (chunk, dh), 0).astype(F32)
    rel = (lax.broadcasted_iota(jnp.int32, (chunk, chunk), 0)
           - lax.broadcasted_iota(jnp.int32, (chunk, chunk), 1)).astype(F32)
    dm_ref[...] = (jnp.where(rel >= 0, jnp.exp(lgf * jnp.maximum(rel, 0.0)), 0.0)
                   + jnp.where(rel <= 0, jnp.exp(lgb * jnp.maximum(-rel, 0.0)), 0.0))
    wt_ref[0] = jnp.exp(lgf * (col + 1.0))
    wt_ref[1] = jnp.exp(lgb * (chunk - col))
    wt_ref[2] = jnp.exp(lgf * (chunk - 1.0 - col))
    wt_ref[3] = jnp.exp(lgb * col)
    dec = jnp.concatenate([jnp.exp(jnp.full((1, dh), lgf * chunk, F32)),
                           jnp.exp(jnp.full((1, dh), lgb * chunk, F32))], axis=1)

    def weighted_v(v, w_fwd, w_bwd):
        vf = v.astype(F32)
        return jnp.concatenate([vf * w_fwd, vf * w_bwd], axis=1).astype(BF16)

    pos_c = lax.broadcasted_iota(jnp.int32, (n_ctx, dh), 0).astype(F32)
    kc = (kc_ref[...].astype(F32) * dh ** -0.5).astype(BF16)
    c0_ref[...] = _dot_tn(kc, weighted_v(vc_ref[...], jnp.exp(lgf * (n_ctx - 1.0 - pos_c)), jnp.exp(lgb * pos_c)))
    st_ref[...] = c0_ref[...]

    def scan_body(t, carry):
        cf = t
        cb = nc - 1 - t
        rows_f = pl.ds(pl.multiple_of(cf * chunk, chunk), chunk)
        rows_b = pl.ds(pl.multiple_of(cb * chunk, chunk), chunk)
        state = st_ref[...]
        r_ref[cf, :, 0:dh] = state[:, 0:dh].astype(BF16)
        r_ref[cb, :, dh:2 * dh] = state[:, dh:2 * dh].astype(BF16)
        kv_f = _dot_tn(k_ref[rows_f, :], (v_ref[rows_f, :].astype(F32) * wt_ref[2]).astype(BF16))
        kv_b = _dot_tn(k_ref[rows_b, :], (v_ref[rows_b, :].astype(F32) * wt_ref[3]).astype(BF16))
        st_ref[...] = state * dec + jnp.concatenate([kv_f, kv_b], axis=1)
        return carry

    lax.fori_loop(0, nc, scan_body, 0, unroll=unroll)

    gng = gng_ref[...]

    def raw_out(c):
        rows = pl.ds(pl.multiple_of(c * chunk, chunk), chunk)
        q = q_ref[rows, :]
        scores = _dot_nt(q, k_ref[rows, :]) * dm_ref[...]
        o = jnp.dot(scores.astype(BF16), v_ref[rows, :], preferred_element_type=F32)
        x = jnp.dot(q, r_ref[c], preferred_element_type=F32)
        return o + x[:, 0:dh] * wt_ref[0] + x[:, dh:2 * dh] * wt_ref[1]

    def finish(c, o):
        rows = pl.ds(pl.multiple_of(c * chunk, chunk), chunk)
        mu = jnp.mean(o, axis=-1, keepdims=True)
        oc = o - mu
        var = jnp.mean(oc * oc, axis=-1, keepdims=True)
        y = oc * lax.rsqrt(var + NORM_EPS) * gng
        o_ref[rows, :] = (y * _silu(gt_ref[rows, :].astype(F32))).astype(o_ref.dtype)

    raw_ref[0] = raw_out(0)

    def out_body(c, carry):
        prev = raw_ref[(c - 1) % 2]
        raw_ref[c % 2] = raw_out(c)
        finish(c - 1, prev)
        return carry

    lax.fori_loop(1, nc, out_body, 0, unroll=unroll)
    finish(nc - 1, raw_ref[(nc - 1) % 2])


def _retention(lg, proj, kv_c, gn_g, *, batch, n, n_ctx, heads, dh):
    assert n % RET_CHUNK == 0 and proj.shape == (4 * heads, batch * n, dh)
    nc = n // RET_CHUNK
    kern = functools.partial(_retention_kernel, chunk=RET_CHUNK, unroll=2)
    return pl.pallas_call(
        kern,
        grid=(batch, heads),
        in_specs=[
            pl.BlockSpec(memory_space=pltpu.SMEM),
            pl.BlockSpec((None, n, dh), lambda b, h: (h, b, 0)),
            pl.BlockSpec((None, n, dh), lambda b, h: (heads + h, b, 0)),
            pl.BlockSpec((None, n, dh), lambda b, h: (2 * heads + h, b, 0)),
            pl.BlockSpec((None, n, dh), lambda b, h: (3 * heads + h, b, 0)),
            pl.BlockSpec((n_ctx, dh), lambda b, h: (b, h)),
            pl.BlockSpec((n_ctx, dh), lambda b, h: (b, heads + h)),
            pl.BlockSpec((1, dh), lambda b, h: (0, h)),
        ],
        out_specs=pl.BlockSpec((None, n, dh), lambda b, h: (h, b, 0)),
        out_shape=jax.ShapeDtypeStruct((heads, batch * n, dh), BF16),
        scratch_shapes=[
            pltpu.VMEM((dh, 2 * dh), F32),
            pltpu.VMEM((nc, dh, 2 * dh), BF16),
            pltpu.VMEM((dh, 2 * dh), F32),
            pltpu.VMEM((RET_CHUNK, RET_CHUNK), F32),
            pltpu.VMEM((4, RET_CHUNK, dh), F32),
            pltpu.VMEM((2, RET_CHUNK, dh), F32),
        ],
        compiler_params=_cparams(("arbitrary", "arbitrary")),
        name="retention",
    )(lg, proj, proj, proj, proj, kv_c, kv_c, gn_g)


def _conv_kernel(a_ref, b_ref, ap_ref, bp_ref, an_ref, bn_ref, gc_ref, w_ref, cb_ref, lg_ref, lb_ref,
                 o_ref, u_ref, y_ref, w3_ref, s1_ref, mu_ref, rs_ref, *, tiles_per_seq, row_chunk, conv_rows):
    tn, cw = a_ref.shape
    halo = ap_ref.shape[0]
    taps = w_ref.shape[0]
    pad = taps // 2
    n_groups = cw // V7X_LANES
    il = pl.program_id(0) % tiles_per_seq

    def glu(a, b):
        return _mul_sigmoid(a.astype(F32), b.astype(F32))

    def lane_group(val, g):
        return val[:, g * V7X_LANES:(g + 1) * V7X_LANES]

    u_prev = jnp.where(il > 0, glu(ap_ref[...], bp_ref[...]), 0.0)
    u_next = jnp.where(il < tiles_per_seq - 1, glu(an_ref[...], bn_ref[...]), 0.0)
    for g in range(n_groups):
        u_ref[g, 0:halo, :] = lane_group(u_prev, g)
        u_ref[g, halo + tn:2 * halo + tn, :] = lane_group(u_next, g)
        w3_ref[g, 0:taps, :] = lane_group(w_ref[...], g)
        w3_ref[g, taps:taps + 1, :] = lane_group(cb_ref[...], g)

    def glu_body(r, carry):
        r0 = pl.multiple_of(r * row_chunk, row_chunk)
        val = glu(a_ref[pl.ds(r0, row_chunk), :], b_ref[pl.ds(r0, row_chunk), :])
        for g in range(n_groups):
            u_ref[g, pl.ds(halo + r0, row_chunk), :] = lane_group(val, g)
        return carry

    lax.fori_loop(0, tn // row_chunk, glu_body, 0)

    s1_ref[...] = jnp.zeros_like(s1_ref)

    def conv_body(g, carry):
        for r0 in range(0, tn, conv_rows):
            acc = jnp.broadcast_to(w3_ref[g, taps:taps + 1, :], (conv_rows, V7X_LANES))
            for t in range(taps):
                s0 = r0 + halo - pad + t
                acc = acc + u_ref[g, s0:s0 + conv_rows, :] * w3_ref[g, t:t + 1, :]
            y_ref[g, r0:r0 + conv_rows, :] = acc
            s1_ref[r0:r0 + conv_rows, :] += acc
        return carry

    lax.fori_loop(0, n_groups, conv_body, 0)

    inv_cw = 1.0 / cw
    for r0 in range(0, tn, conv_rows):
        rows = slice(r0, r0 + conv_rows)
        mu = jnp.broadcast_to(jnp.sum(s1_ref[rows, :], axis=-1, keepdims=True) * inv_cw, (conv_rows, V7X_LANES))
        sq = jnp.zeros((conv_rows, V7X_LANES), F32)
        for g in range(n_groups):
            dlt = y_ref[g, rows, :] - mu
            sq = sq + dlt * dlt
        var = jnp.sum(sq, axis=-1, keepdims=True) * inv_cw
        mu_ref[rows, :] = mu
        rs_ref[rows, :] = jnp.broadcast_to(lax.rsqrt(var + NORM_EPS), (conv_rows, V7X_LANES))

    def row_body(r, carry):
        r0 = pl.multiple_of(r * row_chunk, row_chunk)
        rows = pl.ds(r0, row_chunk)
        mu = mu_ref[rows, :]
        rs = rs_ref[rows, :]
        for g in range(n_groups):
            lanes = slice(g * V7X_LANES, (g + 1) * V7X_LANES)
            z = (y_ref[g, rows, :] - mu) * rs * lg_ref[:, lanes] + lb_ref[:, lanes]
            gate = gc_ref[rows, lanes].astype(F32)
            o_ref[rows, lanes] = (_silu(z) * _silu(gate)).astype(o_ref.dtype)
        return carry

    lax.fori_loop(0, tn // row_chunk, row_body, 0)


def _conv_branch(glu_a, glu_b, gate, dw_w, dw_b, ln_g, ln_b, *, n, tn=512):
    m, cw = glu_a.shape
    halo = CONV_HALO
    assert n % tn == 0 and tn % halo == 0 and halo >= dw_w.shape[0] // 2 and cw % V7X_LANES == 0
    tiles_per_seq = n // tn
    hb = tn // halo
    last_hb = m // halo - 1
    n_groups = cw // V7X_LANES
    kern = functools.partial(_conv_kernel, tiles_per_seq=tiles_per_seq, row_chunk=16, conv_rows=64)
    main = pl.BlockSpec((tn, cw), lambda i: (i, 0))
    prev = pl.BlockSpec((halo, cw), lambda i: (jnp.maximum(i * hb - 1, 0), 0))
    nxt = pl.BlockSpec((halo, cw), lambda i: (jnp.minimum((i + 1) * hb, last_hb), 0))
    vec = lambda rows: pl.BlockSpec((rows, cw), lambda i: (0, 0))
    return pl.pallas_call(
        kern,
        grid=(m // tn,),
        in_specs=[main, main, prev, prev, nxt, nxt, main,
                  vec(dw_w.shape[0]), vec(1), vec(1), vec(1)],
        out_specs=pl.BlockSpec((tn, cw), lambda i: (i, 0)),
        out_shape=jax.ShapeDtypeStruct((m, cw), BF16),
        scratch_shapes=[
            pltpu.VMEM((n_groups, tn + 2 * halo, V7X_LANES), F32),
            pltpu.VMEM((n_groups, tn, V7X_LANES), F32),
            pltpu.VMEM((n_groups, dw_w.shape[0] + 1, V7X_LANES), F32),
            pltpu.VMEM((tn, V7X_LANES), F32),
            pltpu.VMEM((tn, V7X_LANES), F32),
            pltpu.VMEM((tn, V7X_LANES), F32),
        ],
        compiler_params=_cparams(("arbitrary",)),
        name="conv_branch",
    )(glu_a, glu_b, glu_a, glu_b, glu_a, glu_b, gate, dw_w, dw_b, ln_g, ln_b)


def _outproj_kernel(r_ref, c_ref, w1_ref, w2_ref, x_ref, gate_ref, pg_ref, o_ref, rs_ref, *, tn, row_chunk):
    tm, d = o_ref.shape
    ssq = jnp.zeros((tm, V7X_LANES), F32)
    ret = jnp.concatenate([r_ref[hd] for hd in range(r_ref.shape[0])], axis=1)
    for j0 in range(0, d, tn):
        cols = slice(j0, j0 + tn)
        y = jnp.dot(ret, w1_ref[:, cols], preferred_element_type=F32)
        y = y + jnp.dot(c_ref[...], w2_ref[:, cols], preferred_element_type=F32)
        o_ref[:, cols] = y
        for g0 in range(0, tn, V7X_LANES):
            yg = y[:, g0:g0 + V7X_LANES]
            ssq = ssq + yg * yg
    var = jnp.sum(ssq, axis=-1, keepdims=True) * (1.0 / d)
    rs_ref[...] = jnp.broadcast_to(lax.rsqrt(var + NORM_EPS), (tm, V7X_LANES))

    def body(r, carry):
        rows = pl.ds(pl.multiple_of(r * row_chunk, row_chunk), row_chunk)
        rs = rs_ref[rows, :]
        for g0 in range(0, d, V7X_LANES):
            lanes = slice(g0, g0 + V7X_LANES)
            gain = gate_ref[0, :, lanes] * pg_ref[:, lanes]
            o_ref[rows, lanes] = x_ref[rows, lanes] + o_ref[rows, lanes] * rs * gain
        return carry

    lax.fori_loop(0, tm // row_chunk, body, 0)


def _out_proj(ret_b, conv_b, w_out_b, x2d, gate3, post_g, *, tm, tiles_per_batch):
    m, d = x2d.shape
    heads, _, dh = ret_b.shape
    r = heads * dh
    cw = conv_b.shape[1]
    assert r == cw and w_out_b.shape == (r + cw, d)
    kern = functools.partial(_outproj_kernel, tn=512, row_chunk=16)
    return pl.pallas_call(
        kern,
        grid=(m // tm,),
        in_specs=[
            pl.BlockSpec((heads, tm, dh), lambda i: (0, i, 0)),
            pl.BlockSpec((tm, cw), lambda i: (i, 0)),
            pl.BlockSpec((r, d), lambda i: (0, 0), pipeline_mode=pl.Buffered(1)),
            pl.BlockSpec((cw, d), lambda i: (1, 0), pipeline_mode=pl.Buffered(1)),
            pl.BlockSpec((tm, d), lambda i: (i, 0)),
            pl.BlockSpec((1, 1, d), lambda i: (i // tiles_per_batch, 0, 0)),
            pl.BlockSpec((1, d), lambda i: (0, 0)),
        ],
        out_specs=pl.BlockSpec((tm, d), lambda i: (i, 0)),
        out_shape=jax.ShapeDtypeStruct((m, d), F32),
        scratch_shapes=[pltpu.VMEM((tm, V7X_LANES), F32)],
        compiler_params=_cparams(("arbitrary",)),
        name="out_proj",
    )(ret_b, conv_b, w_out_b, w_out_b, x2d, gate3, post_g)


def kernel(x, c, ctx, c_ctx, ada_w, ada_b, pre_norm_g, post_norm_g, w_in, ret_log_decay_fwd,
           ret_log_decay_bwd, ret_gn_g, conv_dw_w, conv_dw_b, conv_ln_g, conv_ln_b, w_out):
    batch, n, d = x.shape
    n_ctx = ctx.shape[1]
    depth = ada_w.shape[0]
    r = ret_gn_g.shape[1]
    cw = conv_dw_w.shape[2]
    heads = RET_HEADS
    dh = r // heads
    assert depth == 1, "single-layer block: the context stream is never updated"
    assert w_in.shape[2] == 4 * r + 3 * cw and r == cw

    cos, sin = _rope_tables(n, dh)
    x2d = x.reshape(batch * n, d)
    ctx2d = ctx.reshape(batch * n_ctx, d)

    mod_rows = 8
    cc = jnp.concatenate([c, c_ctx[None, :], jnp.zeros((mod_rows - batch - 1, d), F32)], axis=0)
    mod2 = _ada_mod(cc, ada_w[0], ada_b, 2 * d).reshape(mod_rows, 2, d)

    ret_cols = 4 * r
    w_ret_b = w_in[0, :, :ret_cols].astype(BF16)

    tm_norm = 512
    h = _prenorm(x2d, pre_norm_g, mod2, tm=tm_norm, mod_row_fn=lambda i: i // (n // tm_norm), name="prenorm")
    h_ctx = _prenorm(ctx2d, pre_norm_g, mod2, tm=tm_norm, mod_row_fn=lambda i: batch, name="prenorm_ctx")

    tm_in, tn_in = 1024, 1024
    side_casts = [(w_in[0], ret_cols // cw + k, cw) for k in range(3)] + [(w_out[0], 0, d)]
    proj, w_glu_a, w_glu_b, w_gate, w_out_b = _in_proj(
        h, w_ret_b, tm=tm_in, tn=tn_in, col_block0=0, n_col_blocks=ret_cols // tn_in, name="in_proj",
        rope=(cos, sin, n, r), side_casts=side_casts)
    kv_c = _in_proj(h_ctx, w_ret_b, tm=batch * n_ctx, tn=tn_in, col_block0=r // tn_in,
                    n_col_blocks=2 * r // tn_in, name="in_proj_ctx")
    glu_a, ada_gate = _in_proj(h, w_glu_a, tm=tm_in, tn=tn_in, col_block0=0, n_col_blocks=cw // tn_in,
                               name="in_proj_glu_a", ada_gate=(cc, ada_w[0], ada_b, 2 * d, d))
    glu_b, gate_c = [
        _in_proj(h, w_k, tm=tm_in, tn=tn_in, col_block0=0, n_col_blocks=cw // tn_in, name=name)
        for w_k, name in ((w_glu_b, "in_proj_glu_b"), (w_gate, "in_proj_conv_gate"))]

    lg = jnp.concatenate([ret_log_decay_fwd, ret_log_decay_bwd], axis=0)
    ret_b = _retention(lg, proj, kv_c, ret_gn_g, batch=batch, n=n, n_ctx=n_ctx, heads=heads, dh=dh)
    conv_b = _conv_branch(glu_a, glu_b, gate_c, conv_dw_w[0], conv_dw_b, conv_ln_g, conv_ln_b, n=n)

    tm_out = 256
    out = _out_proj(ret_b, conv_b, w_out_b, x2d, ada_gate.reshape(mod_rows, 1, d), post_norm_g,
                    tm=tm_out, tiles_per_batch=n // tm_out)
    return out.reshape(batch, n, d)
```

```python
import functools

import numpy as np
import jax
import jax.numpy as jnp
from jax import lax
from jax.experimental import pallas as pl
from jax.experimental.pallas import tpu as pltpu

F32 = jnp.float32
BF16 = jnp.bfloat16

GRID_W = 64
RET_HEADS = 8
ROPE_BASE = 10000.0
NORM_EPS = 1e-6

V7X_LANES = 128
V7X_SUBLANES_BF16 = 16
V7X_VMEM_LIMIT_BYTES = 58 * 1024 * 1024

RET_CHUNK = 256
CONV_HALO = 16


def _cparams(semantics):
    return pltpu.CompilerParams(dimension_semantics=semantics, vmem_limit_bytes=V7X_VMEM_LIMIT_BYTES)


def _silu(x):
    h = 0.5 * x
    return h + h * jnp.tanh(h)


def _mul_sigmoid(a, b):
    ha = 0.5 * a
    return ha + ha * jnp.tanh(0.5 * b)


def _ada_columns(s_ref, w_ref, b_ref):
    s = s_ref[...]
    s = s * jax.nn.sigmoid(s)
    acc = jnp.dot(s.astype(BF16), w_ref[...].astype(BF16), preferred_element_type=F32)
    return acc + b_ref[...]


def _ada_kernel(s_ref, w_ref, b_ref, o_ref):
    o_ref[...] = _ada_columns(s_ref, w_ref, b_ref)


def _ada_mod(cc, ada_w, ada_b, n_cols, tn=512):
    rows, d = cc.shape
    return pl.pallas_call(
        _ada_kernel,
        grid=(n_cols // tn,),
        in_specs=[
            pl.BlockSpec((rows, d), lambda j: (0, 0)),
            pl.BlockSpec((d, tn), lambda j: (0, j)),
            pl.BlockSpec((1, tn), lambda j: (0, j)),
        ],
        out_specs=pl.BlockSpec((rows, tn), lambda j: (0, j)),
        out_shape=jax.ShapeDtypeStruct((rows, n_cols), F32),
        compiler_params=_cparams(("arbitrary",)),
        name="ada_mod",
    )(cc, ada_w, ada_b)


def _prenorm_kernel(x_ref, g_ref, mod_ref, h_ref, *, row_chunk):
    gain = g_ref[...] * (1.0 + mod_ref[0, 1:2, :])
    shift = mod_ref[0, 0:1, :]

    def body(r, carry):
        rows = pl.ds(pl.multiple_of(r * row_chunk, row_chunk), row_chunk)
        xc = x_ref[rows, :]
        var = jnp.mean(xc * xc, axis=-1, keepdims=True)
        h_ref[rows, :] = (xc * lax.rsqrt(var + NORM_EPS) * gain + shift).astype(h_ref.dtype)
        return carry

    lax.fori_loop(0, x_ref.shape[0] // row_chunk, body, 0, unroll=4)


def _prenorm(x2d, pre_g, mod3, *, tm, mod_row_fn, name):
    m, d = x2d.shape
    return pl.pallas_call(
        functools.partial(_prenorm_kernel, row_chunk=16),
        grid=(m // tm,),
        in_specs=[
            pl.BlockSpec((tm, d), lambda i: (i, 0)),
            pl.BlockSpec((1, d), lambda i: (0, 0)),
            pl.BlockSpec((1, 2, d), lambda i: (mod_row_fn(i), 0, 0)),
        ],
        out_specs=pl.BlockSpec((tm, d), lambda i: (i, 0)),
        out_shape=jax.ShapeDtypeStruct((m, d), BF16),
        compiler_params=_cparams(("arbitrary",)),
        name=name,
    )(x2d, pre_g, mod3)


def _inproj_kernel(h_ref, w_ref, o_ref):
    o_ref[...] = jnp.dot(h_ref[...], w_ref[...], preferred_element_type=F32).astype(o_ref.dtype)


def _inproj_silu_kernel(h_ref, w_ref, o_ref):
    o_ref[...] = _silu(jnp.dot(h_ref[...], w_ref[...], preferred_element_type=F32)).astype(o_ref.dtype)


def _inproj_glu_kernel(h_ref, w_ref, s_ref, aw_ref, ab_ref, o_ref, g_ref):
    acc = jnp.dot(h_ref[...], w_ref[...], preferred_element_type=F32)
    for g in range(o_ref.shape[1] // V7X_LANES):
        a = acc[:, 2 * g * V7X_LANES:(2 * g + 1) * V7X_LANES]
        b = acc[:, (2 * g + 1) * V7X_LANES:(2 * g + 2) * V7X_LANES]
        o_ref[:, g * V7X_LANES:(g + 1) * V7X_LANES] = _mul_sigmoid(a, b).astype(o_ref.dtype)
    g_ref[...] = _ada_columns(s_ref, aw_ref, ab_ref)


def _inproj_rope_kernel(h_ref, w_ref, cos_ref, sin_ref, wa_ref, wb_ref, wg_ref, wo_ref,
                        o_ref, wab_ref, wgb_ref, wob_ref, *, q_tiles, k_tiles, head_dim):
    for g in range(wa_ref.shape[1] // V7X_LANES):
        src = slice(g * V7X_LANES, (g + 1) * V7X_LANES)
        wab_ref[:, 2 * g * V7X_LANES:(2 * g + 1) * V7X_LANES] = wa_ref[:, src].astype(wab_ref.dtype)
        wab_ref[:, (2 * g + 1) * V7X_LANES:(2 * g + 2) * V7X_LANES] = wb_ref[:, src].astype(wab_ref.dtype)
    wgb_ref[...] = wg_ref[...].astype(wgb_ref.dtype)
    wob_ref[...] = wo_ref[...].astype(wob_ref.dtype)
    j = pl.program_id(1)
    is_rope = j < q_tiles + k_tiles
    scale = jnp.where(j >= q_tiles, head_dim ** -0.5, 1.0).astype(F32)
    half = V7X_LANES // 2
    groups_per_head = head_dim // V7X_LANES
    acc = jnp.dot(h_ref[...], w_ref[...], preferred_element_type=F32)
    for g in range(acc.shape[1] // V7X_LANES):
        lanes = slice(g * V7X_LANES, (g + 1) * V7X_LANES)
        tl = slice((g % groups_per_head) * V7X_LANES, (g % groups_per_head + 1) * V7X_LANES)
        t = acc[:, lanes]
        roped = (t * cos_ref[:, tl] + pltpu.roll(t, half, axis=1) * sin_ref[:, tl]) * scale
        o_ref[g // groups_per_head, :, tl] = jnp.where(is_rope, roped, t).astype(o_ref.dtype)


def _in_proj(h, w_bf16, *, tm, tn, col_block0, n_col_blocks, name, rope=None, side_casts=None, glu_gate=None,
             silu=False):
    m, d = h.shape
    n_steps = (m // tm) * n_col_blocks
    in_specs = [
        pl.BlockSpec((tm, d), lambda i, j: (i, 0)),
        pl.BlockSpec((d, tn), lambda i, j: (0, j + col_block0)),
    ]
    args = [h, w_bf16]
    out_specs = [pl.BlockSpec((tm, tn), lambda i, j: (i, j))]
    out_shape = [jax.ShapeDtypeStruct((m, n_col_blocks * tn), BF16)]
    if rope is None and glu_gate is not None:
        assert side_casts is None and not silu
        cc, ada_w, ada_b, col0, n_cols = glu_gate
        gw = n_cols // n_steps
        assert gw * n_steps == n_cols and gw % V7X_LANES == 0 and col0 % gw == 0
        step = lambda i, j: i * n_col_blocks + j
        in_specs += [
            pl.BlockSpec(cc.shape, lambda i, j: (0, 0)),
            pl.BlockSpec((ada_w.shape[0], gw), lambda i, j: (0, col0 // gw + step(i, j))),
            pl.BlockSpec((1, gw), lambda i, j: (0, col0 // gw + step(i, j))),
        ]
        args += [cc, ada_w, ada_b]
        out_specs = [pl.BlockSpec((tm, tn // 2), lambda i, j: (i, j)),
                     pl.BlockSpec((cc.shape[0], gw), lambda i, j: (0, step(i, j)))]
        out_shape = [jax.ShapeDtypeStruct((m, n_col_blocks * tn // 2), BF16),
                     jax.ShapeDtypeStruct((cc.shape[0], n_cols), F32)]
        kern = _inproj_glu_kernel
    elif rope is None:
        assert side_casts is None
        kern = _inproj_silu_kernel if silu else _inproj_kernel
    else:
        assert glu_gate is None and not silu and side_casts is not None
        cos, sin, n_tokens, rope_cols = rope
        head_dim = cos.shape[1]
        tiles_per_seq = n_tokens // tm
        table = pl.BlockSpec((tm, head_dim), lambda i, j: (i % tiles_per_seq, 0))
        in_specs += [table, table]
        args += [cos, sin]
        heads_per_tile = tn // head_dim
        out_specs = [pl.BlockSpec((heads_per_tile, tm, head_dim), lambda i, j: (j, i, 0))]
        out_shape = [jax.ShapeDtypeStruct((n_col_blocks * heads_per_tile, m, head_dim), BF16)]
        w_in_f, (cb_a, cb_b, cb_g), width, w_out_f = side_casts
        rows = w_in_f.shape[0]
        assert w_out_f.shape[0] == rows and rows % n_steps == 0 and (rows // n_steps) % V7X_SUBLANES_BF16 == 0
        slab = rows // n_steps
        step_row = lambda i, j: i * n_col_blocks + j
        for cb in (cb_a, cb_b, cb_g):
            in_specs.append(pl.BlockSpec((slab, width), lambda i, j, cb=cb: (step_row(i, j), cb)))
            args.append(w_in_f)
        in_specs.append(pl.BlockSpec((slab, w_out_f.shape[1]), lambda i, j: (step_row(i, j), 0)))
        args.append(w_out_f)
        for cols in (2 * width, width, w_out_f.shape[1]):
            out_specs.append(pl.BlockSpec((slab, cols), lambda i, j: (step_row(i, j), 0)))
            out_shape.append(jax.ShapeDtypeStruct((rows, cols), BF16))
        kern = functools.partial(_inproj_rope_kernel, q_tiles=rope_cols // tn, k_tiles=rope_cols // tn,
                                 head_dim=head_dim)
    outs = pl.pallas_call(
        kern,
        grid=(m // tm, n_col_blocks),
        in_specs=in_specs,
        out_specs=out_specs,
        out_shape=out_shape,
        compiler_params=_cparams(("arbitrary", "arbitrary")),
        name=name,
    )(*args)
    return outs[0] if len(outs) == 1 else outs


def _rope_tables(n_tokens, head_dim):
    rows = n_tokens // GRID_W
    pos_r = np.repeat(np.arange(rows, dtype=np.float64), GRID_W)
    pos_c = np.tile(np.arange(GRID_W, dtype=np.float64), rows)
    n_freq = head_dim // 4
    inv_freq = ROPE_BASE ** (-np.arange(n_freq, dtype=np.float64) / n_freq)
    ang_r = pos_r[:, None] * inv_freq[None, :]
    ang_c = pos_c[:, None] * inv_freq[None, :]
    cos = np.concatenate([np.cos(ang_r), np.cos(ang_r), np.cos(ang_c), np.cos(ang_c)], axis=-1)
    sin = np.concatenate([-np.sin(ang_r), np.sin(ang_r), -np.sin(ang_c), np.sin(ang_c)], axis=-1)
    return jnp.asarray(cos, F32), jnp.asarray(sin, F32)


def _dot_tn(a, b):
    return lax.dot_general(a, b, (((0,), (0,)), ((), ())), preferred_element_type=F32)


def _dot_nt(a, b):
    return lax.dot_general(a, b, (((1,), (1,)), ((), ())), preferred_element_type=F32)


def _retention_kernel(lg_ref, q_ref, k_ref, v_ref, gt_ref, kc_ref, vc_ref, gng_ref,
                      o_ref, c0_ref, r_ref, st_ref, dm_ref, wt_ref, raw_ref, *, chunk, unroll):
    n, dh = q_ref.shape
    nc = n // chunk
    n_ctx = kc_ref.shape[0]
    head = pl.program_id(1)
    lgf = lg_ref[0, head]
    lgb = lg_ref[1, head]

    col = lax.broadcasted_iota(jnp.int32, (chunk, dh), 0).astype(F32)
    rel = (lax.broadcasted_iota(jnp.int32, (chunk, chunk), 0)
           - lax.broadcasted_iota(jnp.int32, (chunk, chunk), 1)).astype(F32)
    dm_ref[...] = (jnp.where(rel >= 0, jnp.exp(lgf * jnp.maximum(rel, 0.0)), 0.0)
                   + jnp.where(rel <= 0, jnp.exp(lgb * jnp.maximum(-rel, 0.0)), 0.0))
    wt_ref[0] = jnp.exp(lgf * (col + 1.0))
    wt_ref[1] = jnp.exp(lgb * (chunk - col))
    wt_ref[2] = jnp.exp(lgf * (chunk - 1.0 - col))
    wt_ref[3] = jnp.exp(lgb * col)
    dec = jnp.concatenate([jnp.exp(jnp.full((1, dh), lgf * chunk, F32)),
                           jnp.exp(jnp.full((1, dh), lgb * chunk, F32))], axis=1)

    def weighted_v(v, w_fwd, w_bwd):
        vf = v.astype(F32)
        return jnp.concatenate([vf * w_fwd, vf * w_bwd], axis=1).astype(BF16)

    pos_c = lax.broadcasted_iota(jnp.int32, (n_ctx, dh), 0).astype(F32)
    kc = (kc_ref[...].astype(F32) * dh ** -0.5).astype(BF16)
    c0_ref[...] = _dot_tn(kc, weighted_v(vc_ref[...], jnp.exp(lgf * (n_ctx - 1.0 - pos_c)), jnp.exp(lgb * pos_c)))
    st_ref[...] = c0_ref[...]

    def scan_body(t, carry):
        cf = t
        cb = nc - 1 - t
        rows_f = pl.ds(pl.multiple_of(cf * chunk, chunk), chunk)
        rows_b = pl.ds(pl.multiple_of(cb * chunk, chunk), chunk)
        state = st_ref[...]
        r_ref[cf, :, 0:dh] = state[:, 0:dh].astype(BF16)
        r_ref[cb, :, dh:2 * dh] = state[:, dh:2 * dh].astype(BF16)
        kv_f = _dot_tn(k_ref[rows_f, :], (v_ref[rows_f, :].astype(F32) * wt_ref[2]).astype(BF16))
        kv_b = _dot_tn(k_ref[rows_b, :], (v_ref[rows_b, :].astype(F32) * wt_ref[3]).astype(BF16))
        st_ref[...] = state * dec + jnp.concatenate([kv_f, kv_b], axis=1)
        return carry

    lax.fori_loop(0, nc, scan_body, 0, unroll=unroll)

    gng = gng_ref[...]

    def raw_out(c):
        rows = pl.ds(pl.multiple_of(c * chunk, chunk), chunk)
        q = q_ref[rows, :]
        scores = _dot_nt(q, k_ref[rows, :]) * dm_ref[...]
        o = jnp.dot(scores.astype(BF16), v_ref[rows, :], preferred_element_type=F32)
        x = jnp.dot(q, r_ref[c], preferred_element_type=F32)
        return o + x[:, 0:dh] * wt_ref[0] + x[:, dh:2 * dh] * wt_ref[1]

    def finish(c, o):
        rows = pl.ds(pl.multiple_of(c * chunk, chunk), chunk)
        mu = jnp.mean(o, axis=-1, keepdims=True)
        oc = o - mu
        var = jnp.mean(oc * oc, axis=-1, keepdims=True)
        y = oc * lax.rsqrt(var + NORM_EPS) * gng
        o_ref[rows, :] = (y * _silu(gt_ref[rows, :].astype(F32))).astype(o_ref.dtype)

    raw_ref[0] = raw_out(0)

    def out_body(c, carry):
        prev = raw_ref[(c - 1) % 2]
        raw_ref[c % 2] = raw_out(c)
        finish(c - 1, prev)
        return carry

    lax.fori_loop(1, nc, out_body, 0, unroll=unroll)
    finish(nc - 1, raw_ref[(nc - 1) % 2])


def _retention(lg, proj, kv_c, gn_g, *, batch, n, n_ctx, heads, dh):
    assert n % RET_CHUNK == 0 and proj.shape == (4 * heads, batch * n, dh)
    nc = n // RET_CHUNK
    kern = functools.partial(_retention_kernel, chunk=RET_CHUNK, unroll=2)
    return pl.pallas_call(
        kern,
        grid=(batch, heads),
        in_specs=[
            pl.BlockSpec(memory_space=pltpu.SMEM),
            pl.BlockSpec((None, n, dh), lambda b, h: (h, b, 0)),
            pl.BlockSpec((None, n, dh), lambda b, h: (heads + h, b, 0)),
            pl.BlockSpec((None, n, dh), lambda b, h: (2 * heads + h, b, 0)),
            pl.BlockSpec((None, n, dh), lambda b, h: (3 * heads + h, b, 0)),
            pl.BlockSpec((n_ctx, dh), lambda b, h: (b, h)),
            pl.BlockSpec((n_ctx, dh), lambda b, h: (b, heads + h)),
            pl.BlockSpec((1, dh), lambda b, h: (0, h)),
        ],
        out_specs=pl.BlockSpec((None, n, dh), lambda b, h: (h, b, 0)),
        out_shape=jax.ShapeDtypeStruct((heads, batch * n, dh), BF16),
        scratch_shapes=[
            pltpu.VMEM((dh, 2 * dh), F32),
            pltpu.VMEM((nc, dh, 2 * dh), BF16),
            pltpu.VMEM((dh, 2 * dh), F32),
            pltpu.VMEM((RET_CHUNK, RET_CHUNK), F32),
            pltpu.VMEM((4, RET_CHUNK, dh), F32),
            pltpu.VMEM((2, RET_CHUNK, dh), F32),
        ],
        compiler_params=_cparams(("arbitrary", "arbitrary")),
        name="retention",
    )(lg, proj, proj, proj, proj, kv_c, kv_c, gn_g)


def _conv_kernel(uin_ref, up_ref, un_ref, gc_ref, w_ref, cb_ref, lg_ref, lb_ref,
                 o_ref, u_ref, y_ref, w3_ref, s1_ref, mu_ref, rs_ref, *, tiles_per_seq, row_chunk, conv_rows):
    tn, cw = uin_ref.shape
    halo = up_ref.shape[0]
    taps = w_ref.shape[0]
    pad = taps // 2
    n_groups = cw // V7X_LANES
    il = pl.program_id(0) % tiles_per_seq

    def lane_group(val, g):
        return val[:, g * V7X_LANES:(g + 1) * V7X_LANES]

    u_prev = jnp.where(il > 0, up_ref[...].astype(F32), 0.0)
    u_next = jnp.where(il < tiles_per_seq - 1, un_ref[...].astype(F32), 0.0)
    for g in range(n_groups):
        u_ref[g, 0:halo, :] = lane_group(u_prev, g)
        u_ref[g, halo + tn:2 * halo + tn, :] = lane_group(u_next, g)
        w3_ref[g, 0:taps, :] = lane_group(w_ref[...], g)
        w3_ref[g, taps:taps + 1, :] = lane_group(cb_ref[...], g)

    def copy_body(r, carry):
        r0 = pl.multiple_of(r * row_chunk, row_chunk)
        val = uin_ref[pl.ds(r0, row_chunk), :].astype(F32)
        for g in range(n_groups):
            u_ref[g, pl.ds(halo + r0, row_chunk), :] = lane_group(val, g)
        return carry

    lax.fori_loop(0, tn // row_chunk, copy_body, 0)

    s1_ref[...] = jnp.zeros_like(s1_ref)

    def conv_body(g, carry):
        for r0 in range(0, tn, conv_rows):
            acc = jnp.broadcast_to(w3_ref[g, taps:taps + 1, :], (conv_rows, V7X_LANES))
            for t in range(taps):
                s0 = r0 + halo - pad + t
                acc = acc + u_ref[g, s0:s0 + conv_rows, :] * w3_ref[g, t:t + 1, :]
            y_ref[g, r0:r0 + conv_rows, :] = acc
            s1_ref[r0:r0 + conv_rows, :] += acc
        return carry

    lax.fori_loop(0, n_groups, conv_body, 0)

    inv_cw = 1.0 / cw
    for r0 in range(0, tn, conv_rows):
        rows = slice(r0, r0 + conv_rows)
        mu = jnp.broadcast_to(jnp.sum(s1_ref[rows, :], axis=-1, keepdims=True) * inv_cw, (conv_rows, V7X_LANES))
        sq = jnp.zeros((conv_rows, V7X_LANES), F32)
        for g in range(n_groups):
            dlt = y_ref[g, rows, :] - mu
            sq = sq + dlt * dlt
        var = jnp.sum(sq, axis=-1, keepdims=True) * inv_cw
        mu_ref[rows, :] = mu
        rs_ref[rows, :] = jnp.broadcast_to(lax.rsqrt(var + NORM_EPS), (conv_rows, V7X_LANES))

    def row_body(r, carry):
        r0 = pl.multiple_of(r * row_chunk, row_chunk)
        rows = pl.ds(r0, row_chunk)
        mu = mu_ref[rows, :]
        rs = rs_ref[rows, :]
        for g in range(n_groups):
            lanes = slice(g * V7X_LANES, (g + 1) * V7X_LANES)
            z = (y_ref[g, rows, :] - mu) * rs * lg_ref[:, lanes] + lb_ref[:, lanes]
            o_ref[rows, lanes] = (_silu(z) * gc_ref[rows, lanes].astype(F32)).astype(o_ref.dtype)
        return carry

    lax.fori_loop(0, tn // row_chunk, row_body, 0)


def _conv_branch(glu, gate, dw_w, dw_b, ln_g, ln_b, *, n, tn=512):
    m, cw = glu.shape
    halo = CONV_HALO
    assert n % tn == 0 and tn % halo == 0 and halo >= dw_w.shape[0] // 2 and cw % V7X_LANES == 0
    tiles_per_seq = n // tn
    hb = tn // halo
    last_hb = m // halo - 1
    n_groups = cw // V7X_LANES
    kern = functools.partial(_conv_kernel, tiles_per_seq=tiles_per_seq, row_chunk=16, conv_rows=64)
    main = pl.BlockSpec((tn, cw), lambda i: (i, 0))
    prev = pl.BlockSpec((halo, cw), lambda i: (jnp.maximum(i * hb - 1, 0), 0))
    nxt = pl.BlockSpec((halo, cw), lambda i: (jnp.minimum((i + 1) * hb, last_hb), 0))
    vec = lambda rows: pl.BlockSpec((rows, cw), lambda i: (0, 0))
    return pl.pallas_call(
        kern,
        grid=(m // tn,),
        in_specs=[main, prev, nxt, main, vec(dw_w.shape[0]), vec(1), vec(1), vec(1)],
        out_specs=pl.BlockSpec((tn, cw), lambda i: (i, 0)),
        out_shape=jax.ShapeDtypeStruct((m, cw), BF16),
        scratch_shapes=[
            pltpu.VMEM((n_groups, tn + 2 * halo, V7X_LANES), F32),
            pltpu.VMEM((n_groups, tn, V7X_LANES), F32),
            pltpu.VMEM((n_groups, dw_w.shape[0] + 1, V7X_LANES), F32),
            pltpu.VMEM((tn, V7X_LANES), F32),
            pltpu.VMEM((tn, V7X_LANES), F32),
            pltpu.VMEM((tn, V7X_LANES), F32),
        ],
        compiler_params=_cparams(("arbitrary",)),
        name="conv_branch",
    )(glu, glu, glu, gate, dw_w, dw_b, ln_g, ln_b)


def _outproj_kernel(r_ref, c_ref, w1_ref, w2_ref, x_ref, gate_ref, pg_ref, o_ref, rs_ref, *, tn, row_chunk):
    tm, d = o_ref.shape
    ssq = jnp.zeros((tm, V7X_LANES), F32)
    ret = jnp.concatenate([r_ref[hd] for hd in range(r_ref.shape[0])], axis=1)
    for j0 in range(0, d, tn):
        cols = slice(j0, j0 + tn)
        y = jnp.dot(ret, w1_ref[:, cols], preferred_element_type=F32)
        y = y + jnp.dot(c_ref[...], w2_ref[:, cols], preferred_element_type=F32)
        o_ref[:, cols] = y
        for g0 in range(0, tn, V7X_LANES):
            yg = y[:, g0:g0 + V7X_LANES]
            ssq = ssq + yg * yg
    var = jnp.sum(ssq, axis=-1, keepdims=True) * (1.0 / d)
    rs_ref[...] = jnp.broadcast_to(lax.rsqrt(var + NORM_EPS), (tm, V7X_LANES))

    def body(r, carry):
        rows = pl.ds(pl.multiple_of(r * row_chunk, row_chunk), row_chunk)
        rs = rs_ref[rows, :]
        for g0 in range(0, d, V7X_LANES):
            lanes = slice(g0, g0 + V7X_LANES)
            gain = gate_ref[0, :, lanes] * pg_ref[:, lanes]
            o_ref[rows, lanes] = x_ref[rows, lanes] + o_ref[rows, lanes] * rs * gain
        return carry

    lax.fori_loop(0, tm // row_chunk, body, 0)


def _out_proj(ret_b, conv_b, w_out_b, x2d, gate3, post_g, *, tm, tiles_per_batch):
    m, d = x2d.shape
    heads, _, dh = ret_b.shape
    r = heads * dh
    cw = conv_b.shape[1]
    assert r == cw and w_out_b.shape == (r + cw, d)
    kern = functools.partial(_outproj_kernel, tn=512, row_chunk=16)
    return pl.pallas_call(
        kern,
        grid=(m // tm,),
        in_specs=[
            pl.BlockSpec((heads, tm, dh), lambda i: (0, i, 0)),
            pl.BlockSpec((tm, cw), lambda i: (i, 0)),
            pl.BlockSpec((r, d), lambda i: (0, 0), pipeline_mode=pl.Buffered(1)),
            pl.BlockSpec((cw, d), lambda i: (1, 0), pipeline_mode=pl.Buffered(1)),
            pl.BlockSpec((tm, d), lambda i: (i, 0)),
            pl.BlockSpec((1, 1, d), lambda i: (i // tiles_per_batch, 0, 0)),
            pl.BlockSpec((1, d), lambda i: (0, 0)),
        ],
        out_specs=pl.BlockSpec((tm, d), lambda i: (i, 0)),
        out_shape=jax.ShapeDtypeStruct((m, d), F32),
        scratch_shapes=[pltpu.VMEM((tm, V7X_LANES), F32)],
        compiler_params=_cparams(("arbitrary",)),
        name="out_proj",
    )(ret_b, conv_b, w_out_b, w_out_b, x2d, gate3, post_g)


def kernel(x, c, ctx, c_ctx, ada_w, ada_b, pre_norm_g, post_norm_g, w_in, ret_log_decay_fwd,
           ret_log_decay_bwd, ret_gn_g, conv_dw_w, conv_dw_b, conv_ln_g, conv_ln_b, w_out):
    batch, n, d = x.shape
    n_ctx = ctx.shape[1]
    depth = ada_w.shape[0]
    r = ret_gn_g.shape[1]
    cw = conv_dw_w.shape[2]
    heads = RET_HEADS
    dh = r // heads
    assert depth == 1, "single-layer block: the context stream is never updated"
    assert w_in.shape[2] == 4 * r + 3 * cw and r == cw

    cos, sin = _rope_tables(n, dh)
    x2d = x.reshape(batch * n, d)
    ctx2d = ctx.reshape(batch * n_ctx, d)

    mod_rows = 8
    cc = jnp.concatenate([c, c_ctx[None, :], jnp.zeros((mod_rows - batch - 1, d), F32)], axis=0)
    mod2 = _ada_mod(cc, ada_w[0], ada_b, 2 * d).reshape(mod_rows, 2, d)

    ret_cols = 4 * r
    w_ret_b = w_in[0, :, :ret_cols].astype(BF16)

    tm_norm = 512
    h = _prenorm(x2d, pre_norm_g, mod2, tm=tm_norm, mod_row_fn=lambda i: i // (n // tm_norm), name="prenorm")
    h_ctx = _prenorm(ctx2d, pre_norm_g, mod2, tm=tm_norm, mod_row_fn=lambda i: batch, name="prenorm_ctx")

    tm_in, tn_in = 1024, 1024
    cb0 = ret_cols // cw
    proj, w_glu_b16, w_gate_b16, w_out_b = _in_proj(
        h, w_ret_b, tm=tm_in, tn=tn_in, col_block0=0, n_col_blocks=ret_cols // tn_in, name="in_proj",
        rope=(cos, sin, n, r), side_casts=(w_in[0], (cb0, cb0 + 1, cb0 + 2), cw, w_out[0]))
    kv_c = _in_proj(h_ctx, w_ret_b, tm=batch * n_ctx, tn=tn_in, col_block0=r // tn_in,
                    n_col_blocks=2 * r // tn_in, name="in_proj_ctx")
    glu, ada_gate = _in_proj(h, w_glu_b16, tm=tm_in, tn=tn_in, col_block0=0, n_col_blocks=2 * cw // tn_in,
                             name="in_proj_glu", glu_gate=(cc, ada_w[0], ada_b, 2 * d, d))
    gate_c = _in_proj(h, w_gate_b16, tm=tm_in, tn=tn_in, col_block0=0, n_col_blocks=cw // tn_in,
                      name="in_proj_conv_gate", silu=True)

    lg = jnp.concatenate([ret_log_decay_fwd, ret_log_decay_bwd], axis=0)
    ret_b = _retention(lg, proj, kv_c, ret_gn_g, batch=batch, n=n, n_ctx=n_ctx, heads=heads, dh=dh)
    conv_b = _conv_branch(glu, gate_c, conv_dw_w[0], conv_dw_b, conv_ln_g, conv_ln_b, n=n)

    tm_out = 256
    out = _out_proj(ret_b, conv_b, w_out_b, x2d, ada_gate.reshape(mod_rows, 1, d), post_norm_g,
                    tm=tm_out, tiles_per_batch=n // tm_out)
    return out.reshape(batch, n, d)
```

```python
import functools

import numpy as np
import jax
import jax.numpy as jnp
from jax import lax
from jax.experimental import pallas as pl
from jax.experimental.pallas import tpu as pltpu

F32 = jnp.float32
BF16 = jnp.bfloat16

GRID_W = 64
RET_HEADS = 8
ROPE_BASE = 10000.0
NORM_EPS = 1e-6

V7X_LANES = 128
V7X_SUBLANES_BF16 = 16
V7X_VMEM_LIMIT_BYTES = 58 * 1024 * 1024

RET_CHUNK = 256
CONV_HALO = 16


def _cparams(semantics):
    return pltpu.CompilerParams(dimension_semantics=semantics, vmem_limit_bytes=V7X_VMEM_LIMIT_BYTES)


def _silu(x):
    h = 0.5 * x
    return h + h * jnp.tanh(h)


def _mul_sigmoid(a, b):
    ha = 0.5 * a
    return ha + ha * jnp.tanh(0.5 * b)


def _ada_columns(s_ref, w_ref, b_ref):
    s = s_ref[...]
    s = s * jax.nn.sigmoid(s)
    acc = jnp.dot(s.astype(BF16), w_ref[...].astype(BF16), preferred_element_type=F32)
    return acc + b_ref[...]


def _ada_kernel(s_ref, w_ref, b_ref, o_ref):
    o_ref[...] = _ada_columns(s_ref, w_ref, b_ref)


def _ada_mod(cc, ada_w, ada_b, n_cols, tn=512):
    rows, d = cc.shape
    return pl.pallas_call(
        _ada_kernel,
        grid=(n_cols // tn,),
        in_specs=[
            pl.BlockSpec((rows, d), lambda j: (0, 0)),
            pl.BlockSpec((d, tn), lambda j: (0, j)),
            pl.BlockSpec((1, tn), lambda j: (0, j)),
        ],
        out_specs=pl.BlockSpec((rows, tn), lambda j: (0, j)),
        out_shape=jax.ShapeDtypeStruct((rows, n_cols), F32),
        compiler_params=_cparams(("arbitrary",)),
        name="ada_mod",
    )(cc, ada_w, ada_b)


def _prenorm_kernel(x_ref, g_ref, mod_ref, h_ref, *, row_chunk):
    gain = g_ref[...] * (1.0 + mod_ref[0, 1:2, :])
    shift = mod_ref[0, 0:1, :]

    def body(r, carry):
        rows = pl.ds(pl.multiple_of(r * row_chunk, row_chunk), row_chunk)
        xc = x_ref[rows, :]
        var = jnp.mean(xc * xc, axis=-1, keepdims=True)
        h_ref[rows, :] = (xc * lax.rsqrt(var + NORM_EPS) * gain + shift).astype(h_ref.dtype)
        return carry

    lax.fori_loop(0, x_ref.shape[0] // row_chunk, body, 0, unroll=4)


PRENORM_RING_DEPTH = 3


def _prenorm_ring_kernel(x_hbm, g_ref, mod_ref, h_ref, xbuf, sem, *, row_chunk, n_steps):
    s = pl.program_id(0)
    tm = h_ref.shape[0]
    depth = xbuf.shape[0]

    def tile_copy(step, slot):
        rows = pl.ds(pl.multiple_of(step * tm, tm), tm)
        return pltpu.make_async_copy(x_hbm.at[rows, :], xbuf.at[slot], sem.at[slot])

    @pl.when(s == 0)
    def _():
        for k in range(depth - 1):
            tile_copy(k, k).start()

    @pl.when(s + (depth - 1) < n_steps)
    def _():
        nxt = s + (depth - 1)
        tile_copy(nxt, nxt % depth).start()

    slot = s % depth
    tile_copy(s, slot).wait()

    gain = g_ref[...] * (1.0 + mod_ref[0, 1:2, :])
    shift = mod_ref[0, 0:1, :]

    def body(r, carry):
        rows = pl.ds(pl.multiple_of(r * row_chunk, row_chunk), row_chunk)
        xc = xbuf[slot, rows, :]
        var = jnp.mean(xc * xc, axis=-1, keepdims=True)
        h_ref[rows, :] = (xc * lax.rsqrt(var + NORM_EPS) * gain + shift).astype(h_ref.dtype)
        return carry

    lax.fori_loop(0, tm // row_chunk, body, 0, unroll=4)


def _prenorm_ring(x2d, pre_g, mod3, *, tm, mod_row_fn, name):
    m, d = x2d.shape
    n_steps = m // tm
    assert m % tm == 0 and n_steps >= PRENORM_RING_DEPTH - 1
    return pl.pallas_call(
        functools.partial(_prenorm_ring_kernel, row_chunk=16, n_steps=n_steps),
        grid=(n_steps,),
        in_specs=[
            pl.BlockSpec(memory_space=pl.ANY),
            pl.BlockSpec((1, d), lambda i: (0, 0)),
            pl.BlockSpec((1, 2, d), lambda i: (mod_row_fn(i), 0, 0)),
        ],
        out_specs=pl.BlockSpec((tm, d), lambda i: (i, 0)),
        out_shape=jax.ShapeDtypeStruct((m, d), BF16),
        scratch_shapes=[pltpu.VMEM((PRENORM_RING_DEPTH, tm, d), F32),
                        pltpu.SemaphoreType.DMA((PRENORM_RING_DEPTH,))],
        compiler_params=_cparams(("arbitrary",)),
        name=name,
    )(x2d, pre_g, mod3)


def _prenorm(x2d, pre_g, mod3, *, tm, mod_row_fn, name):
    m, d = x2d.shape
    return pl.pallas_call(
        functools.partial(_prenorm_kernel, row_chunk=16),
        grid=(m // tm,),
        in_specs=[
            pl.BlockSpec((tm, d), lambda i: (i, 0)),
            pl.BlockSpec((1, d), lambda i: (0, 0)),
            pl.BlockSpec((1, 2, d), lambda i: (mod_row_fn(i), 0, 0)),
        ],
        out_specs=pl.BlockSpec((tm, d), lambda i: (i, 0)),
        out_shape=jax.ShapeDtypeStruct((m, d), BF16),
        compiler_params=_cparams(("arbitrary",)),
        name=name,
    )(x2d, pre_g, mod3)


def _inproj_kernel(h_ref, w_ref, o_ref):
    o_ref[...] = jnp.dot(h_ref[...], w_ref[...], preferred_element_type=F32).astype(o_ref.dtype)


def _inproj_silu_kernel(h_ref, w_ref, o_ref):
    o_ref[...] = _silu(jnp.dot(h_ref[...], w_ref[...], preferred_element_type=F32)).astype(o_ref.dtype)


def _inproj_glu_kernel(h_ref, w_ref, s_ref, aw_ref, ab_ref, o_ref, g_ref):
    acc = jnp.dot(h_ref[...], w_ref[...], preferred_element_type=F32)
    for g in range(o_ref.shape[1] // V7X_LANES):
        a = acc[:, 2 * g * V7X_LANES:(2 * g + 1) * V7X_LANES]
        b = acc[:, (2 * g + 1) * V7X_LANES:(2 * g + 2) * V7X_LANES]
        o_ref[:, g * V7X_LANES:(g + 1) * V7X_LANES] = _mul_sigmoid(a, b).astype(o_ref.dtype)
    g_ref[...] = _ada_columns(s_ref, aw_ref, ab_ref)


def _inproj_rope_kernel(h_ref, w_ref, cos_ref, sin_ref, wa_ref, wb_ref, wg_ref, wo_ref,
                        o_ref, wab_ref, wgb_ref, wob_ref, *, q_tiles, k_tiles, head_dim):
    for g in range(wa_ref.shape[1] // V7X_LANES):
        src = slice(g * V7X_LANES, (g + 1) * V7X_LANES)
        wab_ref[:, 2 * g * V7X_LANES:(2 * g + 1) * V7X_LANES] = wa_ref[:, src].astype(wab_ref.dtype)
        wab_ref[:, (2 * g + 1) * V7X_LANES:(2 * g + 2) * V7X_LANES] = wb_ref[:, src].astype(wab_ref.dtype)
    wgb_ref[...] = wg_ref[...].astype(wgb_ref.dtype)
    wob_ref[...] = wo_ref[...].astype(wob_ref.dtype)
    j = pl.program_id(1)
    is_rope = j < q_tiles + k_tiles
    scale = jnp.where(j >= q_tiles, head_dim ** -0.5, 1.0).astype(F32)
    half = V7X_LANES // 2
    groups_per_head = head_dim // V7X_LANES
    acc = jnp.dot(h_ref[...], w_ref[...], preferred_element_type=F32)
    for g in range(acc.shape[1] // V7X_LANES):
        lanes = slice(g * V7X_LANES, (g + 1) * V7X_LANES)
        tl = slice((g % groups_per_head) * V7X_LANES, (g % groups_per_head + 1) * V7X_LANES)
        t = acc[:, lanes]
        roped = (t * cos_ref[:, tl] + pltpu.roll(t, half, axis=1) * sin_ref[:, tl]) * scale
        o_ref[g // groups_per_head, :, tl] = jnp.where(is_rope, roped, t).astype(o_ref.dtype)


def _in_proj(h, w_bf16, *, tm, tn, col_block0, n_col_blocks, name, rope=None, side_casts=None, glu_gate=None,
             silu=False):
    m, d = h.shape
    n_steps = (m // tm) * n_col_blocks
    in_specs = [
        pl.BlockSpec((tm, d), lambda i, j: (i, 0)),
        pl.BlockSpec((d, tn), lambda i, j: (0, j + col_block0)),
    ]
    args = [h, w_bf16]
    out_specs = [pl.BlockSpec((tm, tn), lambda i, j: (i, j))]
    out_shape = [jax.ShapeDtypeStruct((m, n_col_blocks * tn), BF16)]
    if rope is None and glu_gate is not None:
        assert side_casts is None and not silu
        cc, ada_w, ada_b, col0, n_cols = glu_gate
        gw = n_cols // n_steps
        assert gw * n_steps == n_cols and gw % V7X_LANES == 0 and col0 % gw == 0
        step = lambda i, j: i * n_col_blocks + j
        in_specs += [
            pl.BlockSpec(cc.shape, lambda i, j: (0, 0)),
            pl.BlockSpec((ada_w.shape[0], gw), lambda i, j: (0, col0 // gw + step(i, j))),
            pl.BlockSpec((1, gw), lambda i, j: (0, col0 // gw + step(i, j))),
        ]
        args += [cc, ada_w, ada_b]
        out_specs = [pl.BlockSpec((tm, tn // 2), lambda i, j: (i, j)),
                     pl.BlockSpec((cc.shape[0], gw), lambda i, j: (0, step(i, j)))]
        out_shape = [jax.ShapeDtypeStruct((m, n_col_blocks * tn // 2), BF16),
                     jax.ShapeDtypeStruct((cc.shape[0], n_cols), F32)]
        kern = _inproj_glu_kernel
    elif rope is None:
        assert side_casts is None
        kern = _inproj_silu_kernel if silu else _inproj_kernel
    else:
        assert glu_gate is None and not silu and side_casts is not None
        cos, sin, n_tokens, rope_cols = rope
        head_dim = cos.shape[1]
        tiles_per_seq = n_tokens // tm
        table = pl.BlockSpec((tm, head_dim), lambda i, j: (i % tiles_per_seq, 0))
        in_specs += [table, table]
        args += [cos, sin]
        heads_per_tile = tn // head_dim
        out_specs = [pl.BlockSpec((heads_per_tile, tm, head_dim), lambda i, j: (j, i, 0))]
        out_shape = [jax.ShapeDtypeStruct((n_col_blocks * heads_per_tile, m, head_dim), BF16)]
        w_in_f, (cb_a, cb_b, cb_g), width, w_out_f = side_casts
        rows = w_in_f.shape[0]
        assert w_out_f.shape[0] == rows and rows % n_steps == 0 and (rows // n_steps) % V7X_SUBLANES_BF16 == 0
        slab = rows // n_steps
        step_row = lambda i, j: i * n_col_blocks + j
        for cb in (cb_a, cb_b, cb_g):
            in_specs.append(pl.BlockSpec((slab, width), lambda i, j, cb=cb: (step_row(i, j), cb)))
            args.append(w_in_f)
        in_specs.append(pl.BlockSpec((slab, w_out_f.shape[1]), lambda i, j: (step_row(i, j), 0)))
        args.append(w_out_f)
        for cols in (2 * width, width, w_out_f.shape[1]):
            out_specs.append(pl.BlockSpec((slab, cols), lambda i, j: (step_row(i, j), 0)))
            out_shape.append(jax.ShapeDtypeStruct((rows, cols), BF16))
        kern = functools.partial(_inproj_rope_kernel, q_tiles=rope_cols // tn, k_tiles=rope_cols // tn,
                                 head_dim=head_dim)
    outs = pl.pallas_call(
        kern,
        grid=(m // tm, n_col_blocks),
        in_specs=in_specs,
        out_specs=out_specs,
        out_shape=out_shape,
        compiler_params=_cparams(("arbitrary", "arbitrary")),
        name=name,
    )(*args)
    return outs[0] if len(outs) == 1 else outs


def _rope_tables(n_tokens, head_dim):
    rows = n_tokens // GRID_W
    pos_r = np.repeat(np.arange(rows, dtype=np.float64), GRID_W)
    pos_c = np.tile(np.arange(GRID_W, dtype=np.float64), rows)
    n_freq = head_dim // 4
    inv_freq = ROPE_BASE ** (-np.arange(n_freq, dtype=np.float64) / n_freq)
    ang_r = pos_r[:, None] * inv_freq[None, :]
    ang_c = pos_c[:, None] * inv_freq[None, :]
    cos = np.concatenate([np.cos(ang_r), np.cos(ang_r), np.cos(ang_c), np.cos(ang_c)], axis=-1)
    sin = np.concatenate([-np.sin(ang_r), np.sin(ang_r), -np.sin(ang_c), np.sin(ang_c)], axis=-1)
    return jnp.asarray(cos, F32), jnp.asarray(sin, F32)


def _dot_tn(a, b):
    return lax.dot_general(a, b, (((0,), (0,)), ((), ())), preferred_element_type=F32)


def _dot_nt(a, b):
    return lax.dot_general(a, b, (((1,), (1,)), ((), ())), preferred_element_type=F32)


def _retention_kernel(lg_ref, q_ref, k_ref, v_ref, gt_ref, kc_ref, vc_ref, gng_ref,
                      o_ref, c0_ref, r_ref, st_ref, dm_ref, wt_ref, raw_ref, *, chunk, unroll):
    n, dh = q_ref.shape
    nc = n // chunk
    n_ctx = kc_ref.shape[0]
    head = pl.program_id(1)
    lgf = lg_ref[0, head]
    lgb = lg_ref[1, head]

    col = lax.broadcasted_iota(jnp.int32, (chunk, dh), 0).astype(F32)
    rel = (lax.broadcasted_iota(jnp.int32, (chunk, chunk), 0)
           - lax.broadcasted_iota(jnp.int32, (chunk, chunk), 1)).astype(F32)
    dm_ref[...] = (jnp.where(rel >= 0, jnp.exp(lgf * jnp.maximum(rel, 0.0)), 0.0)
                   + jnp.where(rel <= 0, jnp.exp(lgb * jnp.maximum(-rel, 0.0)), 0.0))
    wt_ref[0] = jnp.exp(lgf * (col + 1.0))
    wt_ref[1] = jnp.exp(lgb * (chunk - col))
    wt_ref[2] = jnp.exp(lgf * (chunk - 1.0 - col))
    wt_ref[3] = jnp.exp(lgb * col)
    dec = jnp.concatenate([jnp.exp(jnp.full((1, dh), lgf * chunk, F32)),
                           jnp.exp(jnp.full((1, dh), lgb * chunk, F32))], axis=1)

    def weighted_v(v, w_fwd, w_bwd):
        vf = v.astype(F32)
        return jnp.concatenate([vf * w_fwd, vf * w_bwd], axis=1).astype(BF16)

    pos_c = lax.broadcasted_iota(jnp.int32, (n_ctx, dh), 0).astype(F32)
    kc = (kc_ref[...].astype(F32) * dh ** -0.5).astype(BF16)
    c0_ref[...] = _dot_tn(kc, weighted_v(vc_ref[...], jnp.exp(lgf * (n_ctx - 1.0 - pos_c)), jnp.exp(lgb * pos_c)))
    st_ref[...] = c0_ref[...]

    def scan_body(t, carry):
        cf = t
        cb = nc - 1 - t
        rows_f = pl.ds(pl.multiple_of(cf * chunk, chunk), chunk)
        rows_b = pl.ds(pl.multiple_of(cb * chunk, chunk), chunk)
        state = st_ref[...]
        r_ref[cf, :, 0:dh] = state[:, 0:dh].astype(BF16)
        r_ref[cb, :, dh:2 * dh] = state[:, dh:2 * dh].astype(BF16)
        kv_f = _dot_tn(k_ref[rows_f, :], (v_ref[rows_f, :].astype(F32) * wt_ref[2]).astype(BF16))
        kv_b = _dot_tn(k_ref[rows_b, :], (v_ref[rows_b, :].astype(F32) * wt_ref[3]).astype(BF16))
        st_ref[...] = state * dec + jnp.concatenate([kv_f, kv_b], axis=1)
        return carry

    lax.fori_loop(0, nc, scan_body, 0, unroll=unroll)

    gng = gng_ref[...]

    def raw_out(c):
        rows = pl.ds(pl.multiple_of(c * chunk, chunk), chunk)
        q = q_ref[rows, :]
        scores = _dot_nt(q, k_ref[rows, :]) * dm_ref[...]
        o = jnp.dot(scores.astype(BF16), v_ref[rows, :], preferred_element_type=F32)
        x = jnp.dot(q, r_ref[c], preferred_element_type=F32)
        return o + x[:, 0:dh] * wt_ref[0] + x[:, dh:2 * dh] * wt_ref[1]

    def finish(c, o):
        rows = pl.ds(pl.multiple_of(c * chunk, chunk), chunk)
        mu = jnp.mean(o, axis=-1, keepdims=True)
        oc = o - mu
        var = jnp.mean(oc * oc, axis=-1, keepdims=True)
        y = oc * lax.rsqrt(var + NORM_EPS) * gng
        o_ref[rows, :] = (y * _silu(gt_ref[rows, :].astype(F32))).astype(o_ref.dtype)

    raw_ref[0] = raw_out(0)

    def out_body(c, carry):
        prev = raw_ref[(c - 1) % 2]
        raw_ref[c % 2] = raw_out(c)
        finish(c - 1, prev)
        return carry

    lax.fori_loop(1, nc, out_body, 0, unroll=unroll)
    finish(nc - 1, raw_ref[(nc - 1) % 2])


def _retention(lg, proj, kv_c, gn_g, *, batch, n, n_ctx, heads, dh):
    assert n % RET_CHUNK == 0 and proj.shape == (4 * heads, batch * n, dh)
    nc = n // RET_CHUNK
    kern = functools.partial(_retention_kernel, chunk=RET_CHUNK, unroll=2)
    return pl.pallas_call(
        kern,
        grid=(batch, heads),
        in_specs=[
            pl.BlockSpec(memory_space=pltpu.SMEM),
            pl.BlockSpec((None, n, dh), lambda b, h: (h, b, 0)),
            pl.BlockSpec((None, n, dh), lambda b, h: (heads + h, b, 0)),
            pl.BlockSpec((None, n, dh), lambda b, h: (2 * heads + h, b, 0)),
            pl.BlockSpec((None, n, dh), lambda b, h: (3 * heads + h, b, 0)),
            pl.BlockSpec((n_ctx, dh), lambda b, h: (b, h)),
            pl.BlockSpec((n_ctx, dh), lambda b, h: (b, heads + h)),
            pl.BlockSpec((1, dh), lambda b, h: (0, h)),
        ],
        out_specs=pl.BlockSpec((None, n, dh), lambda b, h: (h, b, 0)),
        out_shape=jax.ShapeDtypeStruct((heads, batch * n, dh), BF16),
        scratch_shapes=[
            pltpu.VMEM((dh, 2 * dh), F32),
            pltpu.VMEM((nc, dh, 2 * dh), BF16),
            pltpu.VMEM((dh, 2 * dh), F32),
            pltpu.VMEM((RET_CHUNK, RET_CHUNK), F32),
            pltpu.VMEM((4, RET_CHUNK, dh), F32),
            pltpu.VMEM((2, RET_CHUNK, dh), F32),
        ],
        compiler_params=_cparams(("arbitrary", "arbitrary")),
        name="retention",
    )(lg, proj, proj, proj, proj, kv_c, kv_c, gn_g)


def _conv_kernel(uin_ref, up_ref, un_ref, gc_ref, w_ref, cb_ref, lg_ref, lb_ref,
                 o_ref, u_ref, y_ref, w3_ref, s1_ref, mu_ref, rs_ref, *, tiles_per_seq, row_chunk, conv_rows):
    tn, cw = uin_ref.shape
    halo = up_ref.shape[0]
    taps = w_ref.shape[0]
    pad = taps // 2
    n_groups = cw // V7X_LANES
    il = pl.program_id(0) % tiles_per_seq

    def lane_group(val, g):
        return val[:, g * V7X_LANES:(g + 1) * V7X_LANES]

    u_prev = jnp.where(il > 0, up_ref[...].astype(F32), 0.0)
    u_next = jnp.where(il < tiles_per_seq - 1, un_ref[...].astype(F32), 0.0)
    for g in range(n_groups):
        u_ref[g, 0:halo, :] = lane_group(u_prev, g)
        u_ref[g, halo + tn:2 * halo + tn, :] = lane_group(u_next, g)
        w3_ref[g, 0:taps, :] = lane_group(w_ref[...], g)
        w3_ref[g, taps:taps + 1, :] = lane_group(cb_ref[...], g)

    def copy_body(r, carry):
        r0 = pl.multiple_of(r * row_chunk, row_chunk)
        val = uin_ref[pl.ds(r0, row_chunk), :].astype(F32)
        for g in range(n_groups):
            u_ref[g, pl.ds(halo + r0, row_chunk), :] = lane_group(val, g)
        return carry

    lax.fori_loop(0, tn // row_chunk, copy_body, 0)

    s1_ref[...] = jnp.zeros_like(s1_ref)

    def conv_body(g, carry):
        for r0 in range(0, tn, conv_rows):
            acc = jnp.broadcast_to(w3_ref[g, taps:taps + 1, :], (conv_rows, V7X_LANES))
            for t in range(taps):
                s0 = r0 + halo - pad + t
                acc = acc + u_ref[g, s0:s0 + conv_rows, :] * w3_ref[g, t:t + 1, :]
            y_ref[g, r0:r0 + conv_rows, :] = acc
            s1_ref[r0:r0 + conv_rows, :] += acc
        return carry

    lax.fori_loop(0, n_groups, conv_body, 0)

    inv_cw = 1.0 / cw
    for r0 in range(0, tn, conv_rows):
        rows = slice(r0, r0 + conv_rows)
        mu = jnp.broadcast_to(jnp.sum(s1_ref[rows, :], axis=-1, keepdims=True) * inv_cw, (conv_rows, V7X_LANES))
        sq = jnp.zeros((conv_rows, V7X_LANES), F32)
        for g in range(n_groups):
            dlt = y_ref[g, rows, :] - mu
            sq = sq + dlt * dlt
        var = jnp.sum(sq, axis=-1, keepdims=True) * inv_cw
        mu_ref[rows, :] = mu
        rs_ref[rows, :] = jnp.broadcast_to(lax.rsqrt(var + NORM_EPS), (conv_rows, V7X_LANES))

    def row_body(r, carry):
        r0 = pl.multiple_of(r * row_chunk, row_chunk)
        rows = pl.ds(r0, row_chunk)
        mu = mu_ref[rows, :]
        rs = rs_ref[rows, :]
        for g in range(n_groups):
            lanes = slice(g * V7X_LANES, (g + 1) * V7X_LANES)
            z = (y_ref[g, rows, :] - mu) * rs * lg_ref[:, lanes] + lb_ref[:, lanes]
            o_ref[rows, lanes] = (_silu(z) * gc_ref[rows, lanes].astype(F32)).astype(o_ref.dtype)
        return carry

    lax.fori_loop(0, tn // row_chunk, row_body, 0)


def _conv_branch(glu, gate, dw_w, dw_b, ln_g, ln_b, *, n, tn=512):
    m, cw = glu.shape
    halo = CONV_HALO
    assert n % tn == 0 and tn % halo == 0 and halo >= dw_w.shape[0] // 2 and cw % V7X_LANES == 0
    tiles_per_seq = n // tn
    hb = tn // halo
    last_hb = m // halo - 1
    n_groups = cw // V7X_LANES
    kern = functools.partial(_conv_kernel, tiles_per_seq=tiles_per_seq, row_chunk=16, conv_rows=64)
    main = pl.BlockSpec((tn, cw), lambda i: (i, 0))
    prev = pl.BlockSpec((halo, cw), lambda i: (jnp.maximum(i * hb - 1, 0), 0))
    nxt = pl.BlockSpec((halo, cw), lambda i: (jnp.minimum((i + 1) * hb, last_hb), 0))
    vec = lambda rows: pl.BlockSpec((rows, cw), lambda i: (0, 0))
    return pl.pallas_call(
        kern,
        grid=(m // tn,),
        in_specs=[main, prev, nxt, main, vec(dw_w.shape[0]), vec(1), vec(1), vec(1)],
        out_specs=pl.BlockSpec((tn, cw), lambda i: (i, 0)),
        out_shape=jax.ShapeDtypeStruct((m, cw), BF16),
        scratch_shapes=[
            pltpu.VMEM((n_groups, tn + 2 * halo, V7X_LANES), F32),
            pltpu.VMEM((n_groups, tn, V7X_LANES), F32),
            pltpu.VMEM((n_groups, dw_w.shape[0] + 1, V7X_LANES), F32),
            pltpu.VMEM((tn, V7X_LANES), F32),
            pltpu.VMEM((tn, V7X_LANES), F32),
            pltpu.VMEM((tn, V7X_LANES), F32),
        ],
        compiler_params=_cparams(("arbitrary",)),
        name="conv_branch",
    )(glu, glu, glu, gate, dw_w, dw_b, ln_g, ln_b)


def _outproj_kernel(r_ref, c_ref, w1_ref, w2_ref, x_ref, gate_ref, pg_ref, o_ref, rs_ref, *, tn, row_chunk):
    tm, d = o_ref.shape
    ssq = jnp.zeros((tm, V7X_LANES), F32)
    ret = jnp.concatenate([r_ref[hd] for hd in range(r_ref.shape[0])], axis=1)
    for j0 in range(0, d, tn):
        cols = slice(j0, j0 + tn)
        y = jnp.dot(ret, w1_ref[:, cols], preferred_element_type=F32)
        y = y + jnp.dot(c_ref[...], w2_ref[:, cols], preferred_element_type=F32)
        o_ref[:, cols] = y
        for g0 in range(0, tn, V7X_LANES):
            yg = y[:, g0:g0 + V7X_LANES]
            ssq = ssq + yg * yg
    var = jnp.sum(ssq, axis=-1, keepdims=True) * (1.0 / d)
    rs_ref[...] = jnp.broadcast_to(lax.rsqrt(var + NORM_EPS), (tm, V7X_LANES))

    def body(r, carry):
        rows = pl.ds(pl.multiple_of(r * row_chunk, row_chunk), row_chunk)
        rs = rs_ref[rows, :]
        for g0 in range(0, d, V7X_LANES):
            lanes = slice(g0, g0 + V7X_LANES)
            gain = gate_ref[0, :, lanes] * pg_ref[:, lanes]
            o_ref[rows, lanes] = x_ref[rows, lanes] + o_ref[rows, lanes] * rs * gain
        return carry

    lax.fori_loop(0, tm // row_chunk, body, 0)


def _out_proj(ret_b, conv_b, w_out_b, x2d, gate3, post_g, *, tm, tiles_per_batch):
    m, d = x2d.shape
    heads, _, dh = ret_b.shape
    r = heads * dh
    cw = conv_b.shape[1]
    assert r == cw and w_out_b.shape == (r + cw, d)
    kern = functools.partial(_outproj_kernel, tn=512, row_chunk=16)
    return pl.pallas_call(
        kern,
        grid=(m // tm,),
        in_specs=[
            pl.BlockSpec((heads, tm, dh), lambda i: (0, i, 0)),
            pl.BlockSpec((tm, cw), lambda i: (i, 0)),
            pl.BlockSpec((r, d), lambda i: (0, 0), pipeline_mode=pl.Buffered(1)),
            pl.BlockSpec((cw, d), lambda i: (1, 0), pipeline_mode=pl.Buffered(1)),
            pl.BlockSpec((tm, d), lambda i: (i, 0)),
            pl.BlockSpec((1, 1, d), lambda i: (i // tiles_per_batch, 0, 0)),
            pl.BlockSpec((1, d), lambda i: (0, 0)),
        ],
        out_specs=pl.BlockSpec((tm, d), lambda i: (i, 0)),
        out_shape=jax.ShapeDtypeStruct((m, d), F32),
        scratch_shapes=[pltpu.VMEM((tm, V7X_LANES), F32)],
        compiler_params=_cparams(("arbitrary",)),
        name="out_proj",
    )(ret_b, conv_b, w_out_b, w_out_b, x2d, gate3, post_g)


def kernel(x, c, ctx, c_ctx, ada_w, ada_b, pre_norm_g, post_norm_g, w_in, ret_log_decay_fwd,
           ret_log_decay_bwd, ret_gn_g, conv_dw_w, conv_dw_b, conv_ln_g, conv_ln_b, w_out):
    batch, n, d = x.shape
    n_ctx = ctx.shape[1]
    depth = ada_w.shape[0]
    r = ret_gn_g.shape[1]
    cw = conv_dw_w.shape[2]
    heads = RET_HEADS
    dh = r // heads
    assert depth == 1, "single-layer block: the context stream is never updated"
    assert w_in.shape[2] == 4 * r + 3 * cw and r == cw

    cos, sin = _rope_tables(n, dh)
    x2d = x.reshape(batch * n, d)
    ctx2d = ctx.reshape(batch * n_ctx, d)

    mod_rows = 8
    cc = jnp.concatenate([c, c_ctx[None, :], jnp.zeros((mod_rows - batch - 1, d), F32)], axis=0)
    mod2 = _ada_mod(cc, ada_w[0], ada_b, 2 * d).reshape(mod_rows, 2, d)

    ret_cols = 4 * r
    w_ret_b = w_in[0, :, :ret_cols].astype(BF16)

    tm_norm = 512
    h = _prenorm_ring(x2d, pre_norm_g, mod2, tm=tm_norm, mod_row_fn=lambda i: i // (n // tm_norm), name="prenorm")
    h_ctx = _prenorm(ctx2d, pre_norm_g, mod2, tm=tm_norm, mod_row_fn=lambda i: batch, name="prenorm_ctx")

    tm_in, tn_in = 1024, 1024
    cb0 = ret_cols // cw
    proj, w_glu_b16, w_gate_b16, w_out_b = _in_proj(
        h, w_ret_b, tm=tm_in, tn=tn_in, col_block0=0, n_col_blocks=ret_cols // tn_in, name="in_proj",
        rope=(cos, sin, n, r), side_casts=(w_in[0], (cb0, cb0 + 1, cb0 + 2), cw, w_out[0]))
    kv_c = _in_proj(h_ctx, w_ret_b, tm=batch * n_ctx, tn=tn_in, col_block0=r // tn_in,
                    n_col_blocks=2 * r // tn_in, name="in_proj_ctx")
    glu, ada_gate = _in_proj(h, w_glu_b16, tm=tm_in, tn=tn_in, col_block0=0, n_col_blocks=2 * cw // tn_in,
                             name="in_proj_glu", glu_gate=(cc, ada_w[0], ada_b, 2 * d, d))
    gate_c = _in_proj(h, w_gate_b16, tm=tm_in, tn=tn_in, col_block0=0, n_col_blocks=cw // tn_in,
                      name="in_proj_conv_gate", silu=True)

    lg = jnp.concatenate([ret_log_decay_fwd, ret_log_decay_bwd], axis=0)
    ret_b = _retention(lg, proj, kv_c, ret_gn_g, batch=batch, n=n, n_ctx=n_ctx, heads=heads, dh=dh)
    conv_b = _conv_branch(glu, gate_c, conv_dw_w[0], conv_dw_b, conv_ln_g, conv_ln_b, n=n)

    tm_out = 256
    out = _out_proj(ret_b, conv_b, w_out_b, x2d, ada_gate.reshape(mod_rows, 1, d), post_norm_g,
                    tm=tm_out, tiles_per_batch=n // tm_out)
    return out.reshape(batch, n, d)
```

```python
import functools

import numpy as np
import jax
import jax.numpy as jnp
from jax import lax
from jax.experimental import pallas as pl
from jax.experimental.pallas import tpu as pltpu

F32 = jnp.float32
BF16 = jnp.bfloat16

GRID_W = 64
RET_HEADS = 8
ROPE_BASE = 10000.0
NORM_EPS = 1e-6

V7X_LANES = 128
V7X_SUBLANES_BF16 = 16
V7X_VMEM_LIMIT_BYTES = 58 * 1024 * 1024

RET_CHUNK = 256
CONV_HALO = 16


def _cparams(semantics):
    return pltpu.CompilerParams(dimension_semantics=semantics, vmem_limit_bytes=V7X_VMEM_LIMIT_BYTES)


def _silu(x):
    h = 0.5 * x
    return h + h * jnp.tanh(h)


def _mul_sigmoid(a, b):
    ha = 0.5 * a
    return ha + ha * jnp.tanh(0.5 * b)


def _ada_columns(s_ref, w_ref, b_ref):
    s = s_ref[...]
    s = s * jax.nn.sigmoid(s)
    acc = jnp.dot(s.astype(BF16), w_ref[...].astype(BF16), preferred_element_type=F32)
    return acc + b_ref[...]


STREAM_RING_DEPTH = 3


def _ada_kernel(s_ref, w_hbm, b_ref, o_ref, wbuf, sem, *, n_steps):
    j = pl.program_id(0)
    tn = o_ref.shape[1]
    depth = wbuf.shape[0]

    def tile_copy(step, slot):
        cols = pl.ds(pl.multiple_of(step * tn, tn), tn)
        return pltpu.make_async_copy(w_hbm.at[:, cols], wbuf.at[slot], sem.at[slot])

    @pl.when(j == 0)
    def _():
        for k in range(depth - 1):
            tile_copy(k, k).start()

    @pl.when(j + (depth - 1) < n_steps)
    def _():
        nxt = j + (depth - 1)
        tile_copy(nxt, nxt % depth).start()

    slot = j % depth
    tile_copy(j, slot).wait()
    o_ref[...] = _ada_columns(s_ref, wbuf.at[slot], b_ref)


def _ada_mod(cc, ada_w, ada_b, n_cols, tn=512):
    rows, d = cc.shape
    n_steps = n_cols // tn
    assert n_cols % tn == 0 and n_steps >= STREAM_RING_DEPTH - 1
    return pl.pallas_call(
        functools.partial(_ada_kernel, n_steps=n_steps),
        grid=(n_steps,),
        in_specs=[
            pl.BlockSpec((rows, d), lambda j: (0, 0)),
            pl.BlockSpec(memory_space=pl.ANY),
            pl.BlockSpec((1, tn), lambda j: (0, j)),
        ],
        out_specs=pl.BlockSpec((rows, tn), lambda j: (0, j)),
        out_shape=jax.ShapeDtypeStruct((rows, n_cols), F32),
        scratch_shapes=[pltpu.VMEM((STREAM_RING_DEPTH, d, tn), F32),
                        pltpu.SemaphoreType.DMA((STREAM_RING_DEPTH,))],
        compiler_params=_cparams(("arbitrary",)),
        name="ada_mod",
    )(cc, ada_w, ada_b)


def _prenorm_kernel(x_ref, g_ref, mod_ref, h_ref, *, row_chunk):
    gain = g_ref[...] * (1.0 + mod_ref[0, 1:2, :])
    shift = mod_ref[0, 0:1, :]

    def body(r, carry):
        rows = pl.ds(pl.multiple_of(r * row_chunk, row_chunk), row_chunk)
        xc = x_ref[rows, :]
        var = jnp.mean(xc * xc, axis=-1, keepdims=True)
        h_ref[rows, :] = (xc * lax.rsqrt(var + NORM_EPS) * gain + shift).astype(h_ref.dtype)
        return carry

    lax.fori_loop(0, x_ref.shape[0] // row_chunk, body, 0, unroll=4)


def _prenorm_ring_kernel(x_hbm, g_ref, mod_ref, h_ref, xbuf, sem, *, row_chunk, n_steps):
    s = pl.program_id(0)
    tm = h_ref.shape[0]
    depth = xbuf.shape[0]

    def tile_copy(step, slot):
        rows = pl.ds(pl.multiple_of(step * tm, tm), tm)
        return pltpu.make_async_copy(x_hbm.at[rows, :], xbuf.at[slot], sem.at[slot])

    @pl.when(s == 0)
    def _():
        for k in range(depth - 1):
            tile_copy(k, k).start()

    @pl.when(s + (depth - 1) < n_steps)
    def _():
        nxt = s + (depth - 1)
        tile_copy(nxt, nxt % depth).start()

    slot = s % depth
    tile_copy(s, slot).wait()

    gain = g_ref[...] * (1.0 + mod_ref[0, 1:2, :])
    shift = mod_ref[0, 0:1, :]

    def body(r, carry):
        rows = pl.ds(pl.multiple_of(r * row_chunk, row_chunk), row_chunk)
        xc = xbuf[slot, rows, :]
        var = jnp.mean(xc * xc, axis=-1, keepdims=True)
        h_ref[rows, :] = (xc * lax.rsqrt(var + NORM_EPS) * gain + shift).astype(h_ref.dtype)
        return carry

    lax.fori_loop(0, tm // row_chunk, body, 0, unroll=4)


def _prenorm_ring(x2d, pre_g, mod3, *, tm, mod_row_fn, name):
    m, d = x2d.shape
    n_steps = m // tm
    assert m % tm == 0 and n_steps >= STREAM_RING_DEPTH - 1
    return pl.pallas_call(
        functools.partial(_prenorm_ring_kernel, row_chunk=16, n_steps=n_steps),
        grid=(n_steps,),
        in_specs=[
            pl.BlockSpec(memory_space=pl.ANY),
            pl.BlockSpec((1, d), lambda i: (0, 0)),
            pl.BlockSpec((1, 2, d), lambda i: (mod_row_fn(i), 0, 0)),
        ],
        out_specs=pl.BlockSpec((tm, d), lambda i: (i, 0)),
        out_shape=jax.ShapeDtypeStruct((m, d), BF16),
        scratch_shapes=[pltpu.VMEM((STREAM_RING_DEPTH, tm, d), F32),
                        pltpu.SemaphoreType.DMA((STREAM_RING_DEPTH,))],
        compiler_params=_cparams(("arbitrary",)),
        name=name,
    )(x2d, pre_g, mod3)


def _prenorm(x2d, pre_g, mod3, *, tm, mod_row_fn, name):
    m, d = x2d.shape
    return pl.pallas_call(
        functools.partial(_prenorm_kernel, row_chunk=16),
        grid=(m // tm,),
        in_specs=[
            pl.BlockSpec((tm, d), lambda i: (i, 0)),
            pl.BlockSpec((1, d), lambda i: (0, 0)),
            pl.BlockSpec((1, 2, d), lambda i: (mod_row_fn(i), 0, 0)),
        ],
        out_specs=pl.BlockSpec((tm, d), lambda i: (i, 0)),
        out_shape=jax.ShapeDtypeStruct((m, d), BF16),
        compiler_params=_cparams(("arbitrary",)),
        name=name,
    )(x2d, pre_g, mod3)


def _inproj_kernel(h_ref, w_ref, o_ref):
    o_ref[...] = jnp.dot(h_ref[...], w_ref[...], preferred_element_type=F32).astype(o_ref.dtype)


def _inproj_silu_kernel(h_ref, w_ref, o_ref):
    o_ref[...] = _silu(jnp.dot(h_ref[...], w_ref[...], preferred_element_type=F32)).astype(o_ref.dtype)


def _inproj_glu_kernel(h_ref, w_ref, s_ref, aw_ref, ab_ref, o_ref, g_ref):
    acc = jnp.dot(h_ref[...], w_ref[...], preferred_element_type=F32)
    for g in range(o_ref.shape[1] // V7X_LANES):
        a = acc[:, 2 * g * V7X_LANES:(2 * g + 1) * V7X_LANES]
        b = acc[:, (2 * g + 1) * V7X_LANES:(2 * g + 2) * V7X_LANES]
        o_ref[:, g * V7X_LANES:(g + 1) * V7X_LANES] = _mul_sigmoid(a, b).astype(o_ref.dtype)
    g_ref[...] = _ada_columns(s_ref, aw_ref, ab_ref)


def _inproj_rope_kernel(h_ref, w_ref, cos_ref, sin_ref, wa_ref, wb_ref, wg_ref, wo_ref,
                        o_ref, wab_ref, wgb_ref, wob_ref, *, q_tiles, k_tiles, head_dim):
    for g in range(wa_ref.shape[1] // V7X_LANES):
        src = slice(g * V7X_LANES, (g + 1) * V7X_LANES)
        wab_ref[:, 2 * g * V7X_LANES:(2 * g + 1) * V7X_LANES] = wa_ref[:, src].astype(wab_ref.dtype)
        wab_ref[:, (2 * g + 1) * V7X_LANES:(2 * g + 2) * V7X_LANES] = wb_ref[:, src].astype(wab_ref.dtype)
    wgb_ref[...] = wg_ref[...].astype(wgb_ref.dtype)
    wob_ref[...] = wo_ref[...].astype(wob_ref.dtype)
    j = pl.program_id(1)
    is_rope = j < q_tiles + k_tiles
    scale = jnp.where(j >= q_tiles, head_dim ** -0.5, 1.0).astype(F32)
    half = V7X_LANES // 2
    groups_per_head = head_dim // V7X_LANES
    acc = jnp.dot(h_ref[...], w_ref[...], preferred_element_type=F32)
    for g in range(acc.shape[1] // V7X_LANES):
        lanes = slice(g * V7X_LANES, (g + 1) * V7X_LANES)
        tl = slice((g % groups_per_head) * V7X_LANES, (g % groups_per_head + 1) * V7X_LANES)
        t = acc[:, lanes]
        roped = (t * cos_ref[:, tl] + pltpu.roll(t, half, axis=1) * sin_ref[:, tl]) * scale
        o_ref[g // groups_per_head, :, tl] = jnp.where(is_rope, roped, t).astype(o_ref.dtype)


def _in_proj(h, w_bf16, *, tm, tn, col_block0, n_col_blocks, name, rope=None, side_casts=None, glu_gate=None,
             silu=False):
    m, d = h.shape
    n_steps = (m // tm) * n_col_blocks
    in_specs = [
        pl.BlockSpec((tm, d), lambda i, j: (i, 0)),
        pl.BlockSpec((d, tn), lambda i, j: (0, j + col_block0)),
    ]
    args = [h, w_bf16]
    out_specs = [pl.BlockSpec((tm, tn), lambda i, j: (i, j))]
    out_shape = [jax.ShapeDtypeStruct((m, n_col_blocks * tn), BF16)]
    if rope is None and glu_gate is not None:
        assert side_casts is None and not silu
        cc, ada_w, ada_b, col0, n_cols = glu_gate
        gw = n_cols // n_steps
        assert gw * n_steps == n_cols and gw % V7X_LANES == 0 and col0 % gw == 0
        step = lambda i, j: i * n_col_blocks + j
        in_specs += [
            pl.BlockSpec(cc.shape, lambda i, j: (0, 0)),
            pl.BlockSpec((ada_w.shape[0], gw), lambda i, j: (0, col0 // gw + step(i, j))),
            pl.BlockSpec((1, gw), lambda i, j: (0, col0 // gw + step(i, j))),
        ]
        args += [cc, ada_w, ada_b]
        out_specs = [pl.BlockSpec((tm, tn // 2), lambda i, j: (i, j)),
                     pl.BlockSpec((cc.shape[0], gw), lambda i, j: (0, step(i, j)))]
        out_shape = [jax.ShapeDtypeStruct((m, n_col_blocks * tn // 2), BF16),
                     jax.ShapeDtypeStruct((cc.shape[0], n_cols), F32)]
        kern = _inproj_glu_kernel
    elif rope is None:
        assert side_casts is None
        kern = _inproj_silu_kernel if silu else _inproj_kernel
    else:
        assert glu_gate is None and not silu and side_casts is not None
        cos, sin, n_tokens, rope_cols = rope
        head_dim = cos.shape[1]
        tiles_per_seq = n_tokens // tm
        table = pl.BlockSpec((tm, head_dim), lambda i, j: (i % tiles_per_seq, 0))
        in_specs += [table, table]
        args += [cos, sin]
        heads_per_tile = tn // head_dim
        out_specs = [pl.BlockSpec((heads_per_tile, tm, head_dim), lambda i, j: (j, i, 0))]
        out_shape = [jax.ShapeDtypeStruct((n_col_blocks * heads_per_tile, m, head_dim), BF16)]
        w_in_f, (cb_a, cb_b, cb_g), width, w_out_f = side_casts
        rows = w_in_f.shape[0]
        assert w_out_f.shape[0] == rows and rows % n_steps == 0 and (rows // n_steps) % V7X_SUBLANES_BF16 == 0
        slab = rows // n_steps
        step_row = lambda i, j: i * n_col_blocks + j
        for cb in (cb_a, cb_b, cb_g):
            in_specs.append(pl.BlockSpec((slab, width), lambda i, j, cb=cb: (step_row(i, j), cb)))
            args.append(w_in_f)
        in_specs.append(pl.BlockSpec((slab, w_out_f.shape[1]), lambda i, j: (step_row(i, j), 0)))
        args.append(w_out_f)
        for cols in (2 * width, width, w_out_f.shape[1]):
            out_specs.append(pl.BlockSpec((slab, cols), lambda i, j: (step_row(i, j), 0)))
            out_shape.append(jax.ShapeDtypeStruct((rows, cols), BF16))
        kern = functools.partial(_inproj_rope_kernel, q_tiles=rope_cols // tn, k_tiles=rope_cols // tn,
                                 head_dim=head_dim)
    outs = pl.pallas_call(
        kern,
        grid=(m // tm, n_col_blocks),
        in_specs=in_specs,
        out_specs=out_specs,
        out_shape=out_shape,
        compiler_params=_cparams(("arbitrary", "arbitrary")),
        name=name,
    )(*args)
    return outs[0] if len(outs) == 1 else outs


def _rope_tables(n_tokens, head_dim):
    rows = n_tokens // GRID_W
    pos_r = np.repeat(np.arange(rows, dtype=np.float64), GRID_W)
    pos_c = np.tile(np.arange(GRID_W, dtype=np.float64), rows)
    n_freq = head_dim // 4
    inv_freq = ROPE_BASE ** (-np.arange(n_freq, dtype=np.float64) / n_freq)
    ang_r = pos_r[:, None] * inv_freq[None, :]
    ang_c = pos_c[:, None] * inv_freq[None, :]
    cos = np.concatenate([np.cos(ang_r), np.cos(ang_r), np.cos(ang_c), np.cos(ang_c)], axis=-1)
    sin = np.concatenate([-np.sin(ang_r), np.sin(ang_r), -np.sin(ang_c), np.sin(ang_c)], axis=-1)
    return jnp.asarray(cos, F32), jnp.asarray(sin, F32)


def _dot_tn(a, b):
    return lax.dot_general(a, b, (((0,), (0,)), ((), ())), preferred_element_type=F32)


def _dot_nt(a, b):
    return lax.dot_general(a, b, (((1,), (1,)), ((), ())), preferred_element_type=F32)


def _retention_kernel(lg_ref, q_ref, k_ref, v_ref, gt_ref, kc_ref, vc_ref, gng_ref,
                      o_ref, c0_ref, r_ref, st_ref, dm_ref, wt_ref, raw_ref, *, chunk, unroll):
    n, dh = q_ref.shape
    nc = n // chunk
    n_ctx = kc_ref.shape[0]
    head = pl.program_id(1)
    lgf = lg_ref[0, head]
    lgb = lg_ref[1, head]

    col = lax.broadcasted_iota(jnp.int32, (chunk, dh), 0).astype(F32)
    rel = (lax.broadcasted_iota(jnp.int32, (chunk, chunk), 0)
           - lax.broadcasted_iota(jnp.int32, (chunk, chunk), 1)).astype(F32)
    dm_ref[...] = (jnp.where(rel >= 0, jnp.exp(lgf * jnp.maximum(rel, 0.0)), 0.0)
                   + jnp.where(rel <= 0, jnp.exp(lgb * jnp.maximum(-rel, 0.0)), 0.0))
    wt_ref[0] = jnp.exp(lgf * (col + 1.0))
    wt_ref[1] = jnp.exp(lgb * (chunk - col))
    wt_ref[2] = jnp.exp(lgf * (chunk - 1.0 - col))
    wt_ref[3] = jnp.exp(lgb * col)
    dec = jnp.concatenate([jnp.exp(jnp.full((1, dh), lgf * chunk, F32)),
                           jnp.exp(jnp.full((1, dh), lgb * chunk, F32))], axis=1)

    def weighted_v(v, w_fwd, w_bwd):
        vf = v.astype(F32)
        return jnp.concatenate([vf * w_fwd, vf * w_bwd], axis=1).astype(BF16)

    pos_c = lax.broadcasted_iota(jnp.int32, (n_ctx, dh), 0).astype(F32)
    kc = (kc_ref[...].astype(F32) * dh ** -0.5).astype(BF16)
    c0_ref[...] = _dot_tn(kc, weighted_v(vc_ref[...], jnp.exp(lgf * (n_ctx - 1.0 - pos_c)), jnp.exp(lgb * pos_c)))
    st_ref[...] = c0_ref[...]

    def scan_body(t, carry):
        cf = t
        cb = nc - 1 - t
        rows_f = pl.ds(pl.multiple_of(cf * chunk, chunk), chunk)
        rows_b = pl.ds(pl.multiple_of(cb * chunk, chunk), chunk)
        state = st_ref[...]
        r_ref[cf, :, 0:dh] = state[:, 0:dh].astype(BF16)
        r_ref[cb, :, dh:2 * dh] = state[:, dh:2 * dh].astype(BF16)
        kv_f = _dot_tn(k_ref[rows_f, :], (v_ref[rows_f, :].astype(F32) * wt_ref[2]).astype(BF16))
        kv_b = _dot_tn(k_ref[rows_b, :], (v_ref[rows_b, :].astype(F32) * wt_ref[3]).astype(BF16))
        st_ref[...] = state * dec + jnp.concatenate([kv_f, kv_b], axis=1)
        return carry

    lax.fori_loop(0, nc, scan_body, 0, unroll=unroll)

    gng = gng_ref[...]

    def raw_out(c):
        rows = pl.ds(pl.multiple_of(c * chunk, chunk), chunk)
        q = q_ref[rows, :]
        scores = _dot_nt(q, k_ref[rows, :]) * dm_ref[...]
        o = jnp.dot(scores.astype(BF16), v_ref[rows, :], preferred_element_type=F32)
        x = jnp.dot(q, r_ref[c], preferred_element_type=F32)
        return o + x[:, 0:dh] * wt_ref[0] + x[:, dh:2 * dh] * wt_ref[1]

    def finish(c, o):
        rows = pl.ds(pl.multiple_of(c * chunk, chunk), chunk)
        mu = jnp.mean(o, axis=-1, keepdims=True)
        oc = o - mu
        var = jnp.mean(oc * oc, axis=-1, keepdims=True)
        y = oc * lax.rsqrt(var + NORM_EPS) * gng
        o_ref[rows, :] = (y * _silu(gt_ref[rows, :].astype(F32))).astype(o_ref.dtype)

    raw_ref[0] = raw_out(0)

    def out_body(c, carry):
        prev = raw_ref[(c - 1) % 2]
        raw_ref[c % 2] = raw_out(c)
        finish(c - 1, prev)
        return carry

    lax.fori_loop(1, nc, out_body, 0, unroll=unroll)
    finish(nc - 1, raw_ref[(nc - 1) % 2])


def _retention(lg, proj, kv_c, gn_g, *, batch, n, n_ctx, heads, dh):
    assert n % RET_CHUNK == 0 and proj.shape == (4 * heads, batch * n, dh)
    nc = n // RET_CHUNK
    kern = functools.partial(_retention_kernel, chunk=RET_CHUNK, unroll=2)
    return pl.pallas_call(
        kern,
        grid=(batch, heads),
        in_specs=[
            pl.BlockSpec(memory_space=pltpu.SMEM),
            pl.BlockSpec((None, n, dh), lambda b, h: (h, b, 0)),
            pl.BlockSpec((None, n, dh), lambda b, h: (heads + h, b, 0)),
            pl.BlockSpec((None, n, dh), lambda b, h: (2 * heads + h, b, 0)),
            pl.BlockSpec((None, n, dh), lambda b, h: (3 * heads + h, b, 0)),
            pl.BlockSpec((n_ctx, dh), lambda b, h: (b, h)),
            pl.BlockSpec((n_ctx, dh), lambda b, h: (b, heads + h)),
            pl.BlockSpec((1, dh), lambda b, h: (0, h)),
        ],
        out_specs=pl.BlockSpec((None, n, dh), lambda b, h: (h, b, 0)),
        out_shape=jax.ShapeDtypeStruct((heads, batch * n, dh), BF16),
        scratch_shapes=[
            pltpu.VMEM((dh, 2 * dh), F32),
            pltpu.VMEM((nc, dh, 2 * dh), BF16),
            pltpu.VMEM((dh, 2 * dh), F32),
            pltpu.VMEM((RET_CHUNK, RET_CHUNK), F32),
            pltpu.VMEM((4, RET_CHUNK, dh), F32),
            pltpu.VMEM((2, RET_CHUNK, dh), F32),
        ],
        compiler_params=_cparams(("arbitrary", "arbitrary")),
        name="retention",
    )(lg, proj, proj, proj, proj, kv_c, kv_c, gn_g)


def _conv_kernel(uin_ref, up_ref, un_ref, gc_ref, w_ref, cb_ref, lg_ref, lb_ref,
                 o_ref, u_ref, y_ref, w3_ref, s1_ref, mu_ref, rs_ref, *, tiles_per_seq, row_chunk, conv_rows):
    tn, cw = uin_ref.shape
    halo = up_ref.shape[0]
    taps = w_ref.shape[0]
    pad = taps // 2
    n_groups = cw // V7X_LANES
    il = pl.program_id(0) % tiles_per_seq

    def lane_group(val, g):
        return val[:, g * V7X_LANES:(g + 1) * V7X_LANES]

    u_prev = jnp.where(il > 0, up_ref[...].astype(F32), 0.0)
    u_next = jnp.where(il < tiles_per_seq - 1, un_ref[...].astype(F32), 0.0)
    for g in range(n_groups):
        u_ref[g, 0:halo, :] = lane_group(u_prev, g)
        u_ref[g, halo + tn:2 * halo + tn, :] = lane_group(u_next, g)
        w3_ref[g, 0:taps, :] = lane_group(w_ref[...], g)
        w3_ref[g, taps:taps + 1, :] = lane_group(cb_ref[...], g)

    def copy_body(r, carry):
        r0 = pl.multiple_of(r * row_chunk, row_chunk)
        val = uin_ref[pl.ds(r0, row_chunk), :].astype(F32)
        for g in range(n_groups):
            u_ref[g, pl.ds(halo + r0, row_chunk), :] = lane_group(val, g)
        return carry

    lax.fori_loop(0, tn // row_chunk, copy_body, 0)

    s1_ref[...] = jnp.zeros_like(s1_ref)

    def conv_body(g, carry):
        for r0 in range(0, tn, conv_rows):
            acc = jnp.broadcast_to(w3_ref[g, taps:taps + 1, :], (conv_rows, V7X_LANES))
            for t in range(taps):
                s0 = r0 + halo - pad + t
                acc = acc + u_ref[g, s0:s0 + conv_rows, :] * w3_ref[g, t:t + 1, :]
            y_ref[g, r0:r0 + conv_rows, :] = acc
            s1_ref[r0:r0 + conv_rows, :] += acc
        return carry

    lax.fori_loop(0, n_groups, conv_body, 0)

    inv_cw = 1.0 / cw
    for r0 in range(0, tn, conv_rows):
        rows = slice(r0, r0 + conv_rows)
        mu = jnp.broadcast_to(jnp.sum(s1_ref[rows, :], axis=-1, keepdims=True) * inv_cw, (conv_rows, V7X_LANES))
        sq = jnp.zeros((conv_rows, V7X_LANES), F32)
        for g in range(n_groups):
            dlt = y_ref[g, rows, :] - mu
            sq = sq + dlt * dlt
        var = jnp.sum(sq, axis=-1, keepdims=True) * inv_cw
        mu_ref[rows, :] = mu
        rs_ref[rows, :] = jnp.broadcast_to(lax.rsqrt(var + NORM_EPS), (conv_rows, V7X_LANES))

    def row_body(r, carry):
        r0 = pl.multiple_of(r * row_chunk, row_chunk)
        rows = pl.ds(r0, row_chunk)
        mu = mu_ref[rows, :]
        rs = rs_ref[rows, :]
        for g in range(n_groups):
            lanes = slice(g * V7X_LANES, (g + 1) * V7X_LANES)
            z = (y_ref[g, rows, :] - mu) * rs * lg_ref[:, lanes] + lb_ref[:, lanes]
            o_ref[rows, lanes] = (_silu(z) * gc_ref[rows, lanes].astype(F32)).astype(o_ref.dtype)
        return carry

    lax.fori_loop(0, tn // row_chunk, row_body, 0)


def _conv_branch(glu, gate, dw_w, dw_b, ln_g, ln_b, *, n, tn=512):
    m, cw = glu.shape
    halo = CONV_HALO
    assert n % tn == 0 and tn % halo == 0 and halo >= dw_w.shape[0] // 2 and cw % V7X_LANES == 0
    tiles_per_seq = n // tn
    hb = tn // halo
    last_hb = m // halo - 1
    n_groups = cw // V7X_LANES
    kern = functools.partial(_conv_kernel, tiles_per_seq=tiles_per_seq, row_chunk=16, conv_rows=64)
    main = pl.BlockSpec((tn, cw), lambda i: (i, 0))
    prev = pl.BlockSpec((halo, cw), lambda i: (jnp.maximum(i * hb - 1, 0), 0))
    nxt = pl.BlockSpec((halo, cw), lambda i: (jnp.minimum((i + 1) * hb, last_hb), 0))
    vec = lambda rows: pl.BlockSpec((rows, cw), lambda i: (0, 0))
    return pl.pallas_call(
        kern,
        grid=(m // tn,),
        in_specs=[main, prev, nxt, main, vec(dw_w.shape[0]), vec(1), vec(1), vec(1)],
        out_specs=pl.BlockSpec((tn, cw), lambda i: (i, 0)),
        out_shape=jax.ShapeDtypeStruct((m, cw), BF16),
        scratch_shapes=[
            pltpu.VMEM((n_groups, tn + 2 * halo, V7X_LANES), F32),
            pltpu.VMEM((n_groups, tn, V7X_LANES), F32),
            pltpu.VMEM((n_groups, dw_w.shape[0] + 1, V7X_LANES), F32),
            pltpu.VMEM((tn, V7X_LANES), F32),
            pltpu.VMEM((tn, V7X_LANES), F32),
            pltpu.VMEM((tn, V7X_LANES), F32),
        ],
        compiler_params=_cparams(("arbitrary",)),
        name="conv_branch",
    )(glu, glu, glu, gate, dw_w, dw_b, ln_g, ln_b)


def _outproj_kernel(r_ref, c_ref, w1_ref, w2_ref, x_ref, gate_ref, pg_ref, o_ref, rs_ref, *, tn, row_chunk):
    tm, d = o_ref.shape
    ssq = jnp.zeros((tm, V7X_LANES), F32)
    ret = jnp.concatenate([r_ref[hd] for hd in range(r_ref.shape[0])], axis=1)
    for j0 in range(0, d, tn):
        cols = slice(j0, j0 + tn)
        y = jnp.dot(ret, w1_ref[:, cols], preferred_element_type=F32)
        y = y + jnp.dot(c_ref[...], w2_ref[:, cols], preferred_element_type=F32)
        o_ref[:, cols] = y
        for g0 in range(0, tn, V7X_LANES):
            yg = y[:, g0:g0 + V7X_LANES]
            ssq = ssq + yg * yg
    var = jnp.sum(ssq, axis=-1, keepdims=True) * (1.0 / d)
    rs_ref[...] = jnp.broadcast_to(lax.rsqrt(var + NORM_EPS), (tm, V7X_LANES))

    def body(r, carry):
        rows = pl.ds(pl.multiple_of(r * row_chunk, row_chunk), row_chunk)
        rs = rs_ref[rows, :]
        for g0 in range(0, d, V7X_LANES):
            lanes = slice(g0, g0 + V7X_LANES)
            gain = gate_ref[0, :, lanes] * pg_ref[:, lanes]
            o_ref[rows, lanes] = x_ref[rows, lanes] + o_ref[rows, lanes] * rs * gain
        return carry

    lax.fori_loop(0, tm // row_chunk, body, 0)


def _out_proj(ret_b, conv_b, w_out_b, x2d, gate3, post_g, *, tm, tiles_per_batch):
    m, d = x2d.shape
    heads, _, dh = ret_b.shape
    r = heads * dh
    cw = conv_b.shape[1]
    assert r == cw and w_out_b.shape == (r + cw, d)
    kern = functools.partial(_outproj_kernel, tn=512, row_chunk=16)
    return pl.pallas_call(
        kern,
        grid=(m // tm,),
        in_specs=[
            pl.BlockSpec((heads, tm, dh), lambda i: (0, i, 0)),
            pl.BlockSpec((tm, cw), lambda i: (i, 0)),
            pl.BlockSpec((r, d), lambda i: (0, 0), pipeline_mode=pl.Buffered(1)),
            pl.BlockSpec((cw, d), lambda i: (1, 0), pipeline_mode=pl.Buffered(1)),
            pl.BlockSpec((tm, d), lambda i: (i, 0)),
            pl.BlockSpec((1, 1, d), lambda i: (i // tiles_per_batch, 0, 0)),
            pl.BlockSpec((1, d), lambda i: (0, 0)),
        ],
        out_specs=pl.BlockSpec((tm, d), lambda i: (i, 0)),
        out_shape=jax.ShapeDtypeStruct((m, d), F32),
        scratch_shapes=[pltpu.VMEM((tm, V7X_LANES), F32)],
        compiler_params=_cparams(("arbitrary",)),
        name="out_proj",
    )(ret_b, conv_b, w_out_b, w_out_b, x2d, gate3, post_g)


def kernel(x, c, ctx, c_ctx, ada_w, ada_b, pre_norm_g, post_norm_g, w_in, ret_log_decay_fwd,
           ret_log_decay_bwd, ret_gn_g, conv_dw_w, conv_dw_b, conv_ln_g, conv_ln_b, w_out):
    batch, n, d = x.shape
    n_ctx = ctx.shape[1]
    depth = ada_w.shape[0]
    r = ret_gn_g.shape[1]
    cw = conv_dw_w.shape[2]
    heads = RET_HEADS
    dh = r // heads
    assert depth == 1, "single-layer block: the context stream is never updated"
    assert w_in.shape[2] == 4 * r + 3 * cw and r == cw

    cos, sin = _rope_tables(n, dh)
    x2d = x.reshape(batch * n, d)
    ctx2d = ctx.reshape(batch * n_ctx, d)

    mod_rows = 8
    cc = jnp.concatenate([c, c_ctx[None, :], jnp.zeros((mod_rows - batch - 1, d), F32)], axis=0)
    mod2 = _ada_mod(cc, ada_w[0], ada_b, 2 * d).reshape(mod_rows, 2, d)

    ret_cols = 4 * r
    w_ret_b = w_in[0, :, :ret_cols].astype(BF16)

    tm_norm = 512
    h = _prenorm_ring(x2d, pre_norm_g, mod2, tm=tm_norm, mod_row_fn=lambda i: i // (n // tm_norm), name="prenorm")
    h_ctx = _prenorm(ctx2d, pre_norm_g, mod2, tm=tm_norm, mod_row_fn=lambda i: batch, name="prenorm_ctx")

    tm_in, tn_in = 1024, 1024
    cb0 = ret_cols // cw
    proj, w_glu_b16, w_gate_b16, w_out_b = _in_proj(
        h, w_ret_b, tm=tm_in, tn=tn_in, col_block0=0, n_col_blocks=ret_cols // tn_in, name="in_proj",
        rope=(cos, sin, n, r), side_casts=(w_in[0], (cb0, cb0 + 1, cb0 + 2), cw, w_out[0]))
    kv_c = _in_proj(h_ctx, w_ret_b, tm=batch * n_ctx, tn=tn_in, col_block0=r // tn_in,
                    n_col_blocks=2 * r // tn_in, name="in_proj_ctx")
    glu, ada_gate = _in_proj(h, w_glu_b16, tm=tm_in, tn=tn_in, col_block0=0, n_col_blocks=2 * cw // tn_in,
                             name="in_proj_glu", glu_gate=(cc, ada_w[0], ada_b, 2 * d, d))
    gate_c = _in_proj(h, w_gate_b16, tm=tm_in, tn=tn_in, col_block0=0, n_col_blocks=cw // tn_in,
                      name="in_proj_conv_gate", silu=True)

    lg = jnp.concatenate([ret_log_decay_fwd, ret_log_decay_bwd], axis=0)
    ret_b = _retention(lg, proj, kv_c, ret_gn_g, batch=batch, n=n, n_ctx=n_ctx, heads=heads, dh=dh)
    conv_b = _conv_branch(glu, gate_c, conv_dw_w[0], conv_dw_b, conv_ln_g, conv_ln_b, n=n)

    tm_out = 256
    out = _out_proj(ret_b, conv_b, w_out_b, x2d, ada_gate.reshape(mod_rows, 1, d), post_norm_g,
                    tm=tm_out, tiles_per_batch=n // tm_out)
    return out.reshape(batch, n, d)
```
